```python
import math
import jax, jax.numpy as jnp
from jax import lax
import numpy as np

D_MODEL = 1024
BATCH = 2
SEQ = 8192
DEPTH = 1
DEC_BATCH = 128
DEC_SEQ = 1
PAST_LEN = 2048
PAGE_SIZE = 128

ATT_HEADS = 16
HEAD_DIM = 64
ATT_WIDTH = ATT_HEADS * HEAD_DIM
Q_BLOCK = 128
FGATE_BIAS_INIT = 3.0
SSM_HEADS = 16
SSM_HEAD_DIM = 64
SSM_WIDTH = SSM_HEADS * SSM_HEAD_DIM
SSM_GROUPS = 2
HEADS_PER_GROUP = SSM_HEADS // SSM_GROUPS
D_STATE = 128
CONV_WIDTH = 4
CONV_DIM = SSM_WIDTH + 2 * SSM_GROUPS * D_STATE
SSD_CHUNK = 128
DT_MIN = 0.001
DT_MAX = 0.1
MIX_WIDTH = ATT_WIDTH + SSM_WIDTH
IN_PROJ_WIDTH = 3 * ATT_WIDTH + ATT_HEADS + SSM_WIDTH + CONV_DIM + SSM_HEADS
N_EXPERTS = 32
TOP_K = 4
D_FF = D_MODEL
SWIGLU_LIMIT = 7.0
SWIGLU_ALPHA = 1.702
MOE_BLOCK = 128
NORM_EPS = 1e-5

kernel_name = 'hymba_fox_ssd_moe_adaln_step'


def rmsnorm(x, w):
    xf = x.astype(jnp.float32)
    y = xf * lax.rsqrt(jnp.mean(xf * xf, axis=-1, keepdims=True) + NORM_EPS)
    return (y * w.astype(jnp.float32)).astype(x.dtype)


def adaln_modulation(c, ada_w, ada_b):
    m = jax.nn.silu(c) @ ada_w + ada_b
    return jnp.split(m[:, None, :], 6, axis=-1)


def split_in_proj(u):
    sizes = (ATT_WIDTH, ATT_WIDTH, ATT_WIDTH, ATT_HEADS, SSM_WIDTH, CONV_DIM, SSM_HEADS)
    return jnp.split(u, np.cumsum(sizes)[:-1].tolist(), axis=-1)


def fox_attention_prompt(q, k, v, lf):
    b, L = q.shape[:2]
    nb = L // Q_BLOCK
    F = jnp.cumsum(lf, axis=1).transpose(0, 2, 1)
    k_pos = jnp.arange(L)
    scale = HEAD_DIM ** -0.5
    q_blocks = q.reshape(b, nb, Q_BLOCK, ATT_HEADS, HEAD_DIM).transpose(1, 0, 2, 3, 4)
    F_blocks = F.reshape(b, ATT_HEADS, nb, Q_BLOCK).transpose(2, 0, 1, 3)

    def one_block(args):
        q_blk, F_blk, i = args
        s = jnp.einsum('bqhd,bkhd->bhqk', q_blk, k).astype(jnp.float32) * scale
        s = s + F_blk[..., :, None] - F[:, :, None, :]
        q_pos = i * Q_BLOCK + jnp.arange(Q_BLOCK)
        s = jnp.where(k_pos[None, :] <= q_pos[:, None], s, -jnp.inf)
        p = jax.nn.softmax(s, axis=-1).astype(v.dtype)
        return jnp.einsum('bhqk,bkhd->bqhd', p, v)

    out = lax.map(one_block, (q_blocks, F_blocks, jnp.arange(nb)))
    return out.transpose(1, 0, 2, 3, 4).reshape(b, L, ATT_WIDTH)


def fox_attention_sample(q, k_new, v_new, lf_new, cache_k, cache_v, cache_lf, page_table):
    bd, S = q.shape[:2]
    P = page_table.shape[1] * cache_k.shape[1]
    k_all = jnp.concatenate([cache_k[page_table].reshape(bd, P, ATT_HEADS, HEAD_DIM), k_new.astype(cache_k.dtype)], axis=1)
    v_all = jnp.concatenate([cache_v[page_table].reshape(bd, P, ATT_HEADS, HEAD_DIM), v_new.astype(cache_v.dtype)], axis=1)
    lf_all = jnp.concatenate([cache_lf[page_table].reshape(bd, P, ATT_HEADS).astype(jnp.float32), lf_new], axis=1)
    F = jnp.cumsum(lf_all, axis=1).transpose(0, 2, 1)
    s = jnp.einsum('bqhd,bkhd->bhqk', q.astype(k_all.dtype), k_all).astype(jnp.float32) * HEAD_DIM ** -0.5
    s = s + F[:, :, P:, None] - F[:, :, None, :]
    mask = jnp.arange(P + S)[None, :] <= P + jnp.arange(S)[:, None]
    s = jnp.where(mask, s, -jnp.inf)
    p = jax.nn.softmax(s, axis=-1).astype(v_all.dtype)
    return jnp.einsum('bhqk,bkhd->bqhd', p, v_all).reshape(bd, S, ATT_WIDTH)


def causal_depthwise_conv(x_padded, conv_w, conv_b):
    y = lax.conv_general_dilated(x_padded, conv_w[:, None, :].astype(x_padded.dtype), window_strides=(1,),
                                 padding='VALID', dimension_numbers=('NWC', 'WIO', 'NWC'),
                                 feature_group_count=CONV_DIM)
    return jax.nn.silu(y + conv_b)


def ssm_inputs(xbc, dt_raw, dt_bias, A_log):
    b, L = xbc.shape[:2]
    xs, Bm, Cm = jnp.split(xbc, [SSM_WIDTH, SSM_WIDTH + SSM_GROUPS * D_STATE], axis=-1)
    xs = xs.reshape(b, L, SSM_HEADS, SSM_HEAD_DIM).astype(jnp.float32)
    Bh = jnp.repeat(Bm.reshape(b, L, SSM_GROUPS, D_STATE), HEADS_PER_GROUP, axis=2).astype(jnp.float32)
    Ch = jnp.repeat(Cm.reshape(b, L, SSM_GROUPS, D_STATE), HEADS_PER_GROUP, axis=2).astype(jnp.float32)
    dt = jax.nn.softplus(dt_raw.astype(jnp.float32) + dt_bias.astype(jnp.float32))
    A = -jnp.exp(A_log.astype(jnp.float32))
    return xs, Bh, Ch, dt, A


def ssd_chunked(xs, dt, A, Bh, Ch):
    b, L = xs.shape[:2]
    nc = L // SSD_CHUNK
    xdt = (xs * dt[..., None]).reshape(b, nc, SSD_CHUNK, SSM_HEADS, SSM_HEAD_DIM)
    Bc = Bh.reshape(b, nc, SSD_CHUNK, SSM_HEADS, D_STATE)
    Cc = Ch.reshape(b, nc, SSD_CHUNK, SSM_HEADS, D_STATE)
    a_cum = jnp.cumsum((dt * A).reshape(b, nc, SSD_CHUNK, SSM_HEADS).transpose(0, 1, 3, 2), axis=-1)
    causal = jnp.tril(jnp.ones((SSD_CHUNK, SSD_CHUNK), dtype=bool))
    decay = jnp.exp(jnp.where(causal, a_cum[..., :, None] - a_cum[..., None, :], -jnp.inf))
    scores = jnp.einsum('bclhn,bcshn->bchls', Cc, Bc) * decay
    y_diag = jnp.einsum('bchls,bcshp->bclhp', scores, xdt)
    to_end = jnp.exp(a_cum[..., -1:] - a_cum)
    chunk_states = jnp.einsum('bclhn,bchl,bclhp->bchpn', Bc, to_end, xdt)
    chunk_decay = jnp.exp(a_cum[..., -1])

    def carry_state(h, inp):
        st, dec = inp
        return dec[..., None, None] * h + st, h

    h0 = jnp.zeros((b, SSM_HEADS, SSM_HEAD_DIM, D_STATE), jnp.float32)
    h_final, h_in = lax.scan(carry_state, h0, (chunk_states.swapaxes(0, 1), chunk_decay.swapaxes(0, 1)))
    y_off = jnp.einsum('bclhn,cbhpn,bchl->bclhp', Cc, h_in, jnp.exp(a_cum))
    return (y_diag + y_off).reshape(b, L, SSM_HEADS, SSM_HEAD_DIM), h_final


def ssd_recurrent(xs, dt, A, Bh, Ch, h0):
    def step(h, inp):
        x_t, dt_t, B_t, C_t = inp
        h = jnp.exp(dt_t * A)[..., None, None] * h + (x_t * dt_t[..., None])[..., None] * B_t[:, :, None, :]
        return h, jnp.einsum('bhpn,bhn->bhp', h, C_t)

    h_final, ys = lax.scan(step, h0, (xs.swapaxes(0, 1), dt.swapaxes(0, 1), Bh.swapaxes(0, 1), Ch.swapaxes(0, 1)))
    return ys.swapaxes(0, 1), h_final


def ssm_output(y, xs, z, D_skip, ssm_norm_w):
    b, L = y.shape[:2]
    y = (y + D_skip.astype(jnp.float32)[:, None] * xs).reshape(b, L, SSM_WIDTH)
    gated = y * jax.nn.silu(z.astype(jnp.float32))
    return rmsnorm(gated, ssm_norm_w).astype(z.dtype)


def mixer_prompt(h, w_in, b_f, conv_w, conv_b, dt_bias, A_log, D_skip, ssm_norm_w, w_out):
    b, L, _ = h.shape
    q, k, v, f_raw, z, xbc, dt_raw = split_in_proj(h @ w_in)
    q = q.reshape(b, L, ATT_HEADS, HEAD_DIM)
    k = k.reshape(b, L, ATT_HEADS, HEAD_DIM)
    v = v.reshape(b, L, ATT_HEADS, HEAD_DIM)
    lf = jax.nn.log_sigmoid((f_raw + b_f).astype(jnp.float32))
    att = fox_attention_prompt(q, k, v, lf)
    conv_state = xbc[:, L - (CONV_WIDTH - 1):]
    x_pad = jnp.pad(xbc, ((0, 0), (CONV_WIDTH - 1, 0), (0, 0)))
    xs, Bh, Ch, dt, A = ssm_inputs(causal_depthwise_conv(x_pad, conv_w, conv_b), dt_raw, dt_bias, A_log)
    y, ssm_state = ssd_chunked(xs, dt, A, Bh, Ch)
    ssm = ssm_output(y, xs, z, D_skip, ssm_norm_w)
    out = jnp.concatenate([att, ssm.astype(att.dtype)], axis=-1) @ w_out
    return out, (k, v, lf.astype(h.dtype), conv_state, ssm_state.astype(h.dtype))


def mixer_sample(h, cache_k, cache_v, cache_lf, state_conv, state_ssm, page_table,
                 w_in, b_f, conv_w, conv_b, dt_bias, A_log, D_skip, ssm_norm_w, w_out):
    bd, S, _ = h.shape
    q, k, v, f_raw, z, xbc, dt_raw = split_in_proj(h @ w_in)
    q = q.reshape(bd, S, ATT_HEADS, HEAD_DIM)
    k = k.reshape(bd, S, ATT_HEADS, HEAD_DIM)
    v = v.reshape(bd, S, ATT_HEADS, HEAD_DIM)
    lf = jax.nn.log_sigmoid((f_raw + b_f).astype(jnp.float32))
    att = fox_attention_sample(q, k, v, lf, cache_k, cache_v, cache_lf, page_table)
    x_pad = jnp.concatenate([state_conv.astype(xbc.dtype), xbc], axis=1)
    new_conv = x_pad[:, x_pad.shape[1] - (CONV_WIDTH - 1):]
    xs, Bh, Ch, dt, A = ssm_inputs(causal_depthwise_conv(x_pad, conv_w, conv_b), dt_raw, dt_bias, A_log)
    y, ssm_state = ssd_recurrent(xs, dt, A, Bh, Ch, state_ssm.astype(jnp.float32))
    ssm = ssm_output(y, xs, z, D_skip, ssm_norm_w)
    out = jnp.concatenate([att.astype(h.dtype), ssm.astype(h.dtype)], axis=-1) @ w_out
    return out, (k.astype(cache_k.dtype), v.astype(cache_v.dtype), lf.astype(cache_lf.dtype),
                 new_conv.astype(state_conv.dtype), ssm_state.astype(state_ssm.dtype))


def moe_ffn(h, router_w, router_b, w_gate_up, b_gate_up, w_down, b_down):
    N, D = h.shape
    logits = (h @ router_w).astype(jnp.float32) + router_b.astype(jnp.float32)
    top_val, top_idx = lax.top_k(logits, TOP_K)
    gates = jax.nn.softmax(top_val, axis=-1)
    n_assign = N * TOP_K
    flat_e = top_idx.reshape(-1)
    order = jnp.argsort(flat_e)
    sorted_e = flat_e[order]
    counts = jnp.bincount(flat_e, length=N_EXPERTS)
    padded = (counts + MOE_BLOCK - 1) // MOE_BLOCK * MOE_BLOCK
    pad_end = jnp.cumsum(padded)
    pad_start = pad_end - padded
    start = jnp.cumsum(counts) - counts
    dest = pad_start[sorted_e] + jnp.arange(n_assign) - start[sorted_e]
    n_blocks = (n_assign + N_EXPERTS * (MOE_BLOCK - 1) + MOE_BLOCK - 1) // MOE_BLOCK
    token_sorted = order // TOP_K
    slot_token = jnp.full((n_blocks * MOE_BLOCK,), N, jnp.int32).at[dest].set(token_sorted)
    h_pad = jnp.concatenate([h, jnp.zeros((1, D), h.dtype)], axis=0)
    x_blocks = h_pad[slot_token].reshape(n_blocks, MOE_BLOCK, D)
    block_expert = jnp.minimum(jnp.searchsorted(pad_end, jnp.arange(n_blocks) * MOE_BLOCK, side='right'), N_EXPERTS - 1)

    def expert_block(args):
        xb, e = args
        gu = xb @ w_gate_up[e] + b_gate_up[e]
        g, u = jnp.split(gu, 2, axis=-1)
        g = jnp.minimum(g, SWIGLU_LIMIT)
        u = jnp.clip(u, -SWIGLU_LIMIT, SWIGLU_LIMIT)
        act = (u + 1) * (g * jax.nn.sigmoid(SWIGLU_ALPHA * g))
        return act @ w_down[e] + b_down[e]

    out = lax.map(expert_block, (x_blocks, block_expert)).reshape(n_blocks * MOE_BLOCK, D)
    y_assign = out[dest] * gates.reshape(-1)[order][:, None].astype(out.dtype)
    return jax.ops.segment_sum(y_assign, token_sorted, num_segments=N)


def trunk_layer(x, c, mixer, ada_w, ada_b, norm1_w, norm2_w, router_w, router_b, w_gate_up, b_gate_up, w_down, b_down):
    sh1, sc1, g1, sh2, sc2, g2 = adaln_modulation(c, ada_w, ada_b)
    y, state = mixer(rmsnorm(x, norm1_w) * (1 + sc1) + sh1)
    x = x + g1 * y
    b, L, d = x.shape
    h = (rmsnorm(x, norm2_w) * (1 + sc2) + sh2).reshape(b * L, d)
    x = x + g2 * moe_ffn(h, router_w, router_b, w_gate_up, b_gate_up, w_down, b_down).reshape(b, L, d)
    return x, state


def setup_inputs(seed: int = 0) -> dict:
    key = jax.random.key(seed)
    ks = jax.random.split(key, 32)

    def nrm(i, shape, scale):
        return jax.random.normal(ks[i], shape, jnp.float32) * scale

    n_pages = PAST_LEN // PAGE_SIZE
    n_pool = (DEC_BATCH * n_pages * 5 + 3) // 4
    page_table = jax.random.permutation(ks[0], n_pool)[:DEC_BATCH * n_pages].reshape(DEC_BATCH, n_pages).astype(jnp.int32)
    u = jax.random.uniform(ks[1], (DEPTH, SSM_HEADS), jnp.float32)
    dt0 = jnp.exp(u * (math.log(DT_MAX) - math.log(DT_MIN)) + math.log(DT_MIN))
    return {
        'x_prompt': nrm(2, (BATCH, SEQ, D_MODEL), 1.0),
        'x_sample': nrm(3, (DEC_BATCH, DEC_SEQ, D_MODEL), 1.0),
        'c_prompt': nrm(4, (BATCH, D_MODEL), 1.0),
        'c_sample': nrm(5, (DEC_BATCH, D_MODEL), 1.0),
        'cache_k': nrm(6, (DEPTH, n_pool, PAGE_SIZE, ATT_HEADS, HEAD_DIM), 1.0),
        'cache_v': nrm(7, (DEPTH, n_pool, PAGE_SIZE, ATT_HEADS, HEAD_DIM), 1.0),
        'cache_lf': jax.nn.log_sigmoid(nrm(8, (DEPTH, n_pool, PAGE_SIZE, ATT_HEADS), 1.0) + FGATE_BIAS_INIT),
        'state_conv': nrm(9, (DEPTH, DEC_BATCH, CONV_WIDTH - 1, CONV_DIM), 1.0),
        'state_ssm': nrm(10, (DEPTH, DEC_BATCH, SSM_HEADS, SSM_HEAD_DIM, D_STATE), 0.5),
        'page_table': page_table,
        'ada_w': nrm(11, (DEPTH, D_MODEL, 6 * D_MODEL), 0.5 * D_MODEL ** -0.5),
        'ada_b': nrm(12, (DEPTH, 6 * D_MODEL), 0.01),
        'norm1_w': 1.0 + nrm(13, (DEPTH, D_MODEL), 0.01),
        'w_in': nrm(14, (DEPTH, D_MODEL, IN_PROJ_WIDTH), D_MODEL ** -0.5),
        'b_f': FGATE_BIAS_INIT + nrm(15, (DEPTH, ATT_HEADS), 0.5),
        'conv_w': nrm(16, (DEPTH, CONV_WIDTH, CONV_DIM), CONV_WIDTH ** -0.5),
        'conv_b': nrm(17, (DEPTH, CONV_DIM), 0.01),
        'dt_bias': dt0 + jnp.log(-jnp.expm1(-dt0)),
        'A_log': jnp.log(jax.random.uniform(ks[18], (DEPTH, SSM_HEADS), jnp.float32, 1.0, 16.0)),
        'D_skip': 1.0 + nrm(19, (DEPTH, SSM_HEADS), 0.01),
        'ssm_norm_w': 1.0 + nrm(20, (DEPTH, SSM_WIDTH), 0.01),
        'w_out': nrm(21, (DEPTH, MIX_WIDTH, D_MODEL), MIX_WIDTH ** -0.5),
        'norm2_w': 1.0 + nrm(22, (DEPTH, D_MODEL), 0.01),
        'router_w': nrm(23, (DEPTH, D_MODEL, N_EXPERTS), D_MODEL ** -0.5),
        'router_b': nrm(24, (DEPTH, N_EXPERTS), 0.01),
        'w_gate_up': nrm(25, (DEPTH, N_EXPERTS, D_MODEL, 2 * D_FF), D_MODEL ** -0.5),
        'b_gate_up': nrm(26, (DEPTH, N_EXPERTS, 2 * D_FF), 0.01),
        'w_down': nrm(27, (DEPTH, N_EXPERTS, D_FF, D_MODEL), D_FF ** -0.5),
        'b_down': nrm(28, (DEPTH, N_EXPERTS, D_MODEL), 0.01),
        'final_norm_w': 1.0 + nrm(29, (D_MODEL,), 0.01),
    }


def reference(x_prompt, x_sample, c_prompt, c_sample, cache_k, cache_v, cache_lf, state_conv, state_ssm, page_table,
              ada_w, ada_b, norm1_w, w_in, b_f, conv_w, conv_b, dt_bias, A_log, D_skip, ssm_norm_w, w_out,
              norm2_w, router_w, router_b, w_gate_up, b_gate_up, w_down, b_down, final_norm_w):
    xp, xs = x_prompt, x_sample
    new_p, new_s = [], []
    for l in range(DEPTH):
        mix = (w_in[l], b_f[l], conv_w[l], conv_b[l], dt_bias[l], A_log[l], D_skip[l], ssm_norm_w[l], w_out[l])
        ffn = (ada_w[l], ada_b[l], norm1_w[l], norm2_w[l], router_w[l], router_b[l],
               w_gate_up[l], b_gate_up[l], w_down[l], b_down[l])
        xp, sp = trunk_layer(xp, c_prompt, lambda h: mixer_prompt(h, *mix), *ffn)
        xs, ss = trunk_layer(xs, c_sample, lambda h: mixer_sample(h, cache_k[l], cache_v[l], cache_lf[l], state_conv[l],
                                                                  state_ssm[l], page_table, *mix), *ffn)
        new_p.append(sp)
        new_s.append(ss)

    def stack(states, i):
        return jnp.stack([s[i] for s in states])

    y_prompt = rmsnorm(xp, final_norm_w)
    y_sample = rmsnorm(xs, final_norm_w)
    return (y_prompt, y_sample,
            stack(new_p, 0), stack(new_p, 1), stack(new_p, 2), stack(new_p, 3), stack(new_p, 4),
            stack(new_s, 0), stack(new_s, 1), stack(new_s, 2), stack(new_s, 3), stack(new_s, 4))
```

```python
import functools
import math

import numpy as np
import jax
import jax.numpy as jnp
from jax import lax
from jax.experimental import pallas as pl
from jax.experimental.pallas import tpu as pltpu

F32 = jnp.float32
BF16 = jnp.bfloat16
HIGHEST = lax.Precision.HIGHEST

D_MODEL = 1024
ATT_HEADS = 16
HEAD_DIM = 64
ATT_WIDTH = ATT_HEADS * HEAD_DIM
SSM_HEADS = 16
SSM_HEAD_DIM = 64
SSM_WIDTH = SSM_HEADS * SSM_HEAD_DIM
SSM_GROUPS = 2
D_STATE = 128
CONV_WIDTH = 4
CONV_DIM = SSM_WIDTH + 2 * SSM_GROUPS * D_STATE
SSD_CHUNK = 128
N_EXPERTS = 32
TOP_K = 4
D_FF = D_MODEL
SWIGLU_LIMIT = 7.0
SWIGLU_ALPHA = 1.702
NORM_EPS = 1e-5

LANES = 128
SMALL_W = LANES
DT_COL = ATT_HEADS
NEG_BIG = -1e30
VMEM_LIMIT = 48 * 1024 * 1024


def _cparams(*sem):
    return pltpu.CompilerParams(dimension_semantics=sem, vmem_limit_bytes=VMEM_LIMIT)


def _silu(x):
    return x * jax.nn.sigmoid(x)


def _softplus(x):
    return jnp.maximum(x, 0.0) + jnp.log(1.0 + jnp.exp(-jnp.abs(x)))


def _log_sigmoid(x):
    return -_softplus(-x)


def _rmsnorm_rows(x, w):
    var = jnp.mean(x * x, axis=-1, keepdims=True)
    return x * lax.rsqrt(var + NORM_EPS) * w


def _split3_bf16(x):
    hi = x.astype(BF16)
    r = x - hi.astype(F32)
    mid = r.astype(BF16)
    lo = (r - mid.astype(F32)).astype(BF16)
    return hi, mid, lo


def _mod_kernel(c_ref, w_ref, b_ref, o_ref):
    s = _silu(c_ref[...]).astype(BF16)
    o_ref[...] = jnp.dot(s, w_ref[...].astype(BF16), preferred_element_type=F32) + b_ref[...]


def _modulation(c_all, ada_w, ada_b):
    rows = c_all.shape[0]
    n_out = ada_w.shape[1]
    tn = D_MODEL
    return pl.pallas_call(
        _mod_kernel,
        grid=(n_out // tn,),
        in_specs=[pl.BlockSpec((rows, D_MODEL), lambda j: (0, 0)),
                  pl.BlockSpec((D_MODEL, tn), lambda j: (0, j)),
                  pl.BlockSpec((1, tn), lambda j: (0, j))],
        out_specs=pl.BlockSpec((rows, tn), lambda j: (0, j)),
        out_shape=jax.ShapeDtypeStruct((rows, n_out), F32),
        compiler_params=_cparams("arbitrary"),
        name="adaln_mod",
    )(c_all, ada_w, ada_b.reshape(1, n_out))


def _inproj_kernel(x_ref, sh_ref, sc_ref, nw_ref, wq_ref, wk_ref, wv_ref, wz_ref, wx_ref, ws_ref,
                   qb_ref, kb_ref, vb_ref, k_ref, v_ref, z_ref, xbc_ref, sm_ref):
    h = _rmsnorm_rows(x_ref[0], nw_ref[...]) * (1.0 + sc_ref[0]) + sh_ref[0]
    hb = h.astype(BF16)
    q = jnp.dot(hb, wq_ref[...], preferred_element_type=F32)
    qb_ref[0] = (q * (HEAD_DIM ** -0.5)).astype(BF16)
    k = jnp.dot(hb, wk_ref[...], preferred_element_type=F32)
    k_ref[0] = k
    kb_ref[0] = k.astype(BF16)
    v = jnp.dot(hb, wv_ref[...], preferred_element_type=F32)
    v_ref[0] = v
    vb_ref[0] = v.astype(BF16)
    z_ref[0] = jnp.dot(hb, wz_ref[...], preferred_element_type=F32).astype(BF16)
    xbc_ref[0] = jnp.dot(hb, wx_ref[...], preferred_element_type=F32)
    sm_ref[0] = jnp.dot(h, ws_ref[...], precision=HIGHEST, preferred_element_type=F32)


def _in_proj(x, sh, sc, norm_w, wts, tm):
    B, L, _ = x.shape
    per_row = sh.shape[1] != 1
    mod_spec = (pl.BlockSpec((1, tm, D_MODEL), lambda b, i: (b, i, 0)) if per_row
                else pl.BlockSpec((1, 1, D_MODEL), lambda b, i: (b, 0, 0)))
    wq, wk, wv, wz, wx, ws = wts

    def wspec(w):
        return pl.BlockSpec(w.shape, lambda b, i: (0, 0))

    def ospec(width):
        return pl.BlockSpec((1, tm, width), lambda b, i: (b, i, 0))

    def oshape(width, dt):
        return jax.ShapeDtypeStruct((B, L, width), dt)

    return pl.pallas_call(
        _inproj_kernel,
        grid=(B, L // tm),
        in_specs=[pl.BlockSpec((1, tm, D_MODEL), lambda b, i: (b, i, 0)), mod_spec, mod_spec,
                  pl.BlockSpec((1, D_MODEL), lambda b, i: (0, 0)),
                  wspec(wq), wspec(wk), wspec(wv), wspec(wz), wspec(wx), wspec(ws)],
        out_specs=[ospec(ATT_WIDTH), ospec(ATT_WIDTH), ospec(ATT_WIDTH), ospec(ATT_WIDTH), ospec(ATT_WIDTH),
                   ospec(SSM_WIDTH), ospec(CONV_DIM), ospec(SMALL_W)],
        out_shape=[oshape(ATT_WIDTH, BF16), oshape(ATT_WIDTH, BF16), oshape(ATT_WIDTH, BF16),
                   oshape(ATT_WIDTH, F32), oshape(ATT_WIDTH, F32),
                   oshape(SSM_WIDTH, BF16), oshape(CONV_DIM, F32), oshape(SMALL_W, F32)],
        compiler_params=_cparams("arbitrary", "arbitrary"),
        name="in_proj",
    )(x, sh, sc, norm_w.reshape(1, D_MODEL), wq, wk, wv, wz, wx, ws)


def _bias_selectors():
    sq = np.zeros((3, SMALL_W, ATT_HEADS * LANES), np.float32)
    sk = np.zeros((3, SMALL_W, ATT_HEADS * LANES), np.float32)
    oq = np.zeros((1, ATT_HEADS * LANES), np.float32)
    ok = np.zeros((1, ATT_HEADS * LANES), np.float32)
    for h in range(ATT_HEADS):
        base = h * LANES + (HEAD_DIM if h % 2 == 0 else 0)
        for part in range(3):
            sq[part, h, base + part] = 1.0
            ok[0, base + part] = 1.0
            sk[part, h, base + 3 + part] = -1.0
            oq[0, base + 3 + part] = 1.0
    return sq, sk, oq, ok


def _prep_kernel(qb_ref, kb_ref, sm_ref, bf_ref, sq_ref, sk_ref, oq_ref, ok_ref,
                 lf_ref, qp_ref, kp_ref, carry_ref):
    i = pl.program_id(1)
    tl = qb_ref.shape[1]

    @pl.when(i == 0)
    def _():
        carry_ref[...] = jnp.zeros_like(carry_ref)

    lf = _log_sigmoid(sm_ref[0] + bf_ref[...])
    lf_ref[0] = lf[:, :ATT_HEADS]
    row = lax.broadcasted_iota(jnp.int32, (tl, tl), 0)
    col = lax.broadcasted_iota(jnp.int32, (tl, tl), 1)
    tri = (col <= row).astype(F32)
    fcum = jnp.dot(tri, lf, precision=HIGHEST, preferred_element_type=F32) + carry_ref[0:1, :]
    carry_ref[0:1, :] = fcum[tl - 1:tl, :]
    parts = _split3_bf16(fcum)
    augq = oq_ref[...]
    augk = ok_ref[...]
    for p in range(3):
        augq = augq + jnp.dot(parts[p], sq_ref[p], preferred_element_type=F32)
        augk = augk + jnp.dot(parts[p], sk_ref[p], preferred_element_type=F32)
    lane = lax.broadcasted_iota(jnp.int32, (tl, LANES), 1)
    low = lane < HEAD_DIM
    for pair in range(ATT_HEADS // 2):
        qpair = qb_ref[0, :, pair * LANES:(pair + 1) * LANES]
        kpair = kb_ref[0, :, pair * LANES:(pair + 1) * LANES]
        for hh in range(2):
            h = 2 * pair + hh
            keep = low if hh == 0 else jnp.logical_not(low)
            sl = slice(h * LANES, (h + 1) * LANES)
            qp_ref[0, :, sl] = jnp.where(keep, qpair, augq[:, sl].astype(BF16))
            kp_ref[0, :, sl] = jnp.where(keep, kpair, augk[:, sl].astype(BF16))


def _attn_prep(qb, kb, small, b_f, tl):
    B, L, _ = qb.shape
    sq, sk, oq, ok = _bias_selectors()
    wide = ATT_HEADS * LANES
    row_spec = lambda w: pl.BlockSpec((1, tl, w), lambda b, i: (b, i, 0))
    full = lambda a: pl.BlockSpec(a.shape, lambda b, i: (0,) * a.ndim)
    sq, sk = jnp.asarray(sq, BF16), jnp.asarray(sk, BF16)
    oq, ok = jnp.asarray(oq), jnp.asarray(ok)
    bf2 = jnp.pad(b_f.reshape(1, ATT_HEADS), ((0, 0), (0, SMALL_W - ATT_HEADS)))
    return pl.pallas_call(
        _prep_kernel,
        grid=(B, L // tl),
        in_specs=[row_spec(ATT_WIDTH), row_spec(ATT_WIDTH), row_spec(SMALL_W), full(bf2),
                  full(sq), full(sk), full(oq), full(ok)],
        out_specs=[row_spec(ATT_HEADS), row_spec(wide), row_spec(wide)],
        out_shape=[jax.ShapeDtypeStruct((B, L, ATT_HEADS), F32),
                   jax.ShapeDtypeStruct((B, L, wide), BF16),
                   jax.ShapeDtypeStruct((B, L, wide), BF16)],
        scratch_shapes=[pltpu.VMEM((8, LANES), F32)],
        compiler_params=_cparams("arbitrary", "arbitrary"),
        name="attn_prep",
    )(qb, kb, small, bf2, sq, sk, oq, ok)


def _flash_kernel(qi_ref, ki_ref, qp_ref, kp_ref, vb_ref, o_ref, m_ref, l_ref, acc_ref, *, tq, tk):
    t = pl.program_id(2)
    qi = qi_ref[t]
    ki = ki_ref[t]
    last = ((qi + 1) * tq - 1) // tk

    @pl.when(ki == 0)
    def _():
        m_ref[...] = jnp.full_like(m_ref, NEG_BIG)
        l_ref[...] = jnp.zeros_like(l_ref)
        acc_ref[...] = jnp.zeros_like(acc_ref)

    def step(masked):
        v = vb_ref[0]
        if masked:
            qpos = qi * tq + lax.broadcasted_iota(jnp.int32, (tq, tk), 0)
            kpos = ki * tk + lax.broadcasted_iota(jnp.int32, (tq, tk), 1)
            visible = kpos <= qpos
        for hh in range(2):
            q = qp_ref[0, :, hh * LANES:(hh + 1) * LANES]
            k = kp_ref[0, :, hh * LANES:(hh + 1) * LANES]
            s = lax.dot_general(q, k, (((1,), (1,)), ((), ())), preferred_element_type=F32)
            if masked:
                s = jnp.where(visible, s, NEG_BIG)
            m_prev = m_ref[hh]
            m_new = jnp.maximum(m_prev, jnp.max(s, axis=1, keepdims=True))
            alpha = jnp.exp(m_prev - m_new)
            p = jnp.exp(s - pltpu.repeat(m_new, tk // LANES, axis=1))
            l_ref[hh] = alpha * l_ref[hh] + jnp.sum(p, axis=1, keepdims=True)
            acc_ref[hh] = alpha * acc_ref[hh] + jnp.dot(p.astype(BF16), v, preferred_element_type=F32)
            m_ref[hh] = m_new

    crosses = (ki + 1) * tk - 1 > qi * tq

    @pl.when(crosses)
    def _():
        step(True)

    @pl.when(jnp.logical_not(crosses))
    def _():
        step(False)

    @pl.when(ki == last)
    def _():
        lane = lax.broadcasted_iota(jnp.int32, (tq, LANES), 1)
        o0 = acc_ref[0] / l_ref[0]
        o1 = acc_ref[1] / l_ref[1]
        o_ref[0] = jnp.where(lane < HEAD_DIM, o0, o1).astype(o_ref.dtype)


def _flash_attention(qp, kp, vb, tq, tk):
    B, L, _ = vb.shape
    pairs = ATT_HEADS // 2
    qs, ks = [], []
    for qi in range(L // tq):
        for ki in range(((qi + 1) * tq - 1) // tk + 1):
            qs.append(qi)
            ks.append(ki)
    qi_tab = jnp.asarray(np.array(qs, np.int32))
    ki_tab = jnp.asarray(np.array(ks, np.int32))
    grid_spec = pltpu.PrefetchScalarGridSpec(
        num_scalar_prefetch=2,
        grid=(B, pairs, len(qs)),
        in_specs=[pl.BlockSpec((1, tq, 2 * LANES), lambda b, p, t, qt, kt: (b, qt[t], p)),
                  pl.BlockSpec((1, tk, 2 * LANES), lambda b, p, t, qt, kt: (b, kt[t], p)),
                  pl.BlockSpec((1, tk, LANES), lambda b, p, t, qt, kt: (b, kt[t], p))],
        out_specs=pl.BlockSpec((1, tq, LANES), lambda b, p, t, qt, kt: (b, qt[t], p)),
        scratch_shapes=[pltpu.VMEM((2, tq, LANES), F32), pltpu.VMEM((2, tq, LANES), F32),
                        pltpu.VMEM((2, tq, LANES), F32)],
    )
    return pl.pallas_call(
        functools.partial(_flash_kernel, tq=tq, tk=tk),
        grid_spec=grid_spec,
        out_shape=jax.ShapeDtypeStruct((B, L, ATT_WIDTH), BF16),
        compiler_params=_cparams("arbitrary", "arbitrary", "arbitrary"),
        name="fox_flash",
    )(qi_tab, ki_tab, qp, kp, vb)


def _head_expander():
    e = np.zeros((SMALL_W, SSM_WIDTH), np.float32)
    for h in range(SSM_HEADS):
        e[DT_COL + h, h * SSM_HEAD_DIM:(h + 1) * SSM_HEAD_DIM] = 1.0
    return e


def _expand_heads(vals, e_bf16):
    hi = vals.astype(BF16)
    lo = (vals - hi.astype(F32)).astype(BF16)
    return (jnp.dot(hi, e_bf16, preferred_element_type=F32)
            + jnp.dot(lo, e_bf16, preferred_element_type=F32))


def _conv_silu_rows(rows, cw_ref, cb_ref):
    acc = cb_ref[...] + cw_ref[CONV_WIDTH - 1:CONV_WIDTH, :] * rows[0]
    for j in range(1, CONV_WIDTH):
        acc = acc + cw_ref[CONV_WIDTH - 1 - j:CONV_WIDTH - j, :] * rows[j]
    return _silu(acc)


def _ssd_kernel(xbc_ref, sm_ref, z_ref, cw_ref, cb_ref, dtb_ref, alog_ref, e_ref, dx_ref, nw_ref,
                y_ref, st_ref, buf_ref, ht_ref):
    c = pl.program_id(1)
    nc = pl.num_programs(1)
    Q = SSD_CHUNK
    halo = 8

    @pl.when(c == 0)
    def _():
        buf_ref[0:halo, :] = jnp.zeros((halo, CONV_DIM), F32)
        ht_ref[...] = jnp.zeros_like(ht_ref)

    @pl.when(c > 0)
    def _():
        buf_ref[0:halo, :] = buf_ref[Q:Q + halo, :]

    buf_ref[halo:halo + Q, :] = xbc_ref[0]
    xc = _conv_silu_rows([buf_ref[halo - j:halo - j + Q, :] for j in range(CONV_WIDTH)], cw_ref, cb_ref)
    xs = xc[:, :SSM_WIDTH]
    e = e_ref[...]

    dt = _softplus(sm_ref[0] + dtb_ref[...])
    a = dt * (-jnp.exp(alog_ref[...]))
    row = lax.broadcasted_iota(jnp.int32, (Q, Q), 0)
    col = lax.broadcasted_iota(jnp.int32, (Q, Q), 1)
    causal = col <= row
    acum = jnp.dot(causal.astype(F32), a, precision=HIGHEST, preferred_element_type=F32)
    acum_t = acum.T
    dt_x = _expand_heads(dt, e)
    acum_x = _expand_heads(acum, e)
    last_x = acum_x[Q - 1:Q, :]
    xdt = xs * dt_x
    xdt_b = xdt.astype(BF16)
    x_end = (xdt * jnp.exp(last_x - acum_x)).astype(BF16)
    grow = jnp.exp(acum_x)
    cdecay = jnp.exp(last_x)

    lane = lax.broadcasted_iota(jnp.int32, (Q, LANES), 1)
    low = lane < SSM_HEAD_DIM
    hpg = SSM_HEADS // SSM_GROUPS
    gw = hpg * SSM_HEAD_DIM
    y_parts = []
    for g in range(SSM_GROUPS):
        bg = xc[:, SSM_WIDTH + g * D_STATE:SSM_WIDTH + (g + 1) * D_STATE].astype(BF16)
        cg = xc[:, SSM_WIDTH + (SSM_GROUPS + g) * D_STATE:SSM_WIDTH + (SSM_GROUPS + g + 1) * D_STATE].astype(BF16)
        scores = lax.dot_general(cg, bg, (((1,), (1,)), ((), ())), preferred_element_type=F32)
        gs = slice(g * gw, (g + 1) * gw)
        h_prev = ht_ref[:, gs]
        y_off = jnp.dot(cg, h_prev.astype(BF16), preferred_element_type=F32) * grow[:, gs]
        ht_ref[:, gs] = h_prev * cdecay[:, gs] + lax.dot_general(
            bg, x_end[:, gs], (((0,), (0,)), ((), ())), preferred_element_type=F32)
        for pr in range(hpg // 2):
            pair_lo = g * gw + pr * LANES
            xpair = xdt_b[:, pair_lo:pair_lo + LANES]
            halves = []
            for hh in range(2):
                h = g * hpg + 2 * pr + hh
                decay = jnp.where(causal, jnp.exp(acum[:, DT_COL + h:DT_COL + h + 1]
                                                  - acum_t[DT_COL + h:DT_COL + h + 1, :]), 0.0)
                halves.append(jnp.dot((scores * decay).astype(BF16), xpair, preferred_element_type=F32))
            y_parts.append(jnp.where(low, halves[0], halves[1]) + y_off[:, pr * LANES:(pr + 1) * LANES])
    y = jnp.concatenate(y_parts, axis=1) + dx_ref[...] * xs
    gated = y * _silu(z_ref[0].astype(F32))
    y_ref[0] = _rmsnorm_rows(gated, nw_ref[...]).astype(y_ref.dtype)

    @pl.when(c == nc - 1)
    def _():
        st_ref[0] = ht_ref[...]


def _ssm_params(dt_bias, A_log, D_skip):
    pad = (DT_COL, SMALL_W - DT_COL - SSM_HEADS)
    dtb = jnp.pad(dt_bias.astype(F32), pad).reshape(1, SMALL_W)
    alog = jnp.pad(A_log.astype(F32), pad).reshape(1, SMALL_W)
    dx = jnp.repeat(D_skip.astype(F32), SSM_HEAD_DIM).reshape(1, SSM_WIDTH)
    return dtb, alog, dx


def _ssd_prompt(xbc, small, z, conv_w, conv_b, dtb, alog, dx, e, ssm_norm_w):
    B, L, _ = xbc.shape
    Q = SSD_CHUNK
    full = lambda a: pl.BlockSpec(a.shape, lambda b, c: (0,) * a.ndim)
    row_spec = lambda w: pl.BlockSpec((1, Q, w), lambda b, c: (b, c, 0))
    cb = conv_b.reshape(1, CONV_DIM)
    nw = ssm_norm_w.reshape(1, SSM_WIDTH)
    return pl.pallas_call(
        _ssd_kernel,
        grid=(B, L // Q),
        in_specs=[row_spec(CONV_DIM), row_spec(SMALL_W), row_spec(SSM_WIDTH),
                  full(conv_w), full(cb), full(dtb), full(alog), full(e), full(dx), full(nw)],
        out_specs=[row_spec(SSM_WIDTH), pl.BlockSpec((1, D_STATE, SSM_WIDTH), lambda b, c: (b, 0, 0))],
        out_shape=[jax.ShapeDtypeStruct((B, L, SSM_WIDTH), BF16),
                   jax.ShapeDtypeStruct((B, D_STATE, SSM_WIDTH), F32)],
        scratch_shapes=[pltpu.VMEM((Q + 8, CONV_DIM), F32), pltpu.VMEM((D_STATE, SSM_WIDTH), F32)],
        compiler_params=_cparams("arbitrary", "arbitrary"),
        name="ssd_prompt",
    )(xbc, small, z, conv_w, cb, dtb, alog, e, dx, nw)


def _decode_attn_kernel(pt_ref, q_ref, kn_ref, vn_ref, sm_ref, bf_ref, ck_ref, cv_ref, clf_ref,
                        o_ref, lfo_ref, m_ref, l_ref, acc_ref, carry_ref):
    pg = pl.program_id(1)
    npg = pl.num_programs(1)
    H = ATT_HEADS
    sub = lax.broadcasted_iota(jnp.int32, (H, ATT_WIDTH), 0)
    lane_head = lax.broadcasted_iota(jnp.int32, (H, ATT_WIDTH), 1) // HEAD_DIM
    own = sub == lane_head
    q_rows = jnp.where(own, jnp.broadcast_to(q_ref[0].astype(F32), (H, ATT_WIDTH)), 0.0)
    q_blk = q_rows.astype(BF16)

    @pl.when(pg == 0)
    def _():
        lf_new = _log_sigmoid(sm_ref[0] + bf_ref[...])
        lfo_ref[0] = lf_new[:, :H]
        diag = (lax.broadcasted_iota(jnp.int32, (H, SMALL_W), 0)
                == lax.broadcasted_iota(jnp.int32, (H, SMALL_W), 1))
        lf_col = jnp.sum(jnp.where(diag, jnp.broadcast_to(lf_new, (H, SMALL_W)), 0.0), axis=1, keepdims=True)
        carry_ref[...] = jnp.broadcast_to(lf_col, (H, LANES))
        k_new = kn_ref[0].astype(BF16).astype(F32)
        s_new = jnp.sum(q_rows * k_new, axis=1, keepdims=True)
        m_ref[...] = jnp.broadcast_to(s_new, (H, LANES))
        l_ref[...] = jnp.ones_like(l_ref)
        acc_ref[...] = jnp.broadcast_to(vn_ref[0].astype(BF16).astype(F32), (H, ATT_WIDTH))

    kb = ck_ref[0].astype(BF16)
    vb = cv_ref[0].astype(BF16)
    lf_t = clf_ref[0]
    page = lf_t.shape[1]
    later = (lax.broadcasted_iota(jnp.int32, (page, page), 0)
             > lax.broadcasted_iota(jnp.int32, (page, page), 1)).astype(F32)
    suffix = jnp.dot(lf_t, later, precision=HIGHEST, preferred_element_type=F32)
    s = lax.dot_general(q_blk, kb, (((1,), (1,)), ((), ())), preferred_element_type=F32)
    s = s + suffix + carry_ref[...]
    m_prev = m_ref[...]
    m_new = jnp.maximum(m_prev, jnp.max(s, axis=1, keepdims=True))
    alpha = jnp.exp(m_prev - m_new)
    p = jnp.exp(s - m_new)
    l_ref[...] = alpha * l_ref[...] + jnp.sum(p, axis=1, keepdims=True)
    acc_ref[...] = (pltpu.repeat(alpha, ATT_WIDTH // LANES, axis=1) * acc_ref[...]
                    + jnp.dot(p.astype(BF16), vb, preferred_element_type=F32))
    m_ref[...] = m_new
    carry_ref[...] = carry_ref[...] + jnp.sum(lf_t, axis=1, keepdims=True)

    @pl.when(pg == npg - 1)
    def _():
        out = acc_ref[...] / pltpu.repeat(l_ref[...], ATT_WIDTH // LANES, axis=1)
        o_ref[0] = jnp.sum(jnp.where(own, out, 0.0), axis=0, keepdims=True).astype(o_ref.dtype)


def _decode_attention(page_table, qb, k_new, v_new, small, b_f, cache_k, cache_v, cache_lf_t):
    Bd = qb.shape[0]
    n_pages = page_table.shape[1]
    page = cache_k.shape[1]
    pt_flat = page_table.reshape(-1)
    bf2 = jnp.pad(b_f.reshape(1, ATT_HEADS), ((0, 0), (0, SMALL_W - ATT_HEADS)))

    def page_idx(b, pg, pt):
        return pt[b * n_pages + (n_pages - 1 - pg)]

    row = lambda w: pl.BlockSpec((1, 1, w), lambda b, pg, pt: (b, 0, 0))
    grid_spec = pltpu.PrefetchScalarGridSpec(
        num_scalar_prefetch=1,
        grid=(Bd, n_pages),
        in_specs=[row(ATT_WIDTH), row(ATT_WIDTH), row(ATT_WIDTH), row(SMALL_W),
                  pl.BlockSpec((1, SMALL_W), lambda b, pg, pt: (0, 0)),
                  pl.BlockSpec((1, page, ATT_WIDTH), lambda b, pg, pt: (page_idx(b, pg, pt), 0, 0)),
                  pl.BlockSpec((1, page, ATT_WIDTH), lambda b, pg, pt: (page_idx(b, pg, pt), 0, 0)),
                  pl.BlockSpec((1, ATT_HEADS, page), lambda b, pg, pt: (page_idx(b, pg, pt), 0, 0))],
        out_specs=[row(ATT_WIDTH), row(ATT_HEADS)],
        scratch_shapes=[pltpu.VMEM((ATT_HEADS, LANES), F32), pltpu.VMEM((ATT_HEADS, LANES), F32),
                        pltpu.VMEM((ATT_HEADS, ATT_WIDTH), F32), pltpu.VMEM((ATT_HEADS, LANES), F32)],
    )
    return pl.pallas_call(
        _decode_attn_kernel,
        grid_spec=grid_spec,
        out_shape=[jax.ShapeDtypeStruct((Bd, 1, ATT_WIDTH), BF16),
                   jax.ShapeDtypeStruct((Bd, 1, ATT_HEADS), F32)],
        compiler_params=_cparams("arbitrary", "arbitrary"),
        name="fox_decode",
    )(pt_flat, qb, k_new, v_new, small, bf2, cache_k, cache_v, cache_lf_t)


def _ssm_step_kernel(xbc_ref, sc_ref, sm_ref, z_ref, h0_ref, cw_ref, cb_ref, dtb_ref, alog_ref, e_ref,
                     dx_ref, nw_ref, y_ref, st_ref):
    H = SSM_HEADS
    rows = [xbc_ref[0]] + [sc_ref[0, CONV_WIDTH - 1 - j:CONV_WIDTH - j, :] for j in range(1, CONV_WIDTH)]
    xc = _conv_silu_rows(rows, cw_ref, cb_ref)
    xs = xc[:, :SSM_WIDTH]
    e = e_ref[...]
    dt = _softplus(sm_ref[0] + dtb_ref[...])
    da = jnp.exp(dt * (-jnp.exp(alog_ref[...])))
    both = _expand_heads(jnp.concatenate([jnp.broadcast_to(dt, (8, SMALL_W)),
                                          jnp.broadcast_to(da, (8, SMALL_W))], axis=0), e)
    dt_x = both[0:1, :]
    da_x = both[8:9, :]
    xdt = xs * dt_x

    sub = lax.broadcasted_iota(jnp.int32, (H, SSM_WIDTH), 0)
    own = sub == lax.broadcasted_iota(jnp.int32, (H, SSM_WIDTH), 1) // SSM_HEAD_DIM

    def masked_parts(v):
        m = jnp.where(own, jnp.broadcast_to(v, (H, SSM_WIDTH)), 0.0)
        hi = m.astype(BF16)
        return hi, (m - hi.astype(F32)).astype(BF16)

    da_hi, da_lo = masked_parts(da_x)
    x_hi, x_lo = masked_parts(xdt)
    lhs = jnp.concatenate([da_hi, da_lo, x_hi, x_lo], axis=0)
    hpg = H // SSM_GROUPS
    grp = lax.broadcasted_iota(jnp.int32, (H, D_STATE), 0) // hpg
    b_rows = jnp.zeros((H, D_STATE), F32)
    c_rows = jnp.zeros((H, D_STATE), F32)
    for g in range(SSM_GROUPS):
        bg = xc[:, SSM_WIDTH + g * D_STATE:SSM_WIDTH + (g + 1) * D_STATE]
        cg = xc[:, SSM_WIDTH + (SSM_GROUPS + g) * D_STATE:SSM_WIDTH + (SSM_GROUPS + g + 1) * D_STATE]
        b_rows = jnp.where(grp == g, jnp.broadcast_to(bg, (H, D_STATE)), b_rows)
        c_rows = jnp.where(grp == g, jnp.broadcast_to(cg, (H, D_STATE)), c_rows)
    ones = jnp.ones((2 * H, D_STATE), BF16)
    zeros = jnp.zeros((2 * H, D_STATE), BF16)
    b_bf = b_rows.astype(BF16)
    rhs = jnp.concatenate([jnp.concatenate([ones, zeros], axis=1),
                           jnp.concatenate([zeros, jnp.concatenate([b_bf, b_bf], axis=0)], axis=1)], axis=0)
    mix = lax.dot_general(lhs, rhs, (((0,), (0,)), ((), ())), preferred_element_type=F32)
    h0 = h0_ref[0].reshape(SSM_WIDTH, D_STATE)
    h_new = mix[:, :D_STATE] * h0 + mix[:, D_STATE:]
    st_ref[0] = h_new.reshape(H, SSM_HEAD_DIM, D_STATE)
    y_t = lax.dot_general(c_rows.astype(BF16), h_new.astype(BF16), (((1,), (1,)), ((), ())),
                          preferred_element_type=F32)
    y = jnp.sum(jnp.where(own, y_t, 0.0), axis=0, keepdims=True) + dx_ref[...] * xs
    gated = y * _silu(z_ref[0].astype(F32))
    y_ref[0] = _rmsnorm_rows(gated, nw_ref[...]).astype(y_ref.dtype)


def _ssm_step(xbc, state_conv, small, z, state_ssm, conv_w, conv_b, dtb, alog, dx, e, ssm_norm_w):
    Bd = xbc.shape[0]
    full = lambda a: pl.BlockSpec(a.shape, lambda b: (0,) * a.ndim)
    row = lambda w: pl.BlockSpec((1, 1, w), lambda b: (b, 0, 0))
    st_spec = pl.BlockSpec((1, SSM_HEADS, SSM_HEAD_DIM, D_STATE), lambda b: (b, 0, 0, 0))
    cb = conv_b.reshape(1, CONV_DIM)
    nw = ssm_norm_w.reshape(1, SSM_WIDTH)
    return pl.pallas_call(
        _ssm_step_kernel,
        grid=(Bd,),
        in_specs=[row(CONV_DIM), pl.BlockSpec((1, CONV_WIDTH - 1, CONV_DIM), lambda b: (b, 0, 0)),
                  row(SMALL_W), row(SSM_WIDTH), st_spec,
                  full(conv_w), full(cb), full(dtb), full(alog), full(e), full(dx), full(nw)],
        out_specs=[row(SSM_WIDTH), st_spec],
        out_shape=[jax.ShapeDtypeStruct((Bd, 1, SSM_WIDTH), BF16),
                   jax.ShapeDtypeStruct(state_ssm.shape, F32)],
        compiler_params=_cparams("arbitrary"),
        name="ssm_step",
    )(xbc, state_conv, small, z, state_ssm, conv_w, cb, dtb, alog, e, dx, nw)


def _outproj_kernel(att_ref, ssm_ref, x_ref, g1_ref, sh2_ref, sc2_ref, n2_ref, wa_ref, wsm_ref, rw_ref, rb_ref,
                    x1_ref, h2_ref, ti_ref, tg_ref):
    y = (jnp.dot(att_ref[0], wa_ref[...], preferred_element_type=F32)
         + jnp.dot(ssm_ref[0], wsm_ref[...], preferred_element_type=F32))
    x1 = x_ref[0] + g1_ref[0] * y
    x1_ref[0] = x1
    h2 = _rmsnorm_rows(x1, n2_ref[...]) * (1.0 + sc2_ref[0]) + sh2_ref[0]
    h2_ref[0] = h2.astype(BF16)
    logits = jnp.dot(h2, rw_ref[...], precision=HIGHEST, preferred_element_type=F32) + rb_ref[...]
    tm = logits.shape[0]
    lane = lax.broadcasted_iota(jnp.int32, (tm, LANES), 1).astype(F32)
    cur = logits
    idx_tile = jnp.zeros((tm, LANES), F32)
    val_tile = jnp.full((tm, LANES), NEG_BIG, F32)
    for k in range(TOP_K):
        m = jnp.max(cur, axis=1, keepdims=True)
        idx = jnp.min(jnp.where(cur == m, lane, float(LANES)), axis=1, keepdims=True)
        idx_tile = jnp.where(lane == float(k), idx, idx_tile)
        val_tile = jnp.where(lane == float(k), m, val_tile)
        cur = jnp.where(lane == idx, 2.0 * NEG_BIG, cur)
    top = jnp.max(val_tile, axis=1, keepdims=True)
    ex = jnp.exp(val_tile - top)
    tg_ref[0] = ex / jnp.sum(ex, axis=1, keepdims=True)
    ti_ref[0] = idx_tile.astype(jnp.int32)


def _out_proj(att, ssm, x, g1, sh2, sc2, norm2_w, wa, wsm, rw, rb, tm):
    B, L, _ = x.shape
    per_row = g1.shape[1] != 1
    mod_spec = (pl.BlockSpec((1, tm, D_MODEL), lambda b, i: (b, i, 0)) if per_row
                else pl.BlockSpec((1, 1, D_MODEL), lambda b, i: (b, 0, 0)))
    full = lambda a: pl.BlockSpec(a.shape, lambda b, i: (0,) * a.ndim)
    row = lambda w: pl.BlockSpec((1, tm, w), lambda b, i: (b, i, 0))
    n2 = norm2_w.reshape(1, D_MODEL)
    return pl.pallas_call(
        _outproj_kernel,
        grid=(B, L // tm),
        in_specs=[row(ATT_WIDTH), row(SSM_WIDTH), row(D_MODEL), mod_spec, mod_spec, mod_spec,
                  full(n2), full(wa), full(wsm), full(rw), full(rb)],
        out_specs=[row(D_MODEL), row(D_MODEL), row(LANES), row(LANES)],
        out_shape=[jax.ShapeDtypeStruct((B, L, D_MODEL), F32), jax.ShapeDtypeStruct((B, L, D_MODEL), BF16),
                   jax.ShapeDtypeStruct((B, L, LANES), jnp.int32), jax.ShapeDtypeStruct((B, L, LANES), F32)],
        compiler_params=_cparams("arbitrary", "arbitrary"),
        name="out_proj_route",
    )(att, ssm, x, g1, sh2, sc2, n2, wa, wsm, rw, rb)


MOE_ROWS = 512


def _moe_kernel(be_ref, first_ref, valid_ref, x_ref, gate_ref, wgu_ref, bgu_ref, wd_ref, bd_ref,
                o_ref, wgu_s, wd_s):
    i = pl.program_id(0)

    @pl.when(first_ref[i] == 1)
    def _():
        wgu_s[...] = wgu_ref[0].astype(BF16)
        wd_s[...] = wd_ref[0].astype(BF16)

    @pl.when(valid_ref[i] == 1)
    def _():
        gu = jnp.dot(x_ref[...], wgu_s[...], preferred_element_type=F32) + bgu_ref[0]
        g = jnp.minimum(gu[:, :D_FF], SWIGLU_LIMIT)
        u = jnp.clip(gu[:, D_FF:], -SWIGLU_LIMIT, SWIGLU_LIMIT)
        act = (u + 1.0) * (g * jax.nn.sigmoid(SWIGLU_ALPHA * g))
        out = jnp.dot(act.astype(BF16), wd_s[...], preferred_element_type=F32) + bd_ref[0]
        o_ref[...] = out * gate_ref[...]

    @pl.when(valid_ref[i] == 0)
    def _():
        o_ref[...] = jnp.zeros_like(o_ref)


def _moe_blocks(block_expert, block_first, block_valid, x_sorted, gate_sorted, w_gate_up, b_gate_up, w_down, b_down):
    n_rows = x_sorted.shape[0]
    tb = MOE_ROWS
    grid_spec = pltpu.PrefetchScalarGridSpec(
        num_scalar_prefetch=3,
        grid=(n_rows // tb,),
        in_specs=[pl.BlockSpec((tb, D_MODEL), lambda i, be, bf, bv: (i, 0)),
                  pl.BlockSpec((tb, 1), lambda i, be, bf, bv: (i, 0)),
                  pl.BlockSpec((1, D_MODEL, 2 * D_FF), lambda i, be, bf, bv: (be[i], 0, 0)),
                  pl.BlockSpec((1, 1, 2 * D_FF), lambda i, be, bf, bv: (be[i], 0, 0)),
                  pl.BlockSpec((1, D_FF, D_MODEL), lambda i, be, bf, bv: (be[i], 0, 0)),
                  pl.BlockSpec((1, 1, D_MODEL), lambda i, be, bf, bv: (be[i], 0, 0))],
        out_specs=pl.BlockSpec((tb, D_MODEL), lambda i, be, bf, bv: (i, 0)),
        scratch_shapes=[pltpu.VMEM((D_MODEL, 2 * D_FF), BF16), pltpu.VMEM((D_FF, D_MODEL), BF16)],
    )
    return pl.pallas_call(
        _moe_kernel,
        grid_spec=grid_spec,
        out_shape=jax.ShapeDtypeStruct((n_rows, D_MODEL), F32),
        compiler_params=_cparams("arbitrary"),
        name="moe_experts",
    )(block_expert, block_first, block_valid, x_sorted, gate_sorted, w_gate_up,
      b_gate_up.reshape(N_EXPERTS, 1, 2 * D_FF), w_down, b_down.reshape(N_EXPERTS, 1, D_MODEL))


def _moe_dispatch(top_idx, gates, n_tokens):
    tb = MOE_ROWS
    n_assign = n_tokens * TOP_K
    flat_e = top_idx.reshape(-1)
    order = jnp.argsort(flat_e)
    sorted_e = flat_e[order]
    counts = jnp.bincount(flat_e, length=N_EXPERTS)
    padded = (counts + tb - 1) // tb * tb
    pad_end = jnp.cumsum(padded)
    pad_start = pad_end - padded
    start = jnp.cumsum(counts) - counts
    dest = pad_start[sorted_e] + jnp.arange(n_assign) - start[sorted_e]
    n_blocks = (n_assign + N_EXPERTS * (tb - 1)) // tb
    slot_token = jnp.full((n_blocks * tb,), n_tokens, jnp.int32).at[dest].set((order // TOP_K).astype(jnp.int32))
    slot_gate = jnp.zeros((n_blocks * tb,), F32).at[dest].set(gates.reshape(-1)[order])
    block_lo = jnp.arange(n_blocks) * tb
    block_expert = jnp.minimum(jnp.searchsorted(pad_end, block_lo, side='right'), N_EXPERTS - 1).astype(jnp.int32)
    block_valid = (block_lo < pad_end[-1]).astype(jnp.int32)
    prev = jnp.concatenate([jnp.full((1,), -1, jnp.int32), block_expert[:-1]])
    block_first = ((block_expert != prev) & (block_valid == 1)).astype(jnp.int32)
    assign_slot = jnp.zeros((n_assign,), jnp.int32).at[order].set(dest.astype(jnp.int32))
    return slot_token, slot_gate, block_expert, block_first, block_valid, assign_slot.reshape(n_tokens, TOP_K)


def _final_kernel(x1_ref, moe_ref, g2_ref, nw_ref, o_ref):
    x2 = x1_ref[0] + g2_ref[0] * moe_ref[0]
    o_ref[0] = _rmsnorm_rows(x2, nw_ref[...])


def _final(x1, moe, g2, final_norm_w, tm):
    B, L, _ = x1.shape
    per_row = g2.shape[1] != 1
    mod_spec = (pl.BlockSpec((1, tm, D_MODEL), lambda b, i: (b, i, 0)) if per_row
                else pl.BlockSpec((1, 1, D_MODEL), lambda b, i: (b, 0, 0)))
    row = pl.BlockSpec((1, tm, D_MODEL), lambda b, i: (b, i, 0))
    nw = final_norm_w.reshape(1, D_MODEL)
    return pl.pallas_call(
        _final_kernel,
        grid=(B, L // tm),
        in_specs=[row, row, mod_spec, pl.BlockSpec((1, D_MODEL), lambda b, i: (0, 0))],
        out_specs=row,
        out_shape=jax.ShapeDtypeStruct((B, L, D_MODEL), F32),
        compiler_params=_cparams("arbitrary", "arbitrary"),
        name="final_norm",
    )(x1, moe, g2, nw)


def kernel(x_prompt, x_sample, c_prompt, c_sample, cache_k, cache_v, cache_lf, state_conv, state_ssm, page_table,
           ada_w, ada_b, norm1_w, w_in, b_f, conv_w, conv_b, dt_bias, A_log, D_skip, ssm_norm_w, w_out,
           norm2_w, router_w, router_b, w_gate_up, b_gate_up, w_down, b_down, final_norm_w):
    assert ada_w.shape[0] == 1, "single-layer trunk"
    B, L, D = x_prompt.shape
    Bd = x_sample.shape[0]
    assert x_sample.shape[1] == 1 and L % SSD_CHUNK == 0

    n_c = B + Bd
    rows = -(-n_c // 8) * 8
    c_all = jnp.concatenate([c_prompt, c_sample, jnp.zeros((rows - n_c, D), F32)], axis=0)
    mod = _modulation(c_all, ada_w[0], ada_b[0])
    mod_p = [m.reshape(B, 1, D) for m in jnp.split(mod[:B], 6, axis=-1)]
    mod_s = [m.reshape(1, Bd, D) for m in jnp.split(mod[B:n_c], 6, axis=-1)]

    w = w_in[0]
    o_f = 3 * ATT_WIDTH
    o_z = o_f + ATT_HEADS
    o_x = o_z + SSM_WIDTH
    o_dt = o_x + CONV_DIM
    w_small = jnp.concatenate([w[:, o_f:o_z], w[:, o_dt:o_dt + SSM_HEADS],
                               jnp.zeros((D, SMALL_W - ATT_HEADS - SSM_HEADS), F32)], axis=1)
    wts = (w[:, :ATT_WIDTH].astype(BF16), w[:, ATT_WIDTH:2 * ATT_WIDTH].astype(BF16),
           w[:, 2 * ATT_WIDTH:o_f].astype(BF16), w[:, o_z:o_x].astype(BF16), w[:, o_x:o_dt].astype(BF16), w_small)
    wa = w_out[0][:ATT_WIDTH].astype(BF16)
    wsm = w_out[0][ATT_WIDTH:].astype(BF16)
    rw = jnp.pad(router_w[0], ((0, 0), (0, LANES - N_EXPERTS)))
    rb = jnp.pad(router_b[0].reshape(1, N_EXPERTS), ((0, 0), (0, LANES - N_EXPERTS)), constant_values=NEG_BIG)
    dtb, alog, dx = _ssm_params(dt_bias[0], A_log[0], D_skip[0])
    e = jnp.asarray(_head_expander(), BF16)

    tm_p = min(256, L)
    qb, kb, vb, k_p, v_p, z_p, xbc_p, small_p = _in_proj(x_prompt, mod_p[0], mod_p[1], norm1_w[0], wts, tm_p)
    lf_p, qp, kp = _attn_prep(qb, kb, small_p, b_f[0], min(256, L))
    att_p = _flash_attention(qp, kp, vb, min(512, L), min(512, L))
    ssm_p, st_p = _ssd_prompt(xbc_p, small_p, z_p, conv_w[0], conv_b[0], dtb, alog, dx, e, ssm_norm_w[0])
    x1_p, h2_p, ti_p, tg_p = _out_proj(att_p, ssm_p, x_prompt, mod_p[2], mod_p[3], mod_p[4], norm2_w[0],
                                       wa, wsm, rw, rb, min(512, L))

    xs_rows = x_sample.reshape(1, Bd, D)
    qb_s, _, _, k_s, v_s, z_s, xbc_s, small_s = _in_proj(xs_rows, mod_s[0], mod_s[1], norm1_w[0], wts, Bd)
    n_pool, page = cache_k.shape[1], cache_k.shape[2]
    per_row = lambda a: a.reshape(Bd, 1, a.shape[-1])
    att_s, lf_s = _decode_attention(
        page_table, per_row(qb_s), per_row(k_s), per_row(v_s), per_row(small_s), b_f[0],
        cache_k[0].reshape(n_pool, page, ATT_WIDTH), cache_v[0].reshape(n_pool, page, ATT_WIDTH),
        cache_lf[0].transpose(0, 2, 1))
    ssm_s, st_s = _ssm_step(per_row(xbc_s), state_conv[0], per_row(small_s), per_row(z_s), state_ssm[0],
                            conv_w[0], conv_b[0], dtb, alog, dx, e, ssm_norm_w[0])
    x1_s, h2_s, ti_s, tg_s = _out_proj(att_s.reshape(1, Bd, ATT_WIDTH), ssm_s.reshape(1, Bd, SSM_WIDTH), xs_rows,
                                       mod_s[2], mod_s[3], mod_s[4], norm2_w[0], wa, wsm, rw, rb, Bd)

    n_tok = B * L + Bd
    h2_all = jnp.concatenate([h2_p.reshape(B * L, D), h2_s.reshape(Bd, D)], axis=0)
    ti_all = jnp.concatenate([ti_p.reshape(B * L, LANES), ti_s.reshape(Bd, LANES)], axis=0)[:, :TOP_K]
    tg_all = jnp.concatenate([tg_p.reshape(B * L, LANES), tg_s.reshape(Bd, LANES)], axis=0)[:, :TOP_K]
    slot_token, slot_gate, b_exp, b_first, b_valid, assign_slot = _moe_dispatch(ti_all, tg_all, n_tok)
    h2_pad = jnp.concatenate([h2_all, jnp.zeros((1, D), BF16)], axis=0)
    x_sorted = h2_pad[slot_token]
    y_sorted = _moe_blocks(b_exp, b_first, b_valid, x_sorted, slot_gate.reshape(-1, 1),
                           w_gate_up[0], b_gate_up[0], w_down[0], b_down[0])
    moe = jnp.sum(y_sorted[assign_slot], axis=1)

    y_prompt = _final(x1_p, moe[:B * L].reshape(B, L, D), mod_p[5], final_norm_w, min(512, L))
    y_sample = _final(x1_s, moe[B * L:].reshape(1, Bd, D), mod_s[5], final_norm_w, Bd).reshape(Bd, 1, D)

    conv_s = jnp.concatenate([state_conv[0][:, 1:], xbc_s.reshape(Bd, 1, CONV_DIM)], axis=1)
    ssm_state_p = st_p.reshape(B, D_STATE, SSM_HEADS, SSM_HEAD_DIM).transpose(0, 2, 3, 1)
    return (y_prompt, y_sample,
            k_p.reshape(1, B, L, ATT_HEADS, HEAD_DIM), v_p.reshape(1, B, L, ATT_HEADS, HEAD_DIM),
            lf_p.reshape(1, B, L, ATT_HEADS), xbc_p[:, L - (CONV_WIDTH - 1):].reshape(1, B, CONV_WIDTH - 1, CONV_DIM),
            ssm_state_p.reshape(1, B, SSM_HEADS, SSM_HEAD_DIM, D_STATE),
            k_s.reshape(1, Bd, 1, ATT_HEADS, HEAD_DIM), v_s.reshape(1, Bd, 1, ATT_HEADS, HEAD_DIM),
            lf_s.reshape(1, Bd, 1, ATT_HEADS), conv_s.reshape(1, Bd, CONV_WIDTH - 1, CONV_DIM),
            st_s.reshape(1, Bd, SSM_HEADS, SSM_HEAD_DIM, D_STATE))
```

```python
import functools
import math

import numpy as np
import jax
import jax.numpy as jnp
from jax import lax
from jax.experimental import pallas as pl
from jax.experimental.pallas import tpu as pltpu

F32 = jnp.float32
BF16 = jnp.bfloat16
I32 = jnp.int32
HIGHEST = lax.Precision.HIGHEST

D_MODEL = 1024
ATT_HEADS = 16
HEAD_DIM = 64
ATT_WIDTH = ATT_HEADS * HEAD_DIM
SSM_HEADS = 16
SSM_HEAD_DIM = 64
SSM_WIDTH = SSM_HEADS * SSM_HEAD_DIM
SSM_GROUPS = 2
D_STATE = 128
CONV_WIDTH = 4
CONV_DIM = SSM_WIDTH + 2 * SSM_GROUPS * D_STATE
SSD_CHUNK = 128
N_EXPERTS = 32
TOP_K = 4
D_FF = D_MODEL
SWIGLU_LIMIT = 7.0
SWIGLU_ALPHA = 1.702
NORM_EPS = 1e-5

LANES = 128
SUBLANES = 8
SMALL_W = LANES
DT_COL = ATT_HEADS
NEG_BIG = -1e30
LOG2E = math.log2(math.e)
VMEM_LIMIT = 48 * 1024 * 1024
MOE_ROWS = 512
ROUTE_TOKENS = 256


def _cparams(*sem):
    return pltpu.CompilerParams(dimension_semantics=sem, vmem_limit_bytes=VMEM_LIMIT)


def _silu(x):
    return x * jax.nn.sigmoid(x)


def _softplus(x):
    return jnp.maximum(x, 0.0) + jnp.log(1.0 + jnp.exp(-jnp.abs(x)))


def _log_sigmoid(x):
    return -_softplus(-x)


def _rmsnorm_rows(x, w):
    var = jnp.mean(x * x, axis=-1, keepdims=True)
    return x * lax.rsqrt(var + NORM_EPS) * w


def _split3_bf16(x):
    hi = x.astype(BF16)
    r = x - hi.astype(F32)
    mid = r.astype(BF16)
    lo = (r - mid.astype(F32)).astype(BF16)
    return hi, mid, lo


def _mod_kernel(c_ref, w_ref, b_ref, o_ref):
    s = _silu(c_ref[...]).astype(BF16)
    o_ref[...] = jnp.dot(s, w_ref[...].astype(BF16), preferred_element_type=F32) + b_ref[...]


def _modulation(c_all, ada_w, ada_b):
    rows = c_all.shape[0]
    n_out = ada_w.shape[1]
    tn = D_MODEL
    return pl.pallas_call(
        _mod_kernel,
        grid=(n_out // tn,),
        in_specs=[pl.BlockSpec((rows, D_MODEL), lambda j: (0, 0)),
                  pl.BlockSpec((D_MODEL, tn), lambda j: (0, j)),
                  pl.BlockSpec((1, tn), lambda j: (0, j))],
        out_specs=pl.BlockSpec((rows, tn), lambda j: (0, j)),
        out_shape=jax.ShapeDtypeStruct((rows, n_out), F32),
        compiler_params=_cparams("arbitrary"),
        name="adaln_mod",
    )(c_all, ada_w, ada_b.reshape(1, n_out))


def _inproj_kernel(x_ref, sh_ref, sc_ref, nw_ref, wq_ref, wk_ref, wv_ref, wz_ref, wx_ref, ws_ref,
                   qb_ref, kb_ref, vb_ref, k_ref, v_ref, z_ref, xbc_ref, sm_ref):
    h = _rmsnorm_rows(x_ref[0], nw_ref[...]) * (1.0 + sc_ref[0]) + sh_ref[0]
    hb = h.astype(BF16)
    q = jnp.dot(hb, wq_ref[...], preferred_element_type=F32)
    qb_ref[0] = (q * (HEAD_DIM ** -0.5 * LOG2E)).astype(BF16)
    k = jnp.dot(hb, wk_ref[...], preferred_element_type=F32)
    k_ref[0] = k
    kb_ref[0] = k.astype(BF16)
    v = jnp.dot(hb, wv_ref[...], preferred_element_type=F32)
    v_ref[0] = v
    vb_ref[0] = v.astype(BF16)
    z_ref[0] = jnp.dot(hb, wz_ref[...], preferred_element_type=F32).astype(BF16)
    xbc_ref[0] = jnp.dot(hb, wx_ref[...], preferred_element_type=F32)
    sm_ref[0] = jnp.dot(h, ws_ref[...], precision=HIGHEST, preferred_element_type=F32)


def _in_proj(x, sh, sc, norm_w, wts, tm):
    B, L, _ = x.shape
    per_row = sh.shape[1] != 1
    mod_spec = (pl.BlockSpec((1, tm, D_MODEL), lambda b, i: (b, i, 0)) if per_row
                else pl.BlockSpec((1, 1, D_MODEL), lambda b, i: (b, 0, 0)))
    wq, wk, wv, wz, wx, ws = wts

    def wspec(w):
        return pl.BlockSpec(w.shape, lambda b, i: (0, 0))

    def ospec(width):
        return pl.BlockSpec((1, tm, width), lambda b, i: (b, i, 0))

    def oshape(width, dt):
        return jax.ShapeDtypeStruct((B, L, width), dt)

    return pl.pallas_call(
        _inproj_kernel,
        grid=(B, L // tm),
        in_specs=[pl.BlockSpec((1, tm, D_MODEL), lambda b, i: (b, i, 0)), mod_spec, mod_spec,
                  pl.BlockSpec((1, D_MODEL), lambda b, i: (0, 0)),
                  wspec(wq), wspec(wk), wspec(wv), wspec(wz), wspec(wx), wspec(ws)],
        out_specs=[ospec(ATT_WIDTH), ospec(ATT_WIDTH), ospec(ATT_WIDTH), ospec(ATT_WIDTH), ospec(ATT_WIDTH),
                   ospec(SSM_WIDTH), ospec(CONV_DIM), ospec(SMALL_W)],
        out_shape=[oshape(ATT_WIDTH, BF16), oshape(ATT_WIDTH, BF16), oshape(ATT_WIDTH, BF16),
                   oshape(ATT_WIDTH, F32), oshape(ATT_WIDTH, F32),
                   oshape(SSM_WIDTH, BF16), oshape(CONV_DIM, F32), oshape(SMALL_W, F32)],
        compiler_params=_cparams("arbitrary", "arbitrary"),
        name="in_proj",
    )(x, sh, sc, norm_w.reshape(1, D_MODEL), wq, wk, wv, wz, wx, ws)


def _free_half(h):
    return h * LANES + (HEAD_DIM if h % 2 == 0 else 0)


def _bias_selectors():
    wide = ATT_HEADS * LANES
    sq = np.zeros((3, SMALL_W, wide), np.float32)
    sk = np.zeros((3, SMALL_W, wide), np.float32)
    oq = np.zeros((1, wide), np.float32)
    ok = np.zeros((1, wide), np.float32)
    ov = np.zeros((1, wide), np.float32)
    for h in range(ATT_HEADS):
        base = _free_half(h)
        ov[0, base] = 1.0
        for part in range(3):
            sq[part, h, base + part] = 1.0
            ok[0, base + part] = 1.0
            sk[part, h, base + 3 + part] = -1.0
            oq[0, base + 3 + part] = 1.0
    return sq, sk, oq, ok, ov


def _prep_kernel(qb_ref, kb_ref, vb_ref, sm_ref, bf_ref, sq_ref, sk_ref, oq_ref, ok_ref, ov_ref,
                 lf_ref, qp_ref, kp_ref, vp_ref, carry_ref):
    i = pl.program_id(1)
    tl = qb_ref.shape[1]

    @pl.when(i == 0)
    def _():
        carry_ref[...] = jnp.zeros_like(carry_ref)

    lf = _log_sigmoid(sm_ref[0] + bf_ref[...])
    lf_ref[0] = lf[:, :ATT_HEADS]
    row = lax.broadcasted_iota(I32, (tl, tl), 0)
    col = lax.broadcasted_iota(I32, (tl, tl), 1)
    tri = (col <= row).astype(F32)
    fcum = jnp.dot(tri, lf, precision=HIGHEST, preferred_element_type=F32) + carry_ref[0:1, :]
    carry_ref[0:1, :] = fcum[tl - 1:tl, :]
    parts = _split3_bf16(fcum * LOG2E)
    augq = oq_ref[...]
    augk = ok_ref[...]
    for p in range(3):
        augq = augq + jnp.dot(parts[p], sq_ref[p], preferred_element_type=F32)
        augk = augk + jnp.dot(parts[p], sk_ref[p], preferred_element_type=F32)
    lane = lax.broadcasted_iota(I32, (tl, LANES), 1)
    low = lane < HEAD_DIM
    for pair in range(ATT_HEADS // 2):
        qpair = qb_ref[0, :, pair * LANES:(pair + 1) * LANES]
        kpair = kb_ref[0, :, pair * LANES:(pair + 1) * LANES]
        vpair = vb_ref[0, :, pair * LANES:(pair + 1) * LANES]
        for hh in range(2):
            h = 2 * pair + hh
            keep = low if hh == 0 else jnp.logical_not(low)
            sl = slice(h * LANES, (h + 1) * LANES)
            qp_ref[0, :, sl] = jnp.where(keep, qpair, augq[:, sl].astype(BF16))
            kp_ref[0, :, sl] = jnp.where(keep, kpair, augk[:, sl].astype(BF16))
            ones_lane = jnp.broadcast_to(ov_ref[:, sl], (tl, LANES)).astype(BF16)
            vp_ref[0, :, sl] = jnp.where(keep, vpair, ones_lane)


def _attn_prep(qb, kb, vb, small, b_f, tl):
    B, L, _ = qb.shape
    sq, sk, oq, ok, ov = _bias_selectors()
    wide = ATT_HEADS * LANES
    row_spec = lambda w: pl.BlockSpec((1, tl, w), lambda b, i: (b, i, 0))
    full = lambda a: pl.BlockSpec(a.shape, lambda b, i: (0,) * a.ndim)
    sq, sk = jnp.asarray(sq, BF16), jnp.asarray(sk, BF16)
    oq, ok, ov = jnp.asarray(oq), jnp.asarray(ok), jnp.asarray(ov)
    bf2 = jnp.pad(b_f.reshape(1, ATT_HEADS), ((0, 0), (0, SMALL_W - ATT_HEADS)))
    wide_shape = jax.ShapeDtypeStruct((B, L, wide), BF16)
    return pl.pallas_call(
        _prep_kernel,
        grid=(B, L // tl),
        in_specs=[row_spec(ATT_WIDTH), row_spec(ATT_WIDTH), row_spec(ATT_WIDTH), row_spec(SMALL_W), full(bf2),
                  full(sq), full(sk), full(oq), full(ok), full(ov)],
        out_specs=[row_spec(ATT_HEADS), row_spec(wide), row_spec(wide), row_spec(wide)],
        out_shape=[jax.ShapeDtypeStruct((B, L, ATT_HEADS), F32), wide_shape, wide_shape, wide_shape],
        scratch_shapes=[pltpu.VMEM((SUBLANES, LANES), F32)],
        compiler_params=_cparams("arbitrary", "arbitrary"),
        name="attn_prep",
    )(qb, kb, vb, small, bf2, sq, sk, oq, ok, ov)


def _flash_kernel(qi_ref, ki_ref, qp_ref, kp_ref, vp_ref, o_ref, m_ref, acc_ref, *, tq, tk):
    t = pl.program_id(2)
    qi = qi_ref[t]
    ki = ki_ref[t]
    last = ((qi + 1) * tq - 1) // tk

    @pl.when(ki == 0)
    def _():
        m_ref[...] = jnp.full_like(m_ref, NEG_BIG)
        acc_ref[...] = jnp.zeros_like(acc_ref)

    def step(masked):
        if masked:
            qpos = qi * tq + lax.broadcasted_iota(I32, (tq, tk), 0)
            kpos = ki * tk + lax.broadcasted_iota(I32, (tq, tk), 1)
            visible = kpos <= qpos
        for hh in range(2):
            q = qp_ref[0, :, hh * LANES:(hh + 1) * LANES]
            k = kp_ref[0, :, hh * LANES:(hh + 1) * LANES]
            s = lax.dot_general(q, k, (((1,), (1,)), ((), ())), preferred_element_type=F32)
            if masked:
                s = jnp.where(visible, s, NEG_BIG)
            m_prev = m_ref[hh]
            m_new = jnp.maximum(m_prev, jnp.max(s, axis=1, keepdims=True))
            p = jnp.exp2(s - jnp.concatenate([m_new] * (tk // LANES), axis=1))
            acc_ref[hh] = (jnp.exp2(m_prev - m_new) * acc_ref[hh]
                           + jnp.dot(p.astype(BF16), vp_ref[0, :, hh * LANES:(hh + 1) * LANES],
                                     preferred_element_type=F32))
            m_ref[hh] = m_new

    crosses = (ki + 1) * tk - 1 > qi * tq

    @pl.when(crosses)
    def _():
        step(True)

    @pl.when(jnp.logical_not(crosses))
    def _():
        step(False)

    @pl.when(ki == last)
    def _():
        lane = lax.broadcasted_iota(I32, (tq, LANES), 1)
        a0 = acc_ref[0]
        a1 = acc_ref[1]
        o0 = a0 / a0[:, HEAD_DIM:HEAD_DIM + 1]
        o1 = a1 / a1[:, 0:1]
        o_ref[0] = jnp.where(lane < HEAD_DIM, o0, o1).astype(o_ref.dtype)


def _flash_attention(qp, kp, vp, tq, tk):
    B, L, _ = qp.shape
    pairs = ATT_HEADS // 2
    qs, ks = [], []
    for qi in range(L // tq):
        for ki in range(((qi + 1) * tq - 1) // tk + 1):
            qs.append(qi)
            ks.append(ki)
    qi_tab = jnp.asarray(np.array(qs, np.int32))
    ki_tab = jnp.asarray(np.array(ks, np.int32))
    grid_spec = pltpu.PrefetchScalarGridSpec(
        num_scalar_prefetch=2,
        grid=(B, pairs, len(qs)),
        in_specs=[pl.BlockSpec((1, tq, 2 * LANES), lambda b, p, t, qt, kt: (b, qt[t], p)),
                  pl.BlockSpec((1, tk, 2 * LANES), lambda b, p, t, qt, kt: (b, kt[t], p)),
                  pl.BlockSpec((1, tk, 2 * LANES), lambda b, p, t, qt, kt: (b, kt[t], p))],
        out_specs=pl.BlockSpec((1, tq, LANES), lambda b, p, t, qt, kt: (b, qt[t], p)),
        scratch_shapes=[pltpu.VMEM((2, tq, LANES), F32), pltpu.VMEM((2, tq, LANES), F32)],
    )
    return pl.pallas_call(
        functools.partial(_flash_kernel, tq=tq, tk=tk),
        grid_spec=grid_spec,
        out_shape=jax.ShapeDtypeStruct((B, L, ATT_WIDTH), BF16),
        compiler_params=_cparams("arbitrary", "arbitrary", "arbitrary"),
        name="fox_flash",
    )(qi_tab, ki_tab, qp, kp, vp)


def _head_expander():
    e = np.zeros((SMALL_W, SSM_WIDTH), np.float32)
    for h in range(SSM_HEADS):
        e[DT_COL + h, h * SSM_HEAD_DIM:(h + 1) * SSM_HEAD_DIM] = 1.0
    return e


def _expand_heads(vals, e_bf16):
    hi = vals.astype(BF16)
    lo = (vals - hi.astype(F32)).astype(BF16)
    return (jnp.dot(hi, e_bf16, preferred_element_type=F32)
            + jnp.dot(lo, e_bf16, preferred_element_type=F32))


def _conv_silu_rows(rows, cw_ref, cb_ref):
    acc = cb_ref[...] + cw_ref[CONV_WIDTH - 1:CONV_WIDTH, :] * rows[0]
    for j in range(1, CONV_WIDTH):
        acc = acc + cw_ref[CONV_WIDTH - 1 - j:CONV_WIDTH - j, :] * rows[j]
    return _silu(acc)


def _ssd_kernel(xbc_ref, sm_ref, z_ref, cw_ref, cb_ref, dtb_ref, alog_ref, e_ref, dx_ref, nw_ref,
                y_ref, st_ref, buf_ref, ht_ref):
    c = pl.program_id(1)
    nc = pl.num_programs(1)
    Q = SSD_CHUNK
    halo = SUBLANES

    @pl.when(c == 0)
    def _():
        buf_ref[0:halo, :] = jnp.zeros((halo, CONV_DIM), F32)
        ht_ref[...] = jnp.zeros_like(ht_ref)

    @pl.when(c > 0)
    def _():
        buf_ref[0:halo, :] = buf_ref[Q:Q + halo, :]

    buf_ref[halo:halo + Q, :] = xbc_ref[0]
    xc = _conv_silu_rows([buf_ref[halo - j:halo - j + Q, :] for j in range(CONV_WIDTH)], cw_ref, cb_ref)
    xs = xc[:, :SSM_WIDTH]
    e = e_ref[...]

    dt = _softplus(sm_ref[0] + dtb_ref[...])
    a = dt * (-jnp.exp(alog_ref[...]))
    row = lax.broadcasted_iota(I32, (Q, Q), 0)
    col = lax.broadcasted_iota(I32, (Q, Q), 1)
    causal = col <= row
    acum = jnp.dot(causal.astype(F32), a, precision=HIGHEST, preferred_element_type=F32)
    acum_t = acum.T
    dt_x = _expand_heads(dt, e)
    acum_x = _expand_heads(acum, e)
    last_x = acum_x[Q - 1:Q, :]
    xdt = xs * dt_x
    xdt_b = xdt.astype(BF16)
    x_end = (xdt * jnp.exp(last_x - acum_x)).astype(BF16)
    grow = jnp.exp(acum_x)
    cdecay = jnp.exp(last_x)

    lane = lax.broadcasted_iota(I32, (Q, LANES), 1)
    low = lane < SSM_HEAD_DIM
    hpg = SSM_HEADS // SSM_GROUPS
    gw = hpg * SSM_HEAD_DIM
    y_parts = []
    for g in range(SSM_GROUPS):
        bg = xc[:, SSM_WIDTH + g * D_STATE:SSM_WIDTH + (g + 1) * D_STATE].astype(BF16)
        cg = xc[:, SSM_WIDTH + (SSM_GROUPS + g) * D_STATE:SSM_WIDTH + (SSM_GROUPS + g + 1) * D_STATE].astype(BF16)
        scores = lax.dot_general(cg, bg, (((1,), (1,)), ((), ())), preferred_element_type=F32)
        gs = slice(g * gw, (g + 1) * gw)
        h_prev = ht_ref[:, gs]
        y_off = jnp.dot(cg, h_prev.astype(BF16), preferred_element_type=F32) * grow[:, gs]
        ht_ref[:, gs] = h_prev * cdecay[:, gs] + lax.dot_general(
            bg, x_end[:, gs], (((0,), (0,)), ((), ())), preferred_element_type=F32)
        for pr in range(hpg // 2):
            pair_lo = g * gw + pr * LANES
            xpair = xdt_b[:, pair_lo:pair_lo + LANES]
            halves = []
            for hh in range(2):
                h = g * hpg + 2 * pr + hh
                decay = jnp.where(causal, jnp.exp(acum[:, DT_COL + h:DT_COL + h + 1]
                                                  - acum_t[DT_COL + h:DT_COL + h + 1, :]), 0.0)
                halves.append(jnp.dot((scores * decay).astype(BF16), xpair, preferred_element_type=F32))
            y_parts.append(jnp.where(low, halves[0], halves[1]) + y_off[:, pr * LANES:(pr + 1) * LANES])
    y = jnp.concatenate(y_parts, axis=1) + dx_ref[...] * xs
    gated = y * _silu(z_ref[0].astype(F32))
    y_ref[0] = _rmsnorm_rows(gated, nw_ref[...]).astype(y_ref.dtype)

    @pl.when(c == nc - 1)
    def _():
        st_ref[0] = ht_ref[...]


def _ssm_params(dt_bias, A_log, D_skip):
    pad = (DT_COL, SMALL_W - DT_COL - SSM_HEADS)
    dtb = jnp.pad(dt_bias.astype(F32), pad).reshape(1, SMALL_W)
    alog = jnp.pad(A_log.astype(F32), pad).reshape(1, SMALL_W)
    dx = jnp.repeat(D_skip.astype(F32), SSM_HEAD_DIM).reshape(1, SSM_WIDTH)
    return dtb, alog, dx


def _ssd_prompt(xbc, small, z, conv_w, conv_b, dtb, alog, dx, e, ssm_norm_w):
    B, L, _ = xbc.shape
    Q = SSD_CHUNK
    full = lambda a: pl.BlockSpec(a.shape, lambda b, c: (0,) * a.ndim)
    row_spec = lambda w: pl.BlockSpec((1, Q, w), lambda b, c: (b, c, 0))
    cb = conv_b.reshape(1, CONV_DIM)
    nw = ssm_norm_w.reshape(1, SSM_WIDTH)
    return pl.pallas_call(
        _ssd_kernel,
        grid=(B, L // Q),
        in_specs=[row_spec(CONV_DIM), row_spec(SMALL_W), row_spec(SSM_WIDTH),
                  full(conv_w), full(cb), full(dtb), full(alog), full(e), full(dx), full(nw)],
        out_specs=[row_spec(SSM_WIDTH), pl.BlockSpec((1, D_STATE, SSM_WIDTH), lambda b, c: (b, 0, 0))],
        out_shape=[jax.ShapeDtypeStruct((B, L, SSM_WIDTH), BF16),
                   jax.ShapeDtypeStruct((B, D_STATE, SSM_WIDTH), F32)],
        scratch_shapes=[pltpu.VMEM((Q + SUBLANES, CONV_DIM), F32), pltpu.VMEM((D_STATE, SSM_WIDTH), F32)],
        compiler_params=_cparams("arbitrary", "arbitrary"),
        name="ssd_prompt",
    )(xbc, small, z, conv_w, cb, dtb, alog, e, dx, nw)


DECODE_PAGES = 4


def _decode_attn_kernel(pt_ref, qt_ref, knt_ref, vnt_ref, sm_ref, bf_ref, *refs, pps):
    k_refs = refs[0:pps]
    v_refs = refs[pps:2 * pps]
    lf_refs = refs[2 * pps:3 * pps]
    o_ref, lfo_ref, qrep_ref, m_ref, l_ref, acc_ref, carry_ref, bias_ref = refs[3 * pps:]
    blk = pl.program_id(1)
    nblk = pl.num_programs(1)
    H = ATT_HEADS
    page = k_refs[0].shape[3]
    lane_row = lax.broadcasted_iota(I32, (1, page), 1)

    @pl.when(blk == 0)
    def _():
        lf_new = _log_sigmoid(sm_ref[0] + bf_ref[...])
        lfo_ref[0] = lf_new[:, :H]
        diag = (lax.broadcasted_iota(I32, (H, SMALL_W), 0) == lax.broadcasted_iota(I32, (H, SMALL_W), 1))
        lf_col = jnp.sum(jnp.where(diag, jnp.broadcast_to(lf_new, (H, SMALL_W)), 0.0), axis=1, keepdims=True)
        carry_ref[...] = jnp.broadcast_to(lf_col, (H, page))
        qt = qt_ref[0]
        knt = knt_ref[0].astype(BF16).astype(F32)
        vnt = vnt_ref[0].astype(BF16).astype(F32)
        s_row = jnp.sum(qt * knt, axis=0, keepdims=True)
        lane = lax.broadcasted_iota(I32, (HEAD_DIM, page), 1)
        for h in range(H):
            qrep_ref[h] = jnp.broadcast_to(qt[:, h:h + 1], (HEAD_DIM, page))
            m_ref[h:h + 1, :] = jnp.where(lane_row == 0, jnp.broadcast_to(s_row[:, h:h + 1], (1, page)), NEG_BIG)
            acc_ref[h] = jnp.where(lane == 0, jnp.broadcast_to(vnt[:, h:h + 1], (HEAD_DIM, page)), 0.0)
        l_ref[...] = jnp.broadcast_to(jnp.where(lane_row == 0, 1.0, 0.0), (H, page))

    later = (lax.broadcasted_iota(I32, (page, page), 0) > lax.broadcasted_iota(I32, (page, page), 1)).astype(F32)
    carry = carry_ref[...]
    lf_all = jnp.concatenate([lf_refs[j][0] for j in range(pps)], axis=0)
    suffix = jnp.dot(lf_all, later, precision=HIGHEST, preferred_element_type=F32)
    for j in range(pps):
        bias_ref[j] = (suffix[j * H:(j + 1) * H, :] + carry) * LOG2E
        carry = carry + jnp.sum(lf_refs[j][0], axis=1, keepdims=True)
    carry_ref[...] = carry

    def head_body(h, _):
        q3 = qrep_ref[h]
        row = pl.ds(h, 1)
        m = m_ref[row, :]
        l = l_ref[row, :]
        acc = acc_ref[h]
        for j in range(pps):
            s = jnp.sum(q3 * k_refs[j][0, h], axis=0, keepdims=True) + bias_ref[j, row, :]
            m_new = jnp.maximum(m, s)
            alpha = jnp.exp2(m - m_new)
            p = jnp.exp2(s - m_new)
            l = alpha * l + p
            acc = alpha * acc + p * v_refs[j][0, h]
            m = m_new
        m_ref[row, :] = m
        l_ref[row, :] = l
        acc_ref[h] = acc
        return 0

    lax.fori_loop(0, H, head_body, 0)

    @pl.when(blk == nblk - 1)
    def _():
        m_all = m_ref[...]
        w = jnp.exp2(m_all - jnp.max(m_all, axis=1, keepdims=True))
        den = jnp.sum(l_ref[...] * w, axis=1, keepdims=True)
        for h in range(H):
            num = jnp.sum(acc_ref[h] * w[h:h + 1, :], axis=1, keepdims=True)
            o_ref[0, h] = num / den[h:h + 1, :]


def _decode_attention(page_table, q_t, kn_t, vn_t, small, b_f, cache_k_t, cache_v_t, cache_lf_t):
    Bd = q_t.shape[0]
    n_pages = page_table.shape[1]
    page = cache_k_t.shape[3]
    pps = math.gcd(DECODE_PAGES, n_pages)
    pt_flat = page_table.reshape(-1)
    bf2 = jnp.pad(b_f.reshape(1, ATT_HEADS), ((0, 0), (0, SMALL_W - ATT_HEADS)))

    def page_spec(shape, j):
        def imap(b, blk, pt):
            return (pt[b * n_pages + (n_pages - 1 - (blk * pps + j))],) + (0,) * (len(shape) - 1)
        return pl.BlockSpec(shape, imap)

    col_spec = pl.BlockSpec((1, HEAD_DIM, ATT_HEADS), lambda b, blk, pt: (b, 0, 0))
    grid_spec = pltpu.PrefetchScalarGridSpec(
        num_scalar_prefetch=1,
        grid=(Bd, n_pages // pps),
        in_specs=([col_spec, col_spec, col_spec,
                   pl.BlockSpec((1, 1, SMALL_W), lambda b, blk, pt: (b, 0, 0)),
                   pl.BlockSpec((1, SMALL_W), lambda b, blk, pt: (0, 0))]
                  + [page_spec((1, ATT_HEADS, HEAD_DIM, page), j) for j in range(pps)]
                  + [page_spec((1, ATT_HEADS, HEAD_DIM, page), j) for j in range(pps)]
                  + [page_spec((1, ATT_HEADS, page), j) for j in range(pps)]),
        out_specs=[pl.BlockSpec((1, ATT_HEADS, HEAD_DIM, 1), lambda b, blk, pt: (b, 0, 0, 0)),
                   pl.BlockSpec((1, 1, ATT_HEADS), lambda b, blk, pt: (b, 0, 0))],
        scratch_shapes=[pltpu.VMEM((ATT_HEADS, HEAD_DIM, page), F32), pltpu.VMEM((ATT_HEADS, page), F32),
                        pltpu.VMEM((ATT_HEADS, page), F32), pltpu.VMEM((ATT_HEADS, HEAD_DIM, page), F32),
                        pltpu.VMEM((ATT_HEADS, page), F32), pltpu.VMEM((pps, ATT_HEADS, page), F32)],
    )
    return pl.pallas_call(
        functools.partial(_decode_attn_kernel, pps=pps),
        grid_spec=grid_spec,
        out_shape=[jax.ShapeDtypeStruct((Bd, ATT_HEADS, HEAD_DIM, 1), F32),
                   jax.ShapeDtypeStruct((Bd, 1, ATT_HEADS), F32)],
        compiler_params=_cparams("arbitrary", "arbitrary"),
        name="fox_decode",
    )(pt_flat, q_t, kn_t, vn_t, small, bf2, *([cache_k_t] * pps), *([cache_v_t] * pps), *([cache_lf_t] * pps))


def _ssm_step_kernel(xbc_ref, sc_ref, sm_ref, z_ref, h0_ref, cw_ref, cb_ref, dtb_ref, alog_ref, e_ref,
                     dx_ref, nw_ref, y_ref, st_ref):
    H = SSM_HEADS
    rows = [xbc_ref[0]] + [sc_ref[0, CONV_WIDTH - 1 - j:CONV_WIDTH - j, :] for j in range(1, CONV_WIDTH)]
    xc = _conv_silu_rows(rows, cw_ref, cb_ref)
    xs = xc[:, :SSM_WIDTH]
    e = e_ref[...]
    dt = _softplus(sm_ref[0] + dtb_ref[...])
    da = jnp.exp(dt * (-jnp.exp(alog_ref[...])))
    both = _expand_heads(jnp.concatenate([jnp.broadcast_to(dt, (SUBLANES, SMALL_W)),
                                          jnp.broadcast_to(da, (SUBLANES, SMALL_W))], axis=0), e)
    dt_x = both[0:1, :]
    da_x = both[SUBLANES:SUBLANES + 1, :]
    xdt = xs * dt_x

    sub = lax.broadcasted_iota(I32, (H, SSM_WIDTH), 0)
    own = sub == lax.broadcasted_iota(I32, (H, SSM_WIDTH), 1) // SSM_HEAD_DIM

    def masked_parts(v):
        m = jnp.where(own, jnp.broadcast_to(v, (H, SSM_WIDTH)), 0.0)
        hi = m.astype(BF16)
        return hi, (m - hi.astype(F32)).astype(BF16)

    da_hi, da_lo = masked_parts(da_x)
    x_hi, x_lo = masked_parts(xdt)
    lhs = jnp.concatenate([da_hi, da_lo, x_hi, x_lo], axis=0)
    hpg = H // SSM_GROUPS
    grp = lax.broadcasted_iota(I32, (H, D_STATE), 0) // hpg
    b_rows = jnp.zeros((H, D_STATE), F32)
    c_rows = jnp.zeros((H, D_STATE), F32)
    for g in range(SSM_GROUPS):
        bg = xc[:, SSM_WIDTH + g * D_STATE:SSM_WIDTH + (g + 1) * D_STATE]
        cg = xc[:, SSM_WIDTH + (SSM_GROUPS + g) * D_STATE:SSM_WIDTH + (SSM_GROUPS + g + 1) * D_STATE]
        b_rows = jnp.where(grp == g, jnp.broadcast_to(bg, (H, D_STATE)), b_rows)
        c_rows = jnp.where(grp == g, jnp.broadcast_to(cg, (H, D_STATE)), c_rows)
    ones = jnp.ones((2 * H, D_STATE), BF16)
    zeros = jnp.zeros((2 * H, D_STATE), BF16)
    b_bf = b_rows.astype(BF16)
    rhs = jnp.concatenate([jnp.concatenate([ones, zeros], axis=1),
                           jnp.concatenate([zeros, jnp.concatenate([b_bf, b_bf], axis=0)], axis=1)], axis=0)
    mix = lax.dot_general(lhs, rhs, (((0,), (0,)), ((), ())), preferred_element_type=F32)
    h0 = h0_ref[0].reshape(SSM_WIDTH, D_STATE)
    h_new = mix[:, :D_STATE] * h0 + mix[:, D_STATE:]
    st_ref[0] = h_new.reshape(H, SSM_HEAD_DIM, D_STATE)
    y_t = lax.dot_general(c_rows.astype(BF16), h_new.astype(BF16), (((1,), (1,)), ((), ())),
                          preferred_element_type=F32)
    y = jnp.sum(jnp.where(own, y_t, 0.0), axis=0, keepdims=True) + dx_ref[...] * xs
    gated = y * _silu(z_ref[0].astype(F32))
    y_ref[0] = _rmsnorm_rows(gated, nw_ref[...]).astype(y_ref.dtype)


def _ssm_step(xbc, state_conv, small, z, state_ssm, conv_w, conv_b, dtb, alog, dx, e, ssm_norm_w):
    Bd = xbc.shape[0]
    full = lambda a: pl.BlockSpec(a.shape, lambda b: (0,) * a.ndim)
    row = lambda w: pl.BlockSpec((1, 1, w), lambda b: (b, 0, 0))
    st_spec = pl.BlockSpec((1, SSM_HEADS, SSM_HEAD_DIM, D_STATE), lambda b: (b, 0, 0, 0))
    cb = conv_b.reshape(1, CONV_DIM)
    nw = ssm_norm_w.reshape(1, SSM_WIDTH)
    return pl.pallas_call(
        _ssm_step_kernel,
        grid=(Bd,),
        in_specs=[row(CONV_DIM), pl.BlockSpec((1, CONV_WIDTH - 1, CONV_DIM), lambda b: (b, 0, 0)),
                  row(SMALL_W), row(SSM_WIDTH), st_spec,
                  full(conv_w), full(cb), full(dtb), full(alog), full(e), full(dx), full(nw)],
        out_specs=[row(SSM_WIDTH), st_spec],
        out_shape=[jax.ShapeDtypeStruct((Bd, 1, SSM_WIDTH), BF16),
                   jax.ShapeDtypeStruct(state_ssm.shape, F32)],
        compiler_params=_cparams("arbitrary"),
        name="ssm_step",
    )(xbc, state_conv, small, z, state_ssm, conv_w, cb, dtb, alog, e, dx, nw)


def _outproj_kernel(att_ref, ssm_ref, x_ref, g1_ref, sh2_ref, sc2_ref, n2_ref, wa_ref, wsm_ref, rw_ref, rb_ref,
                    cin_ref, x1_ref, h2_ref, rt_ref, tg_ref, cout_ref, cnt_ref):
    @pl.when((pl.program_id(0) == 0) & (pl.program_id(1) == 0))
    def _():
        cnt_ref[...] = cin_ref[...]

    y = (jnp.dot(att_ref[0], wa_ref[...], preferred_element_type=F32)
         + jnp.dot(ssm_ref[0], wsm_ref[...], preferred_element_type=F32))
    x1 = x_ref[0] + g1_ref[0] * y
    x1_ref[0] = x1
    h2 = _rmsnorm_rows(x1, n2_ref[...]) * (1.0 + sc2_ref[0]) + sh2_ref[0]
    h2_ref[0] = h2
    logits = jnp.dot(h2, rw_ref[...], precision=HIGHEST, preferred_element_type=F32) + rb_ref[...]
    tm = logits.shape[0]
    lane = lax.broadcasted_iota(I32, (tm, LANES), 1).astype(F32)
    cur = logits
    idxs = []
    val_tile = jnp.full((tm, LANES), NEG_BIG, F32)
    chosen = jnp.zeros((tm, LANES), F32)
    for k in range(TOP_K):
        m = jnp.max(cur, axis=1, keepdims=True)
        idx = jnp.min(jnp.where(cur == m, lane, float(LANES)), axis=1, keepdims=True)
        idxs.append(idx)
        val_tile = jnp.where(lane == float(k), m, val_tile)
        hit = lane == idx
        chosen = jnp.where(hit, 1.0, chosen)
        cur = jnp.where(hit, 2.0 * NEG_BIG, cur)
    top = jnp.max(val_tile, axis=1, keepdims=True)
    ex = jnp.exp2((val_tile - top) * LOG2E)
    tg_ref[0] = ex / jnp.sum(ex, axis=1, keepdims=True)

    before = (lax.broadcasted_iota(I32, (tm, tm), 1) < lax.broadcasted_iota(I32, (tm, tm), 0)).astype(BF16)
    rank = jnp.dot(before, chosen.astype(BF16), preferred_element_type=F32) + cnt_ref[0:1, :]
    cnt_ref[0:1, :] = cnt_ref[0:1, :] + jnp.sum(chosen, axis=0, keepdims=True)
    cout_ref[...] = cnt_ref[...]
    route = jnp.zeros((tm, LANES), F32)
    for k in range(TOP_K):
        rank_k = jnp.sum(jnp.where(lane == idxs[k], rank, 0.0), axis=1, keepdims=True)
        route = jnp.where(lane == float(k), idxs[k], route)
        route = jnp.where(lane == float(TOP_K + k), rank_k, route)
    rt_ref[0] = route.astype(I32)


def _out_proj(att, ssm, x, g1, sh2, sc2, norm2_w, wa, wsm, rw, rb, counts_in, tm):
    B, L, _ = x.shape
    per_row = g1.shape[1] != 1
    mod_spec = (pl.BlockSpec((1, tm, D_MODEL), lambda b, i: (b, i, 0)) if per_row
                else pl.BlockSpec((1, 1, D_MODEL), lambda b, i: (b, 0, 0)))
    full = lambda a: pl.BlockSpec(a.shape, lambda b, i: (0,) * a.ndim)
    row = lambda w: pl.BlockSpec((1, tm, w), lambda b, i: (b, i, 0))
    n2 = norm2_w.reshape(1, D_MODEL)
    cnt_spec = pl.BlockSpec((SUBLANES, LANES), lambda b, i: (0, 0))
    return pl.pallas_call(
        _outproj_kernel,
        grid=(B, L // tm),
        in_specs=[row(ATT_WIDTH), row(SSM_WIDTH), row(D_MODEL), mod_spec, mod_spec, mod_spec,
                  full(n2), full(wa), full(wsm), full(rw), full(rb), cnt_spec],
        out_specs=[row(D_MODEL), row(D_MODEL), row(LANES), row(LANES), cnt_spec],
        out_shape=[jax.ShapeDtypeStruct((B, L, D_MODEL), F32), jax.ShapeDtypeStruct((B, L, D_MODEL), F32),
                   jax.ShapeDtypeStruct((B, L, LANES), I32), jax.ShapeDtypeStruct((B, L, LANES), F32),
                   jax.ShapeDtypeStruct((SUBLANES, LANES), F32)],
        scratch_shapes=[pltpu.VMEM((SUBLANES, LANES), F32)],
        compiler_params=_cparams("arbitrary", "arbitrary"),
        name="out_proj_route",
    )(att, ssm, x, g1, sh2, sc2, n2, wa, wsm, rw, rb, counts_in)


def _moe_block_tables(counts, n_blocks):
    tb = MOE_ROWS
    nb = (counts + tb - 1) // tb
    cum = jnp.cumsum(nb)
    blk_start = cum - nb
    b = jnp.arange(n_blocks, dtype=I32)
    block_expert = jnp.minimum(jnp.sum((cum[None, :] <= b[:, None]).astype(I32), axis=1), N_EXPERTS - 1)
    rows_left = counts[block_expert] - (b - blk_start[block_expert]) * tb
    block_rows = jnp.where(b < cum[-1], jnp.clip(rows_left, 0, tb), 0).astype(I32)
    prev = jnp.concatenate([jnp.full((1,), -1, I32), block_expert[:-1]])
    block_first = ((block_expert != prev) & (block_rows > 0)).astype(I32)
    return (blk_start * tb).astype(I32), block_expert.astype(I32), block_first, block_rows


def _dispatch_kernel(ps_ref, e_ref, r_ref, h_ref, xin_ref, xs_ref, sem):
    del xin_ref
    tt = h_ref.shape[0]

    def row_copy(r, slot):
        return pltpu.make_async_copy(h_ref.at[pl.ds(r, 1)], xs_ref.at[pl.ds(slot, 1)], sem)

    def issue(r, _):
        for k in range(TOP_K):
            a = r * TOP_K + k
            row_copy(r, ps_ref[e_ref[a]] + r_ref[a]).start()
        return 0

    lax.fori_loop(0, tt, issue, 0)

    def drain(r, _):
        for k in range(TOP_K):
            row_copy(0, 0).wait()
        return 0

    lax.fori_loop(0, tt, drain, 0)


def _moe_dispatch(pad_start, e_flat, r_flat, h2, x_sorted, tt):
    T = h2.shape[0]
    grid_spec = pltpu.PrefetchScalarGridSpec(
        num_scalar_prefetch=1,
        grid=(T // tt,),
        in_specs=[pl.BlockSpec((tt * TOP_K,), lambda i, ps: (i,), memory_space=pltpu.SMEM),
                  pl.BlockSpec((tt * TOP_K,), lambda i, ps: (i,), memory_space=pltpu.SMEM),
                  pl.BlockSpec((tt, D_MODEL), lambda i, ps: (i, 0)),
                  pl.BlockSpec(memory_space=pl.ANY)],
        out_specs=pl.BlockSpec(memory_space=pl.ANY),
        scratch_shapes=[pltpu.SemaphoreType.DMA],
    )
    return pl.pallas_call(
        _dispatch_kernel,
        grid_spec=grid_spec,
        out_shape=jax.ShapeDtypeStruct(x_sorted.shape, x_sorted.dtype),
        input_output_aliases={4: 0},
        compiler_params=_cparams("arbitrary"),
        name="moe_dispatch",
    )(pad_start, e_flat, r_flat, h2, x_sorted)


def _moe_kernel(be_ref, first_ref, rows_ref, x_ref, wgu_ref, bgu_ref, wd_ref, bd_ref, o_ref, wgu_s, wd_s):
    i = pl.program_id(0)

    @pl.when(first_ref[i] == 1)
    def _():
        wgu_s[...] = wgu_ref[0].astype(BF16)
        wd_s[...] = wd_ref[0].astype(BF16)

    @pl.when(rows_ref[i] > 0)
    def _():
        gu = jnp.dot(x_ref[...].astype(BF16), wgu_s[...], preferred_element_type=F32) + bgu_ref[0]
        g = jnp.minimum(gu[:, :D_FF], SWIGLU_LIMIT)
        u = jnp.clip(gu[:, D_FF:], -SWIGLU_LIMIT, SWIGLU_LIMIT)
        act = (u + 1.0) * (g * jax.nn.sigmoid(SWIGLU_ALPHA * g))
        o_ref[...] = jnp.dot(act.astype(BF16), wd_s[...], preferred_element_type=F32) + bd_ref[0]

    @pl.when(rows_ref[i] == 0)
    def _():
        o_ref[...] = jnp.zeros_like(o_ref)


def _moe_blocks(block_expert, block_first, block_rows, x_sorted, w_gate_up, b_gate_up, w_down, b_down):
    n_rows = x_sorted.shape[0]
    tb = MOE_ROWS
    grid_spec = pltpu.PrefetchScalarGridSpec(
        num_scalar_prefetch=3,
        grid=(n_rows // tb,),
        in_specs=[pl.BlockSpec((tb, D_MODEL), lambda i, be, bf, br: (i, 0)),
                  pl.BlockSpec((1, D_MODEL, 2 * D_FF), lambda i, be, bf, br: (be[i], 0, 0)),
                  pl.BlockSpec((1, 1, 2 * D_FF), lambda i, be, bf, br: (be[i], 0, 0)),
                  pl.BlockSpec((1, D_FF, D_MODEL), lambda i, be, bf, br: (be[i], 0, 0)),
                  pl.BlockSpec((1, 1, D_MODEL), lambda i, be, bf, br: (be[i], 0, 0))],
        out_specs=pl.BlockSpec((tb, D_MODEL), lambda i, be, bf, br: (i, 0)),
        scratch_shapes=[pltpu.VMEM((D_MODEL, 2 * D_FF), BF16), pltpu.VMEM((D_FF, D_MODEL), BF16)],
    )
    return pl.pallas_call(
        _moe_kernel,
        grid_spec=grid_spec,
        out_shape=jax.ShapeDtypeStruct((n_rows, D_MODEL), F32),
        compiler_params=_cparams("arbitrary"),
        name="moe_experts",
    )(block_expert, block_first, block_rows, x_sorted, w_gate_up,
      b_gate_up.reshape(N_EXPERTS, 1, 2 * D_FF), w_down, b_down.reshape(N_EXPERTS, 1, D_MODEL))


def _combine_kernel(ps_ref, e_ref, r_ref, x1_ref, tg_ref, g2_ref, nw_ref, ys_ref, o_ref, ybuf, sem):
    tt = x1_ref.shape[1]

    def row_copy(slot, dst):
        return pltpu.make_async_copy(ys_ref.at[pl.ds(slot, 1)], ybuf.at[pl.ds(dst, 1)], sem)

    def issue(r, _):
        for k in range(TOP_K):
            a = r * TOP_K + k
            row_copy(ps_ref[e_ref[a]] + r_ref[a], k * tt + r).start()
        return 0

    lax.fori_loop(0, tt, issue, 0)

    def drain(r, _):
        for k in range(TOP_K):
            row_copy(0, 0).wait()
        return 0

    lax.fori_loop(0, tt, drain, 0)
    gates = tg_ref[0]
    moe = gates[:, 0:1] * ybuf[0:tt, :]
    for k in range(1, TOP_K):
        moe = moe + gates[:, k:k + 1] * ybuf[k * tt:(k + 1) * tt, :]
    o_ref[0] = _rmsnorm_rows(x1_ref[0] + g2_ref[0] * moe, nw_ref[...])


def _moe_combine(pad_start, e_flat, r_flat, x1, gates, g2, final_norm_w, y_sorted, tt):
    B, L, _ = x1.shape
    steps = L // tt
    per_row = g2.shape[1] != 1
    mod_spec = (pl.BlockSpec((1, tt, D_MODEL), lambda b, i, ps: (b, i, 0)) if per_row
                else pl.BlockSpec((1, 1, D_MODEL), lambda b, i, ps: (b, 0, 0)))
    smem_spec = pl.BlockSpec((tt * TOP_K,), lambda b, i, ps: (b * steps + i,), memory_space=pltpu.SMEM)
    nw = final_norm_w.reshape(1, D_MODEL)
    grid_spec = pltpu.PrefetchScalarGridSpec(
        num_scalar_prefetch=1,
        grid=(B, steps),
        in_specs=[smem_spec, smem_spec,
                  pl.BlockSpec((1, tt, D_MODEL), lambda b, i, ps: (b, i, 0)),
                  pl.BlockSpec((1, tt, LANES), lambda b, i, ps: (b, i, 0)),
                  mod_spec, pl.BlockSpec((1, D_MODEL), lambda b, i, ps: (0, 0)),
                  pl.BlockSpec(memory_space=pl.ANY)],
        out_specs=pl.BlockSpec((1, tt, D_MODEL), lambda b, i, ps: (b, i, 0)),
        scratch_shapes=[pltpu.VMEM((TOP_K * tt, D_MODEL), F32), pltpu.SemaphoreType.DMA],
    )
    return pl.pallas_call(
        _combine_kernel,
        grid_spec=grid_spec,
        out_shape=jax.ShapeDtypeStruct((B, L, D_MODEL), F32),
        compiler_params=_cparams("arbitrary", "arbitrary"),
        name="moe_combine_final",
    )(pad_start, e_flat, r_flat, x1, gates, g2, nw, y_sorted)


def kernel(x_prompt, x_sample, c_prompt, c_sample, cache_k, cache_v, cache_lf, state_conv, state_ssm, page_table,
           ada_w, ada_b, norm1_w, w_in, b_f, conv_w, conv_b, dt_bias, A_log, D_skip, ssm_norm_w, w_out,
           norm2_w, router_w, router_b, w_gate_up, b_gate_up, w_down, b_down, final_norm_w):
    assert ada_w.shape[0] == 1, "single-layer trunk"
    B, L, D = x_prompt.shape
    Bd = x_sample.shape[0]
    assert x_sample.shape[1] == 1 and L % SSD_CHUNK == 0

    n_c = B + Bd
    rows = -(-n_c // SUBLANES) * SUBLANES
    c_all = jnp.concatenate([c_prompt, c_sample, jnp.zeros((rows - n_c, D), F32)], axis=0)
    mod = _modulation(c_all, ada_w[0], ada_b[0])
    mod_p = [m.reshape(B, 1, D) for m in jnp.split(mod[:B], 6, axis=-1)]
    mod_s = [m.reshape(1, Bd, D) for m in jnp.split(mod[B:n_c], 6, axis=-1)]

    w = w_in[0]
    o_f = 3 * ATT_WIDTH
    o_z = o_f + ATT_HEADS
    o_x = o_z + SSM_WIDTH
    o_dt = o_x + CONV_DIM
    w_small = jnp.concatenate([w[:, o_f:o_z], w[:, o_dt:o_dt + SSM_HEADS],
                               jnp.zeros((D, SMALL_W - ATT_HEADS - SSM_HEADS), F32)], axis=1)
    wts = (w[:, :ATT_WIDTH].astype(BF16), w[:, ATT_WIDTH:2 * ATT_WIDTH].astype(BF16),
           w[:, 2 * ATT_WIDTH:o_f].astype(BF16), w[:, o_z:o_x].astype(BF16), w[:, o_x:o_dt].astype(BF16), w_small)
    wa = w_out[0][:ATT_WIDTH].astype(BF16)
    wsm = w_out[0][ATT_WIDTH:].astype(BF16)
    rw = jnp.pad(router_w[0], ((0, 0), (0, LANES - N_EXPERTS)))
    rb = jnp.pad(router_b[0].reshape(1, N_EXPERTS), ((0, 0), (0, LANES - N_EXPERTS)), constant_values=NEG_BIG)
    dtb, alog, dx = _ssm_params(dt_bias[0], A_log[0], D_skip[0])
    e = jnp.asarray(_head_expander(), BF16)

    tm_p = min(256, L)
    qb, kb, vb, k_p, v_p, z_p, xbc_p, small_p = _in_proj(x_prompt, mod_p[0], mod_p[1], norm1_w[0], wts, tm_p)
    lf_p, qp, kp, vp = _attn_prep(qb, kb, vb, small_p, b_f[0], min(256, L))
    att_p = _flash_attention(qp, kp, vp, min(1024, L), min(512, L))
    ssm_p, st_p = _ssd_prompt(xbc_p, small_p, z_p, conv_w[0], conv_b[0], dtb, alog, dx, e, ssm_norm_w[0])
    zero_counts = jnp.zeros((SUBLANES, LANES), F32)
    x1_p, h2_p, rt_p, tg_p, counts_p = _out_proj(att_p, ssm_p, x_prompt, mod_p[2], mod_p[3], mod_p[4], norm2_w[0],
                                                 wa, wsm, rw, rb, zero_counts, min(512, L))

    xs_rows = x_sample.reshape(1, Bd, D)
    qb_s, _, _, k_s, v_s, z_s, xbc_s, small_s = _in_proj(xs_rows, mod_s[0], mod_s[1], norm1_w[0], wts, Bd)
    per_row = lambda a: a.reshape(Bd, 1, a.shape[-1])
    head_cols = lambda a: a.reshape(Bd, ATT_HEADS, HEAD_DIM).transpose(0, 2, 1).astype(F32)
    att_s4, lf_s = _decode_attention(
        page_table, head_cols(qb_s), head_cols(k_s), head_cols(v_s), per_row(small_s), b_f[0],
        cache_k[0].transpose(0, 2, 3, 1), cache_v[0].transpose(0, 2, 3, 1), cache_lf[0].transpose(0, 2, 1))
    att_s = att_s4.reshape(1, Bd, ATT_WIDTH).astype(BF16)
    ssm_s, st_s = _ssm_step(per_row(xbc_s), state_conv[0], per_row(small_s), per_row(z_s), state_ssm[0],
                            conv_w[0], conv_b[0], dtb, alog, dx, e, ssm_norm_w[0])
    x1_s, h2_s, rt_s, tg_s, counts = _out_proj(att_s, ssm_s.reshape(1, Bd, SSM_WIDTH), xs_rows,
                                               mod_s[2], mod_s[3], mod_s[4], norm2_w[0], wa, wsm, rw, rb, counts_p, Bd)

    n_tok = B * L + Bd
    n_blocks = (n_tok * TOP_K + N_EXPERTS * (MOE_ROWS - 1)) // MOE_ROWS
    pad_start, b_exp, b_first, b_rows = _moe_block_tables(counts[0, :N_EXPERTS].astype(I32), n_blocks)
    flat = lambda rt, lo: rt[..., lo:lo + TOP_K].reshape(-1)
    tt_p = min(ROUTE_TOKENS, L)
    x_sorted = jnp.zeros((n_blocks * MOE_ROWS, D), F32)
    x_sorted = _moe_dispatch(pad_start, flat(rt_p, 0), flat(rt_p, TOP_K), h2_p.reshape(B * L, D), x_sorted, tt_p)
    x_sorted = _moe_dispatch(pad_start, flat(rt_s, 0), flat(rt_s, TOP_K), h2_s.reshape(Bd, D), x_sorted, Bd)
    y_sorted = _moe_blocks(b_exp, b_first, b_rows, x_sorted, w_gate_up[0], b_gate_up[0], w_down[0], b_down[0])
    y_prompt = _moe_combine(pad_start, flat(rt_p, 0), flat(rt_p, TOP_K), x1_p, tg_p, mod_p[5], final_norm_w,
                            y_sorted, tt_p)
    y_sample = _moe_combine(pad_start, flat(rt_s, 0), flat(rt_s, TOP_K), x1_s, tg_s, mod_s[5], final_norm_w,
                            y_sorted, Bd).reshape(Bd, 1, D)

    conv_s = jnp.concatenate([state_conv[0][:, 1:], xbc_s.reshape(Bd, 1, CONV_DIM)], axis=1)
    ssm_state_p = st_p.reshape(B, D_STATE, SSM_HEADS, SSM_HEAD_DIM).transpose(0, 2, 3, 1)
    return (y_prompt, y_sample,
            k_p.reshape(1, B, L, ATT_HEADS, HEAD_DIM), v_p.reshape(1, B, L, ATT_HEADS, HEAD_DIM),
            lf_p.reshape(1, B, L, ATT_HEADS), xbc_p[:, L - (CONV_WIDTH - 1):].reshape(1, B, CONV_WIDTH - 1, CONV_DIM),
            ssm_state_p.reshape(1, B, SSM_HEADS, SSM_HEAD_DIM, D_STATE),
            k_s.reshape(1, Bd, 1, ATT_HEADS, HEAD_DIM), v_s.reshape(1, Bd, 1, ATT_HEADS, HEAD_DIM),
            lf_s.reshape(1, Bd, 1, ATT_HEADS), conv_s.reshape(1, Bd, CONV_WIDTH - 1, CONV_DIM),
            st_s.reshape(1, Bd, SSM_HEADS, SSM_HEAD_DIM, D_STATE))
```

```python
import functools
import math

import numpy as np
import jax
import jax.numpy as jnp
from jax import lax
from jax.experimental import pallas as pl
from jax.experimental.pallas import tpu as pltpu
from jax.experimental.pallas import tpu_sc as plsc

F32 = jnp.float32
BF16 = jnp.bfloat16
I32 = jnp.int32
HIGHEST = lax.Precision.HIGHEST

D_MODEL = 1024
ATT_HEADS = 16
HEAD_DIM = 64
ATT_WIDTH = ATT_HEADS * HEAD_DIM
SSM_HEADS = 16
SSM_HEAD_DIM = 64
SSM_WIDTH = SSM_HEADS * SSM_HEAD_DIM
SSM_GROUPS = 2
D_STATE = 128
CONV_WIDTH = 4
CONV_DIM = SSM_WIDTH + 2 * SSM_GROUPS * D_STATE
SSD_CHUNK = 128
N_EXPERTS = 32
TOP_K = 4
D_FF = D_MODEL
SWIGLU_LIMIT = 7.0
SWIGLU_ALPHA = 1.702
NORM_EPS = 1e-5

LANES = 128
SUBLANES = 8
SMALL_W = LANES
DT_COL = ATT_HEADS
NEG_BIG = -1e30
LOG2E = math.log2(math.e)
VMEM_LIMIT = 48 * 1024 * 1024
MOE_ROWS = 512
ROUTE_TOKENS = 256


def _cparams(*sem):
    return pltpu.CompilerParams(dimension_semantics=sem, vmem_limit_bytes=VMEM_LIMIT)


def _silu(x):
    return x * jax.nn.sigmoid(x)


def _softplus(x):
    return jnp.maximum(x, 0.0) + jnp.log(1.0 + jnp.exp(-jnp.abs(x)))


def _log_sigmoid(x):
    return -_softplus(-x)


def _rmsnorm_rows(x, w):
    var = jnp.mean(x * x, axis=-1, keepdims=True)
    return x * lax.rsqrt(var + NORM_EPS) * w


def _split3_bf16(x):
    hi = x.astype(BF16)
    r = x - hi.astype(F32)
    mid = r.astype(BF16)
    lo = (r - mid.astype(F32)).astype(BF16)
    return hi, mid, lo


def _split_weight(w):
    hi = w.astype(BF16)
    lo = (w - hi.astype(F32)).astype(BF16)
    return jnp.concatenate([hi, lo], axis=1)


def _dot_split(x, x_hi, w_ref):
    x_lo = (x - x_hi.astype(F32)).astype(BF16)
    both = jnp.dot(x_hi, w_ref[...], preferred_element_type=F32)
    return (both[:, :LANES] + both[:, LANES:]
            + jnp.dot(x_lo, w_ref[:, :LANES], preferred_element_type=F32))


def _mod_kernel(c_ref, w_ref, b_ref, o_ref):
    s = _silu(c_ref[...]).astype(BF16)
    o_ref[...] = jnp.dot(s, w_ref[...].astype(BF16), preferred_element_type=F32) + b_ref[...]


def _modulation(c_all, ada_w, ada_b):
    rows = c_all.shape[0]
    n_out = ada_w.shape[1]
    tn = D_MODEL
    return pl.pallas_call(
        _mod_kernel,
        grid=(n_out // tn,),
        in_specs=[pl.BlockSpec((rows, D_MODEL), lambda j: (0, 0)),
                  pl.BlockSpec((D_MODEL, tn), lambda j: (0, j)),
                  pl.BlockSpec((1, tn), lambda j: (0, j))],
        out_specs=pl.BlockSpec((rows, tn), lambda j: (0, j)),
        out_shape=jax.ShapeDtypeStruct((rows, n_out), F32),
        compiler_params=_cparams("arbitrary"),
        name="adaln_mod",
    )(c_all, ada_w, ada_b.reshape(1, n_out))


def _inproj_kernel(x_ref, sh_ref, sc_ref, nw_ref, wq_ref, wk_ref, wv_ref, wz_ref, wx_ref, ws_ref,
                   qb_ref, kb_ref, vb_ref, k_ref, v_ref, z_ref, xbc_ref, sm_ref):
    h = _rmsnorm_rows(x_ref[0], nw_ref[...]) * (1.0 + sc_ref[0]) + sh_ref[0]
    hb = h.astype(BF16)
    q = jnp.dot(hb, wq_ref[...], preferred_element_type=F32)
    qb_ref[0] = (q * (HEAD_DIM ** -0.5 * LOG2E)).astype(BF16)
    k = jnp.dot(hb, wk_ref[...], preferred_element_type=F32)
    k_ref[0] = k
    kb_ref[0] = k.astype(BF16)
    v = jnp.dot(hb, wv_ref[...], preferred_element_type=F32)
    v_ref[0] = v
    vb_ref[0] = v.astype(BF16)
    z_ref[0] = jnp.dot(hb, wz_ref[...], preferred_element_type=F32).astype(BF16)
    xbc_ref[0] = jnp.dot(hb, wx_ref[...], preferred_element_type=F32)
    sm_ref[0] = _dot_split(h, hb, ws_ref)


def _in_proj(x, sh, sc, norm_w, wts, tm):
    B, L, _ = x.shape
    per_row = sh.shape[1] != 1
    mod_spec = (pl.BlockSpec((1, tm, D_MODEL), lambda b, i: (b, i, 0)) if per_row
                else pl.BlockSpec((1, 1, D_MODEL), lambda b, i: (b, 0, 0)))
    wq, wk, wv, wz, wx, ws = wts

    def wspec(w):
        return pl.BlockSpec(w.shape, lambda b, i: (0, 0), pipeline_mode=pl.Buffered(1))

    def ospec(width):
        return pl.BlockSpec((1, tm, width), lambda b, i: (b, i, 0))

    def oshape(width, dt):
        return jax.ShapeDtypeStruct((B, L, width), dt)

    return pl.pallas_call(
        _inproj_kernel,
        grid=(B, L // tm),
        in_specs=[pl.BlockSpec((1, tm, D_MODEL), lambda b, i: (b, i, 0)), mod_spec, mod_spec,
                  pl.BlockSpec((1, D_MODEL), lambda b, i: (0, 0)),
                  wspec(wq), wspec(wk), wspec(wv), wspec(wz), wspec(wx), wspec(ws)],
        out_specs=[ospec(ATT_WIDTH), ospec(ATT_WIDTH), ospec(ATT_WIDTH), ospec(ATT_WIDTH), ospec(ATT_WIDTH),
                   ospec(SSM_WIDTH), ospec(CONV_DIM), ospec(SMALL_W)],
        out_shape=[oshape(ATT_WIDTH, BF16), oshape(ATT_WIDTH, BF16), oshape(ATT_WIDTH, BF16),
                   oshape(ATT_WIDTH, F32), oshape(ATT_WIDTH, F32),
                   oshape(SSM_WIDTH, BF16), oshape(CONV_DIM, F32), oshape(SMALL_W, F32)],
        compiler_params=_cparams("arbitrary", "arbitrary"),
        name="in_proj",
    )(x, sh, sc, norm_w.reshape(1, D_MODEL), wq, wk, wv, wz, wx, ws)


def _free_half(h):
    return h * LANES + (HEAD_DIM if h % 2 == 0 else 0)


def _bias_selectors():
    wide = ATT_HEADS * LANES
    sq = np.zeros((3, SMALL_W, wide), np.float32)
    sk = np.zeros((3, SMALL_W, wide), np.float32)
    oq = np.zeros((1, wide), np.float32)
    ok = np.zeros((1, wide), np.float32)
    ov = np.zeros((1, wide), np.float32)
    for h in range(ATT_HEADS):
        base = _free_half(h)
        ov[0, base] = 1.0
        for part in range(3):
            sq[part, h, base + part] = 1.0
            ok[0, base + part] = 1.0
            sk[part, h, base + 3 + part] = -1.0
            oq[0, base + 3 + part] = 1.0
    return sq, sk, oq, ok, ov


def _prep_kernel(qb_ref, kb_ref, vb_ref, sm_ref, bf_ref, sq_ref, sk_ref, oq_ref, ok_ref, ov_ref,
                 lf_ref, qp_ref, kp_ref, vp_ref, carry_ref):
    i = pl.program_id(1)
    tl = qb_ref.shape[1]

    @pl.when(i == 0)
    def _():
        carry_ref[...] = jnp.zeros_like(carry_ref)

    lf = _log_sigmoid(sm_ref[0] + bf_ref[...])
    lf_ref[0] = lf[:, :ATT_HEADS]
    row = lax.broadcasted_iota(I32, (tl, tl), 0)
    col = lax.broadcasted_iota(I32, (tl, tl), 1)
    tri = (col <= row).astype(F32)
    fcum = jnp.dot(tri, lf, precision=HIGHEST, preferred_element_type=F32) + carry_ref[0:1, :]
    carry_ref[0:1, :] = fcum[tl - 1:tl, :]
    parts = _split3_bf16(fcum * LOG2E)
    augq = oq_ref[...]
    augk = ok_ref[...]
    for p in range(3):
        augq = augq + jnp.dot(parts[p], sq_ref[p], preferred_element_type=F32)
        augk = augk + jnp.dot(parts[p], sk_ref[p], preferred_element_type=F32)
    lane = lax.broadcasted_iota(I32, (tl, LANES), 1)
    low = lane < HEAD_DIM
    for pair in range(ATT_HEADS // 2):
        qpair = qb_ref[0, :, pair * LANES:(pair + 1) * LANES]
        kpair = kb_ref[0, :, pair * LANES:(pair + 1) * LANES]
        vpair = vb_ref[0, :, pair * LANES:(pair + 1) * LANES]
        for hh in range(2):
            h = 2 * pair + hh
            keep = low if hh == 0 else jnp.logical_not(low)
            sl = slice(h * LANES, (h + 1) * LANES)
            qp_ref[0, :, sl] = jnp.where(keep, qpair, augq[:, sl].astype(BF16))
            kp_ref[0, :, sl] = jnp.where(keep, kpair, augk[:, sl].astype(BF16))
            ones_lane = jnp.broadcast_to(ov_ref[:, sl], (tl, LANES)).astype(BF16)
            vp_ref[0, :, sl] = jnp.where(keep, vpair, ones_lane)


def _attn_prep(qb, kb, vb, small, b_f, tl):
    B, L, _ = qb.shape
    sq, sk, oq, ok, ov = _bias_selectors()
    wide = ATT_HEADS * LANES
    row_spec = lambda w: pl.BlockSpec((1, tl, w), lambda b, i: (b, i, 0))
    full = lambda a: pl.BlockSpec(a.shape, lambda b, i: (0,) * a.ndim)
    sq, sk = jnp.asarray(sq, BF16), jnp.asarray(sk, BF16)
    oq, ok, ov = jnp.asarray(oq), jnp.asarray(ok), jnp.asarray(ov)
    bf2 = jnp.pad(b_f.reshape(1, ATT_HEADS), ((0, 0), (0, SMALL_W - ATT_HEADS)))
    wide_shape = jax.ShapeDtypeStruct((B, L, wide), BF16)
    return pl.pallas_call(
        _prep_kernel,
        grid=(B, L // tl),
        in_specs=[row_spec(ATT_WIDTH), row_spec(ATT_WIDTH), row_spec(ATT_WIDTH), row_spec(SMALL_W), full(bf2),
                  full(sq), full(sk), full(oq), full(ok), full(ov)],
        out_specs=[row_spec(ATT_HEADS), row_spec(wide), row_spec(wide), row_spec(wide)],
        out_shape=[jax.ShapeDtypeStruct((B, L, ATT_HEADS), F32), wide_shape, wide_shape, wide_shape],
        scratch_shapes=[pltpu.VMEM((SUBLANES, LANES), F32)],
        compiler_params=_cparams("arbitrary", "arbitrary"),
        name="attn_prep",
    )(qb, kb, vb, small, bf2, sq, sk, oq, ok, ov)


FLASH_HEADS = 4


def _flash_kernel(qi_ref, ki_ref, qp_ref, kp_ref, vp_ref, o_ref, m_ref, acc_ref, *, tq, tk):
    t = pl.program_id(2)
    qi = qi_ref[t]
    ki = ki_ref[t]
    last = ((qi + 1) * tq - 1) // tk

    @pl.when(ki == 0)
    def _():
        m_ref[...] = jnp.full_like(m_ref, NEG_BIG)
        acc_ref[...] = jnp.zeros_like(acc_ref)

    def step(masked):
        if masked:
            qpos = qi * tq + lax.broadcasted_iota(I32, (tq, tk), 0)
            kpos = ki * tk + lax.broadcasted_iota(I32, (tq, tk), 1)
            visible = kpos <= qpos
        for hh in range(FLASH_HEADS):
            q = qp_ref[0, :, hh * LANES:(hh + 1) * LANES]
            k = kp_ref[0, :, hh * LANES:(hh + 1) * LANES]
            s = lax.dot_general(q, k, (((1,), (1,)), ((), ())), preferred_element_type=F32)
            if masked:
                s = jnp.where(visible, s, NEG_BIG)
            m_prev = m_ref[hh]
            m_new = jnp.maximum(m_prev, jnp.max(s, axis=1, keepdims=True))
            p = jnp.exp2(s - jnp.concatenate([m_new] * (tk // LANES), axis=1))
            acc_ref[hh] = (jnp.exp2(m_prev - m_new) * acc_ref[hh]
                           + jnp.dot(p.astype(BF16), vp_ref[0, :, hh * LANES:(hh + 1) * LANES],
                                     preferred_element_type=F32))
            m_ref[hh] = m_new

    crosses = (ki + 1) * tk - 1 > qi * tq

    @pl.when(crosses)
    def _():
        step(True)

    @pl.when(jnp.logical_not(crosses))
    def _():
        step(False)

    @pl.when(ki == last)
    def _():
        lane = lax.broadcasted_iota(I32, (tq, LANES), 1)
        for pr in range(FLASH_HEADS // 2):
            a0 = acc_ref[2 * pr]
            a1 = acc_ref[2 * pr + 1]
            o0 = a0 / a0[:, HEAD_DIM:HEAD_DIM + 1]
            o1 = a1 / a1[:, 0:1]
            o_ref[0, :, pr * LANES:(pr + 1) * LANES] = jnp.where(lane < HEAD_DIM, o0, o1).astype(o_ref.dtype)


def _flash_attention(qp, kp, vp, tq, tk):
    B, L, _ = qp.shape
    pairs = ATT_HEADS // FLASH_HEADS
    hw = FLASH_HEADS * LANES
    qs, ks = [], []
    for qi in range(L // tq):
        for ki in range(((qi + 1) * tq - 1) // tk + 1):
            qs.append(qi)
            ks.append(ki)
    qi_tab = jnp.asarray(np.array(qs, np.int32))
    ki_tab = jnp.asarray(np.array(ks, np.int32))
    grid_spec = pltpu.PrefetchScalarGridSpec(
        num_scalar_prefetch=2,
        grid=(B, pairs, len(qs)),
        in_specs=[pl.BlockSpec((1, tq, hw), lambda b, p, t, qt, kt: (b, qt[t], p)),
                  pl.BlockSpec((1, tk, hw), lambda b, p, t, qt, kt: (b, kt[t], p)),
                  pl.BlockSpec((1, tk, hw), lambda b, p, t, qt, kt: (b, kt[t], p))],
        out_specs=pl.BlockSpec((1, tq, hw // 2), lambda b, p, t, qt, kt: (b, qt[t], p)),
        scratch_shapes=[pltpu.VMEM((FLASH_HEADS, tq, LANES), F32), pltpu.VMEM((FLASH_HEADS, tq, LANES), F32)],
    )
    return pl.pallas_call(
        functools.partial(_flash_kernel, tq=tq, tk=tk),
        grid_spec=grid_spec,
        out_shape=jax.ShapeDtypeStruct((B, L, ATT_WIDTH), BF16),
        compiler_params=_cparams("arbitrary", "arbitrary", "arbitrary"),
        name="fox_flash",
    )(qi_tab, ki_tab, qp, kp, vp)


def _head_expander():
    e = np.zeros((SMALL_W, SSM_WIDTH), np.float32)
    for h in range(SSM_HEADS):
        e[DT_COL + h, h * SSM_HEAD_DIM:(h + 1) * SSM_HEAD_DIM] = 1.0
    return e


def _expand_heads(vals, e_bf16):
    hi = vals.astype(BF16)
    lo = (vals - hi.astype(F32)).astype(BF16)
    return (jnp.dot(hi, e_bf16, preferred_element_type=F32)
            + jnp.dot(lo, e_bf16, preferred_element_type=F32))


def _conv_silu_rows(rows, cw_ref, cb_ref):
    acc = cb_ref[...] + cw_ref[CONV_WIDTH - 1:CONV_WIDTH, :] * rows[0]
    for j in range(1, CONV_WIDTH):
        acc = acc + cw_ref[CONV_WIDTH - 1 - j:CONV_WIDTH - j, :] * rows[j]
    return _silu(acc)


def _ssd_kernel(xbc_ref, sm_ref, z_ref, cw_ref, cb_ref, dtb_ref, alog_ref, e_ref, dx_ref, nw_ref,
                y_ref, st_ref, buf_ref, ht_ref):
    c = pl.program_id(1)
    nc = pl.num_programs(1)
    Q = SSD_CHUNK
    halo = SUBLANES

    @pl.when(c == 0)
    def _():
        buf_ref[0:halo, :] = jnp.zeros((halo, CONV_DIM), F32)
        ht_ref[...] = jnp.zeros_like(ht_ref)

    @pl.when(c > 0)
    def _():
        buf_ref[0:halo, :] = buf_ref[Q:Q + halo, :]

    buf_ref[halo:halo + Q, :] = xbc_ref[0]
    xc = _conv_silu_rows([buf_ref[halo - j:halo - j + Q, :] for j in range(CONV_WIDTH)], cw_ref, cb_ref)
    xs = xc[:, :SSM_WIDTH]
    e = e_ref[...]

    dt = _softplus(sm_ref[0] + dtb_ref[...])
    a = dt * (-jnp.exp(alog_ref[...]))
    row = lax.broadcasted_iota(I32, (Q, Q), 0)
    col = lax.broadcasted_iota(I32, (Q, Q), 1)
    causal = col <= row
    acum = jnp.dot(causal.astype(F32), a, precision=HIGHEST, preferred_element_type=F32)
    acum_t = acum.T
    dt_x = _expand_heads(dt, e)
    acum_x = _expand_heads(acum, e)
    last_x = acum_x[Q - 1:Q, :]
    xdt = xs * dt_x
    xdt_b = xdt.astype(BF16)
    x_end = (xdt * jnp.exp(last_x - acum_x)).astype(BF16)
    grow = jnp.exp(acum_x)
    cdecay = jnp.exp(last_x)

    lane = lax.broadcasted_iota(I32, (Q, LANES), 1)
    low = lane < SSM_HEAD_DIM
    hpg = SSM_HEADS // SSM_GROUPS
    gw = hpg * SSM_HEAD_DIM
    y_parts = []
    for g in range(SSM_GROUPS):
        bg = xc[:, SSM_WIDTH + g * D_STATE:SSM_WIDTH + (g + 1) * D_STATE].astype(BF16)
        cg = xc[:, SSM_WIDTH + (SSM_GROUPS + g) * D_STATE:SSM_WIDTH + (SSM_GROUPS + g + 1) * D_STATE].astype(BF16)
        scores = lax.dot_general(cg, bg, (((1,), (1,)), ((), ())), preferred_element_type=F32)
        gs = slice(g * gw, (g + 1) * gw)
        h_prev = ht_ref[:, gs]
        y_off = jnp.dot(cg, h_prev.astype(BF16), preferred_element_type=F32) * grow[:, gs]
        ht_ref[:, gs] = h_prev * cdecay[:, gs] + lax.dot_general(
            bg, x_end[:, gs], (((0,), (0,)), ((), ())), preferred_element_type=F32)
        for pr in range(hpg // 2):
            pair_lo = g * gw + pr * LANES
            xpair = xdt_b[:, pair_lo:pair_lo + LANES]
            halves = []
            for hh in range(2):
                h = g * hpg + 2 * pr + hh
                decay = jnp.where(causal, jnp.exp(acum[:, DT_COL + h:DT_COL + h + 1]
                                                  - acum_t[DT_COL + h:DT_COL + h + 1, :]), 0.0)
                halves.append(jnp.dot((scores * decay).astype(BF16), xpair, preferred_element_type=F32))
            y_parts.append(jnp.where(low, halves[0], halves[1]) + y_off[:, pr * LANES:(pr + 1) * LANES])
    y = jnp.concatenate(y_parts, axis=1) + dx_ref[...] * xs
    gated = y * _silu(z_ref[0].astype(F32))
    y_ref[0] = _rmsnorm_rows(gated, nw_ref[...]).astype(y_ref.dtype)

    @pl.when(c == nc - 1)
    def _():
        st_ref[0] = ht_ref[...]


def _ssm_params(dt_bias, A_log, D_skip):
    pad = (DT_COL, SMALL_W - DT_COL - SSM_HEADS)
    dtb = jnp.pad(dt_bias.astype(F32), pad).reshape(1, SMALL_W)
    alog = jnp.pad(A_log.astype(F32), pad).reshape(1, SMALL_W)
    dx = jnp.repeat(D_skip.astype(F32), SSM_HEAD_DIM).reshape(1, SSM_WIDTH)
    return dtb, alog, dx


def _ssd_prompt(xbc, small, z, conv_w, conv_b, dtb, alog, dx, e, ssm_norm_w):
    B, L, _ = xbc.shape
    Q = SSD_CHUNK
    full = lambda a: pl.BlockSpec(a.shape, lambda b, c: (0,) * a.ndim)
    row_spec = lambda w: pl.BlockSpec((1, Q, w), lambda b, c: (b, c, 0))
    cb = conv_b.reshape(1, CONV_DIM)
    nw = ssm_norm_w.reshape(1, SSM_WIDTH)
    return pl.pallas_call(
        _ssd_kernel,
        grid=(B, L // Q),
        in_specs=[row_spec(CONV_DIM), row_spec(SMALL_W), row_spec(SSM_WIDTH),
                  full(conv_w), full(cb), full(dtb), full(alog), full(e), full(dx), full(nw)],
        out_specs=[row_spec(SSM_WIDTH), pl.BlockSpec((1, D_STATE, SSM_WIDTH), lambda b, c: (b, 0, 0))],
        out_shape=[jax.ShapeDtypeStruct((B, L, SSM_WIDTH), BF16),
                   jax.ShapeDtypeStruct((B, D_STATE, SSM_WIDTH), F32)],
        scratch_shapes=[pltpu.VMEM((Q + SUBLANES, CONV_DIM), F32), pltpu.VMEM((D_STATE, SSM_WIDTH), F32)],
        compiler_params=_cparams("arbitrary", "arbitrary"),
        name="ssd_prompt",
    )(xbc, small, z, conv_w, cb, dtb, alog, e, dx, nw)


DECODE_PAGES = 8


def _decode_attn_kernel(pt_ref, qt_ref, knt_ref, vnt_ref, sm_ref, bf_ref, *refs, pps):
    k_refs = refs[0:pps]
    v_refs = refs[pps:2 * pps]
    lf_refs = refs[2 * pps:3 * pps]
    o_ref, lfo_ref, qrep_ref, m_ref, l_ref, acc_ref, carry_ref, bias_ref = refs[3 * pps:]
    blk = pl.program_id(1)
    nblk = pl.num_programs(1)
    H = ATT_HEADS
    page = k_refs[0].shape[3]
    lane_row = lax.broadcasted_iota(I32, (1, page), 1)

    @pl.when(blk == 0)
    def _():
        lf_new = _log_sigmoid(sm_ref[0] + bf_ref[...])
        lfo_ref[0] = lf_new[:, :H]
        diag = (lax.broadcasted_iota(I32, (H, SMALL_W), 0) == lax.broadcasted_iota(I32, (H, SMALL_W), 1))
        lf_col = jnp.sum(jnp.where(diag, jnp.broadcast_to(lf_new, (H, SMALL_W)), 0.0), axis=1, keepdims=True)
        carry_ref[...] = jnp.broadcast_to(lf_col, (H, page))
        qt = qt_ref[0]
        knt = knt_ref[0].astype(BF16).astype(F32)
        vnt = vnt_ref[0].astype(BF16).astype(F32)
        s_row = jnp.sum(qt * knt, axis=0, keepdims=True)
        lane = lax.broadcasted_iota(I32, (HEAD_DIM, page), 1)
        for h in range(H):
            qrep_ref[h] = jnp.broadcast_to(qt[:, h:h + 1], (HEAD_DIM, page))
            m_ref[h:h + 1, :] = jnp.where(lane_row == 0, jnp.broadcast_to(s_row[:, h:h + 1], (1, page)), NEG_BIG)
            acc_ref[h] = jnp.where(lane == 0, jnp.broadcast_to(vnt[:, h:h + 1], (HEAD_DIM, page)), 0.0)
        l_ref[...] = jnp.broadcast_to(jnp.where(lane_row == 0, 1.0, 0.0), (H, page))

    later = (lax.broadcasted_iota(I32, (page, page), 0) > lax.broadcasted_iota(I32, (page, page), 1)).astype(F32)
    carry = carry_ref[...]
    lf_all = jnp.concatenate([lf_refs[j][0] for j in range(pps)], axis=0)
    suffix = jnp.dot(lf_all, later, precision=HIGHEST, preferred_element_type=F32)
    for j in range(pps):
        bias_ref[j] = (suffix[j * H:(j + 1) * H, :] + carry) * LOG2E
        carry = carry + jnp.sum(lf_refs[j][0], axis=1, keepdims=True)
    carry_ref[...] = carry

    def head_body(h, _):
        q3 = qrep_ref[h]
        row = pl.ds(h, 1)
        m = m_ref[row, :]
        l = l_ref[row, :]
        acc = acc_ref[h]
        for j in range(pps):
            s = jnp.sum(q3 * k_refs[j][0, h], axis=0, keepdims=True) + bias_ref[j, row, :]
            m_new = jnp.maximum(m, s)
            alpha = jnp.exp2(m - m_new)
            p = jnp.exp2(s - m_new)
            l = alpha * l + p
            acc = alpha * acc + p * v_refs[j][0, h]
            m = m_new
        m_ref[row, :] = m
        l_ref[row, :] = l
        acc_ref[h] = acc
        return 0

    lax.fori_loop(0, H, head_body, 0)

    @pl.when(blk == nblk - 1)
    def _():
        m_all = m_ref[...]
        w = jnp.exp2(m_all - jnp.max(m_all, axis=1, keepdims=True))
        den = jnp.sum(l_ref[...] * w, axis=1, keepdims=True)
        for h in range(H):
            num = jnp.sum(acc_ref[h] * w[h:h + 1, :], axis=1, keepdims=True)
            o_ref[0, h] = num / den[h:h + 1, :]


def _decode_attention(page_table, q_t, kn_t, vn_t, small, b_f, cache_k_t, cache_v_t, cache_lf_t):
    Bd = q_t.shape[0]
    n_pages = page_table.shape[1]
    page = cache_k_t.shape[3]
    pps = math.gcd(DECODE_PAGES, n_pages)
    pt_flat = page_table.reshape(-1)
    bf2 = jnp.pad(b_f.reshape(1, ATT_HEADS), ((0, 0), (0, SMALL_W - ATT_HEADS)))

    def page_spec(shape, j):
        def imap(b, blk, pt):
            return (pt[b * n_pages + (n_pages - 1 - (blk * pps + j))],) + (0,) * (len(shape) - 1)
        return pl.BlockSpec(shape, imap)

    col_spec = pl.BlockSpec((1, HEAD_DIM, ATT_HEADS), lambda b, blk, pt: (b, 0, 0))
    grid_spec = pltpu.PrefetchScalarGridSpec(
        num_scalar_prefetch=1,
        grid=(Bd, n_pages // pps),
        in_specs=([col_spec, col_spec, col_spec,
                   pl.BlockSpec((1, 1, SMALL_W), lambda b, blk, pt: (b, 0, 0)),
                   pl.BlockSpec((1, SMALL_W), lambda b, blk, pt: (0, 0))]
                  + [page_spec((1, ATT_HEADS, HEAD_DIM, page), j) for j in range(pps)]
                  + [page_spec((1, ATT_HEADS, HEAD_DIM, page), j) for j in range(pps)]
                  + [page_spec((1, ATT_HEADS, page), j) for j in range(pps)]),
        out_specs=[pl.BlockSpec((1, ATT_HEADS, HEAD_DIM, 1), lambda b, blk, pt: (b, 0, 0, 0)),
                   pl.BlockSpec((1, 1, ATT_HEADS), lambda b, blk, pt: (b, 0, 0))],
        scratch_shapes=[pltpu.VMEM((ATT_HEADS, HEAD_DIM, page), F32), pltpu.VMEM((ATT_HEADS, page), F32),
                        pltpu.VMEM((ATT_HEADS, page), F32), pltpu.VMEM((ATT_HEADS, HEAD_DIM, page), F32),
                        pltpu.VMEM((ATT_HEADS, page), F32), pltpu.VMEM((pps, ATT_HEADS, page), F32)],
    )
    return pl.pallas_call(
        functools.partial(_decode_attn_kernel, pps=pps),
        grid_spec=grid_spec,
        out_shape=[jax.ShapeDtypeStruct((Bd, ATT_HEADS, HEAD_DIM, 1), F32),
                   jax.ShapeDtypeStruct((Bd, 1, ATT_HEADS), F32)],
        compiler_params=_cparams("arbitrary", "arbitrary"),
        name="fox_decode",
    )(pt_flat, q_t, kn_t, vn_t, small, bf2, *([cache_k_t] * pps), *([cache_v_t] * pps), *([cache_lf_t] * pps))


def _ssm_step_kernel(xbc_ref, sc_ref, sm_ref, z_ref, h0_ref, cw_ref, cb_ref, dtb_ref, alog_ref, e_ref,
                     dx_ref, nw_ref, y_ref, st_ref):
    H = SSM_HEADS
    rows = [xbc_ref[0]] + [sc_ref[0, CONV_WIDTH - 1 - j:CONV_WIDTH - j, :] for j in range(1, CONV_WIDTH)]
    xc = _conv_silu_rows(rows, cw_ref, cb_ref)
    xs = xc[:, :SSM_WIDTH]
    e = e_ref[...]
    dt = _softplus(sm_ref[0] + dtb_ref[...])
    da = jnp.exp(dt * (-jnp.exp(alog_ref[...])))
    both = _expand_heads(jnp.concatenate([jnp.broadcast_to(dt, (SUBLANES, SMALL_W)),
                                          jnp.broadcast_to(da, (SUBLANES, SMALL_W))], axis=0), e)
    dt_x = both[0:1, :]
    da_x = both[SUBLANES:SUBLANES + 1, :]
    xdt = xs * dt_x

    sub = lax.broadcasted_iota(I32, (H, SSM_WIDTH), 0)
    own = sub == lax.broadcasted_iota(I32, (H, SSM_WIDTH), 1) // SSM_HEAD_DIM

    def masked_parts(v):
        m = jnp.where(own, jnp.broadcast_to(v, (H, SSM_WIDTH)), 0.0)
        hi = m.astype(BF16)
        return hi, (m - hi.astype(F32)).astype(BF16)

    da_hi, da_lo = masked_parts(da_x)
    x_hi, x_lo = masked_parts(xdt)
    lhs = jnp.concatenate([da_hi, da_lo, x_hi, x_lo], axis=0)
    hpg = H // SSM_GROUPS
    grp = lax.broadcasted_iota(I32, (H, D_STATE), 0) // hpg
    b_rows = jnp.zeros((H, D_STATE), F32)
    c_rows = jnp.zeros((H, D_STATE), F32)
    for g in range(SSM_GROUPS):
        bg = xc[:, SSM_WIDTH + g * D_STATE:SSM_WIDTH + (g + 1) * D_STATE]
        cg = xc[:, SSM_WIDTH + (SSM_GROUPS + g) * D_STATE:SSM_WIDTH + (SSM_GROUPS + g + 1) * D_STATE]
        b_rows = jnp.where(grp == g, jnp.broadcast_to(bg, (H, D_STATE)), b_rows)
        c_rows = jnp.where(grp == g, jnp.broadcast_to(cg, (H, D_STATE)), c_rows)
    ones = jnp.ones((2 * H, D_STATE), BF16)
    zeros = jnp.zeros((2 * H, D_STATE), BF16)
    b_bf = b_rows.astype(BF16)
    rhs = jnp.concatenate([jnp.concatenate([ones, zeros], axis=1),
                           jnp.concatenate([zeros, jnp.concatenate([b_bf, b_bf], axis=0)], axis=1)], axis=0)
    mix = lax.dot_general(lhs, rhs, (((0,), (0,)), ((), ())), preferred_element_type=F32)
    h0 = h0_ref[0].reshape(SSM_WIDTH, D_STATE)
    h_new = mix[:, :D_STATE] * h0 + mix[:, D_STATE:]
    st_ref[0] = h_new.reshape(H, SSM_HEAD_DIM, D_STATE)
    y_t = lax.dot_general(c_rows.astype(BF16), h_new.astype(BF16), (((1,), (1,)), ((), ())),
                          preferred_element_type=F32)
    y = jnp.sum(jnp.where(own, y_t, 0.0), axis=0, keepdims=True) + dx_ref[...] * xs
    gated = y * _silu(z_ref[0].astype(F32))
    y_ref[0] = _rmsnorm_rows(gated, nw_ref[...]).astype(y_ref.dtype)


def _ssm_step(xbc, state_conv, small, z, state_ssm, conv_w, conv_b, dtb, alog, dx, e, ssm_norm_w):
    Bd = xbc.shape[0]
    full = lambda a: pl.BlockSpec(a.shape, lambda b: (0,) * a.ndim)
    row = lambda w: pl.BlockSpec((1, 1, w), lambda b: (b, 0, 0))
    st_spec = pl.BlockSpec((1, SSM_HEADS, SSM_HEAD_DIM, D_STATE), lambda b: (b, 0, 0, 0))
    cb = conv_b.reshape(1, CONV_DIM)
    nw = ssm_norm_w.reshape(1, SSM_WIDTH)
    return pl.pallas_call(
        _ssm_step_kernel,
        grid=(Bd,),
        in_specs=[row(CONV_DIM), pl.BlockSpec((1, CONV_WIDTH - 1, CONV_DIM), lambda b: (b, 0, 0)),
                  row(SMALL_W), row(SSM_WIDTH), st_spec,
                  full(conv_w), full(cb), full(dtb), full(alog), full(e), full(dx), full(nw)],
        out_specs=[row(SSM_WIDTH), st_spec],
        out_shape=[jax.ShapeDtypeStruct((Bd, 1, SSM_WIDTH), BF16),
                   jax.ShapeDtypeStruct(state_ssm.shape, F32)],
        compiler_params=_cparams("arbitrary"),
        name="ssm_step",
    )(xbc, state_conv, small, z, state_ssm, conv_w, cb, dtb, alog, e, dx, nw)


def _outproj_kernel(att_ref, ssm_ref, x_ref, g1_ref, sh2_ref, sc2_ref, n2_ref, wa_ref, wsm_ref, rw_ref, rb_ref,
                    cin_ref, x1_ref, h2_ref, rt_ref, tg_ref, cout_ref, cnt_ref):
    @pl.when((pl.program_id(0) == 0) & (pl.program_id(1) == 0))
    def _():
        cnt_ref[...] = cin_ref[...]

    y = (jnp.dot(att_ref[0], wa_ref[...], preferred_element_type=F32)
         + jnp.dot(ssm_ref[0], wsm_ref[...], preferred_element_type=F32))
    x1 = x_ref[0] + g1_ref[0] * y
    x1_ref[0] = x1
    h2 = _rmsnorm_rows(x1, n2_ref[...]) * (1.0 + sc2_ref[0]) + sh2_ref[0]
    h2_ref[0] = h2
    logits = _dot_split(h2, h2.astype(BF16), rw_ref) + rb_ref[...]
    tm = logits.shape[0]
    lane = lax.broadcasted_iota(I32, (tm, LANES), 1).astype(F32)
    cur = logits
    idxs = []
    val_tile = jnp.full((tm, LANES), NEG_BIG, F32)
    chosen = jnp.zeros((tm, LANES), F32)
    for k in range(TOP_K):
        m = jnp.max(cur, axis=1, keepdims=True)
        idx = jnp.min(jnp.where(cur == m, lane, float(LANES)), axis=1, keepdims=True)
        idxs.append(idx)
        val_tile = jnp.where(lane == float(k), m, val_tile)
        hit = lane == idx
        chosen = jnp.where(hit, 1.0, chosen)
        cur = jnp.where(hit, 2.0 * NEG_BIG, cur)
    top = jnp.max(val_tile, axis=1, keepdims=True)
    ex = jnp.exp2((val_tile - top) * LOG2E)
    tg_ref[0] = ex / jnp.sum(ex, axis=1, keepdims=True)

    before = (lax.broadcasted_iota(I32, (tm, tm), 1) < lax.broadcasted_iota(I32, (tm, tm), 0)).astype(BF16)
    rank = jnp.dot(before, chosen.astype(BF16), preferred_element_type=F32) + cnt_ref[0:1, :]
    cnt_ref[0:1, :] = cnt_ref[0:1, :] + jnp.sum(chosen, axis=0, keepdims=True)
    cout_ref[...] = cnt_ref[...]
    route = jnp.zeros((tm, LANES), F32)
    for k in range(TOP_K):
        rank_k = jnp.sum(jnp.where(lane == idxs[k], rank, 0.0), axis=1, keepdims=True)
        route = jnp.where(lane == float(k), idxs[k], route)
        route = jnp.where(lane == float(TOP_K + k), rank_k, route)
    rt_ref[0] = route.astype(I32)


def _out_proj(att, ssm, x, g1, sh2, sc2, norm2_w, wa, wsm, rw, rb, counts_in, tm):
    B, L, _ = x.shape
    per_row = g1.shape[1] != 1
    mod_spec = (pl.BlockSpec((1, tm, D_MODEL), lambda b, i: (b, i, 0)) if per_row
                else pl.BlockSpec((1, 1, D_MODEL), lambda b, i: (b, 0, 0)))
    full = lambda a: pl.BlockSpec(a.shape, lambda b, i: (0,) * a.ndim)
    row = lambda w: pl.BlockSpec((1, tm, w), lambda b, i: (b, i, 0))
    n2 = norm2_w.reshape(1, D_MODEL)
    cnt_spec = pl.BlockSpec((SUBLANES, LANES), lambda b, i: (0, 0))
    return pl.pallas_call(
        _outproj_kernel,
        grid=(B, L // tm),
        in_specs=[row(ATT_WIDTH), row(SSM_WIDTH), row(D_MODEL), mod_spec, mod_spec, mod_spec,
                  full(n2), full(wa), full(wsm), full(rw), full(rb), cnt_spec],
        out_specs=[row(D_MODEL), row(D_MODEL), row(LANES), row(LANES), cnt_spec],
        out_shape=[jax.ShapeDtypeStruct((B, L, D_MODEL), F32), jax.ShapeDtypeStruct((B, L, D_MODEL), F32),
                   jax.ShapeDtypeStruct((B, L, LANES), I32), jax.ShapeDtypeStruct((B, L, LANES), F32),
                   jax.ShapeDtypeStruct((SUBLANES, LANES), F32)],
        scratch_shapes=[pltpu.VMEM((SUBLANES, LANES), F32)],
        compiler_params=_cparams("arbitrary", "arbitrary"),
        name="out_proj_route",
    )(att, ssm, x, g1, sh2, sc2, n2, wa, wsm, rw, rb, counts_in)


def _moe_block_tables(counts, n_blocks):
    tb = MOE_ROWS
    nb = (counts + tb - 1) // tb
    cum = jnp.cumsum(nb)
    blk_start = cum - nb
    b = jnp.arange(n_blocks, dtype=I32)
    block_expert = jnp.minimum(jnp.sum((cum[None, :] <= b[:, None]).astype(I32), axis=1), N_EXPERTS - 1)
    rows_left = counts[block_expert] - (b - blk_start[block_expert]) * tb
    block_rows = jnp.where(b < cum[-1], jnp.clip(rows_left, 0, tb), 0).astype(I32)
    prev = jnp.concatenate([jnp.full((1,), -1, I32), block_expert[:-1]])
    block_first = ((block_expert != prev) & (block_rows > 0)).astype(I32)
    return (blk_start * tb).astype(I32), block_expert.astype(I32), block_first, block_rows


def _dispatch_kernel(ps_ref, e_ref, r_ref, h_ref, xin_ref, xs_ref, sem):
    del xin_ref
    tt = h_ref.shape[0]

    def row_copy(r, slot):
        return pltpu.make_async_copy(h_ref.at[pl.ds(r, 1)], xs_ref.at[pl.ds(slot, 1)], sem)

    def issue(r, _):
        for k in range(TOP_K):
            a = r * TOP_K + k
            row_copy(r, ps_ref[e_ref[a]] + r_ref[a]).start()
        return 0

    lax.fori_loop(0, tt, issue, 0)

    def drain(r, _):
        for k in range(TOP_K):
            row_copy(0, 0).wait()
        return 0

    lax.fori_loop(0, tt, drain, 0)


def _moe_dispatch(pad_start, e_flat, r_flat, h2, x_sorted, tt):
    T = h2.shape[0]
    grid_spec = pltpu.PrefetchScalarGridSpec(
        num_scalar_prefetch=1,
        grid=(T // tt,),
        in_specs=[pl.BlockSpec((tt * TOP_K,), lambda i, ps: (i,), memory_space=pltpu.SMEM),
                  pl.BlockSpec((tt * TOP_K,), lambda i, ps: (i,), memory_space=pltpu.SMEM),
                  pl.BlockSpec((tt, D_MODEL), lambda i, ps: (i, 0)),
                  pl.BlockSpec(memory_space=pl.ANY)],
        out_specs=pl.BlockSpec(memory_space=pl.ANY),
        scratch_shapes=[pltpu.SemaphoreType.DMA],
    )
    return pl.pallas_call(
        _dispatch_kernel,
        grid_spec=grid_spec,
        out_shape=jax.ShapeDtypeStruct(x_sorted.shape, x_sorted.dtype),
        input_output_aliases={4: 0},
        compiler_params=_cparams("arbitrary"),
        name="moe_dispatch",
    )(pad_start, e_flat, r_flat, h2, x_sorted)


def _moe_kernel(be_ref, first_ref, rows_ref, x_ref, wgu_ref, bgu_ref, wd_ref, bd_ref, o_ref, wgu_s, wd_s):
    i = pl.program_id(0)

    @pl.when(first_ref[i] == 1)
    def _():
        wgu_s[...] = wgu_ref[0].astype(BF16)
        wd_s[...] = wd_ref[0].astype(BF16)

    @pl.when(rows_ref[i] > 0)
    def _():
        gu = jnp.dot(x_ref[...].astype(BF16), wgu_s[...], preferred_element_type=F32) + bgu_ref[0]
        g = jnp.minimum(gu[:, :D_FF], SWIGLU_LIMIT)
        u = jnp.clip(gu[:, D_FF:], -SWIGLU_LIMIT, SWIGLU_LIMIT)
        act = (u + 1.0) * (g * jax.nn.sigmoid(SWIGLU_ALPHA * g))
        o_ref[...] = jnp.dot(act.astype(BF16), wd_s[...], preferred_element_type=F32) + bd_ref[0]

    @pl.when(rows_ref[i] == 0)
    def _():
        o_ref[...] = jnp.zeros_like(o_ref)


def _moe_blocks(block_expert, block_first, block_rows, x_sorted, w_gate_up, b_gate_up, w_down, b_down):
    n_rows = x_sorted.shape[0]
    tb = MOE_ROWS
    grid_spec = pltpu.PrefetchScalarGridSpec(
        num_scalar_prefetch=3,
        grid=(n_rows // tb,),
        in_specs=[pl.BlockSpec((tb, D_MODEL), lambda i, be, bf, br: (i, 0)),
                  pl.BlockSpec((1, D_MODEL, 2 * D_FF), lambda i, be, bf, br: (be[i], 0, 0)),
                  pl.BlockSpec((1, 1, 2 * D_FF), lambda i, be, bf, br: (be[i], 0, 0)),
                  pl.BlockSpec((1, D_FF, D_MODEL), lambda i, be, bf, br: (be[i], 0, 0)),
                  pl.BlockSpec((1, 1, D_MODEL), lambda i, be, bf, br: (be[i], 0, 0))],
        out_specs=pl.BlockSpec((tb, D_MODEL), lambda i, be, bf, br: (i, 0)),
        scratch_shapes=[pltpu.VMEM((D_MODEL, 2 * D_FF), BF16), pltpu.VMEM((D_FF, D_MODEL), BF16)],
    )
    return pl.pallas_call(
        _moe_kernel,
        grid_spec=grid_spec,
        out_shape=jax.ShapeDtypeStruct((n_rows, D_MODEL), F32),
        compiler_params=_cparams("arbitrary"),
        name="moe_experts",
    )(block_expert, block_first, block_rows, x_sorted, w_gate_up,
      b_gate_up.reshape(N_EXPERTS, 1, 2 * D_FF), w_down, b_down.reshape(N_EXPERTS, 1, D_MODEL))


def _combine_kernel(ps_ref, e_ref, r_ref, x1_ref, tg_ref, g2_ref, nw_ref, ys_ref, o_ref, ybuf, sem):
    tt = x1_ref.shape[1]

    def row_copy(slot, dst):
        return pltpu.make_async_copy(ys_ref.at[pl.ds(slot, 1)], ybuf.at[pl.ds(dst, 1)], sem)

    def issue(r, _):
        for k in range(TOP_K):
            a = r * TOP_K + k
            row_copy(ps_ref[e_ref[a]] + r_ref[a], k * tt + r).start()
        return 0

    lax.fori_loop(0, tt, issue, 0)

    def drain(r, _):
        for k in range(TOP_K):
            row_copy(0, 0).wait()
        return 0

    lax.fori_loop(0, tt, drain, 0)
    gates = tg_ref[0]
    moe = gates[:, 0:1] * ybuf[0:tt, :]
    for k in range(1, TOP_K):
        moe = moe + gates[:, k:k + 1] * ybuf[k * tt:(k + 1) * tt, :]
    o_ref[0] = _rmsnorm_rows(x1_ref[0] + g2_ref[0] * moe, nw_ref[...])


def _moe_combine(pad_start, e_flat, r_flat, x1, gates, g2, final_norm_w, y_sorted, tt):
    B, L, _ = x1.shape
    steps = L // tt
    per_row = g2.shape[1] != 1
    mod_spec = (pl.BlockSpec((1, tt, D_MODEL), lambda b, i, ps: (b, i, 0)) if per_row
                else pl.BlockSpec((1, 1, D_MODEL), lambda b, i, ps: (b, 0, 0)))
    smem_spec = pl.BlockSpec((tt * TOP_K,), lambda b, i, ps: (b * steps + i,), memory_space=pltpu.SMEM)
    nw = final_norm_w.reshape(1, D_MODEL)
    grid_spec = pltpu.PrefetchScalarGridSpec(
        num_scalar_prefetch=1,
        grid=(B, steps),
        in_specs=[smem_spec, smem_spec,
                  pl.BlockSpec((1, tt, D_MODEL), lambda b, i, ps: (b, i, 0)),
                  pl.BlockSpec((1, tt, LANES), lambda b, i, ps: (b, i, 0)),
                  mod_spec, pl.BlockSpec((1, D_MODEL), lambda b, i, ps: (0, 0)),
                  pl.BlockSpec(memory_space=pl.ANY)],
        out_specs=pl.BlockSpec((1, tt, D_MODEL), lambda b, i, ps: (b, i, 0)),
        scratch_shapes=[pltpu.VMEM((TOP_K * tt, D_MODEL), F32), pltpu.SemaphoreType.DMA],
    )
    return pl.pallas_call(
        _combine_kernel,
        grid_spec=grid_spec,
        out_shape=jax.ShapeDtypeStruct((B, L, D_MODEL), F32),
        compiler_params=_cparams("arbitrary", "arbitrary"),
        name="moe_combine_final",
    )(pad_start, e_flat, r_flat, x1, gates, g2, nw, y_sorted)


SC_WINDOW = 32


def _sc_gather_rows(table, idx):
    n = idx.shape[0]
    d = table.shape[1]
    win = SC_WINDOW
    idx_rows = jnp.pad(idx.reshape(n // win, win), ((0, 0), (0, LANES - win)))
    mesh = plsc.VectorSubcoreMesh(core_axis_name="c", subcore_axis_name="s")

    @functools.partial(pl.kernel, out_type=jax.ShapeDtypeStruct((n, d), table.dtype), mesh=mesh, name="moe_gather")
    def gather(t_hbm, i_hbm, o_hbm):
        def body(i_vmem, o_vmem):
            pltpu.sync_copy(t_hbm.at[i_vmem.at[0, pl.ds(0, win)]], o_vmem)

        pltpu.emit_pipeline(body, grid=(n // win,),
                            in_specs=[pl.BlockSpec((1, LANES), lambda i: (i, 0))],
                            out_specs=[pl.BlockSpec((win, d), lambda i: (i, 0))],
                            core_axis_name=("c", "s"), dimension_semantics=(pltpu.PARALLEL,))(i_hbm, o_hbm)

    return gather(table, idx_rows)


def _final_kernel(x1_ref, tg_ref, g2_ref, nw_ref, y0_ref, y1_ref, y2_ref, y3_ref, o_ref):
    gates = tg_ref[0]
    moe = gates[:, 0:1] * y0_ref[...]
    for k, y_ref in enumerate((y1_ref, y2_ref, y3_ref), start=1):
        moe = moe + gates[:, k:k + 1] * y_ref[...]
    o_ref[0] = _rmsnorm_rows(x1_ref[0] + g2_ref[0] * moe, nw_ref[...])


def _moe_final(x1, gates, g2, final_norm_w, y_rows, row0, tt):
    B, L, _ = x1.shape
    steps = L // tt
    blocks_per_k = B * steps
    base = row0 // tt
    assert row0 % tt == 0 and TOP_K == 4
    per_row = g2.shape[1] != 1
    mod_spec = (pl.BlockSpec((1, tt, D_MODEL), lambda b, i: (b, i, 0)) if per_row
                else pl.BlockSpec((1, 1, D_MODEL), lambda b, i: (b, 0, 0)))
    y_specs = [pl.BlockSpec((tt, D_MODEL), functools.partial(lambda b, i, k: (base + k * blocks_per_k + b * steps + i, 0), k=k))
               for k in range(TOP_K)]
    nw = final_norm_w.reshape(1, D_MODEL)
    return pl.pallas_call(
        _final_kernel,
        grid=(B, steps),
        in_specs=[pl.BlockSpec((1, tt, D_MODEL), lambda b, i: (b, i, 0)),
                  pl.BlockSpec((1, tt, LANES), lambda b, i: (b, i, 0)),
                  mod_spec, pl.BlockSpec((1, D_MODEL), lambda b, i: (0, 0))] + y_specs,
        out_specs=pl.BlockSpec((1, tt, D_MODEL), lambda b, i: (b, i, 0)),
        out_shape=jax.ShapeDtypeStruct((B, L, D_MODEL), F32),
        compiler_params=_cparams("arbitrary", "arbitrary"),
        name="moe_final",
    )(x1, gates, g2, nw, y_rows, y_rows, y_rows, y_rows)


def kernel(x_prompt, x_sample, c_prompt, c_sample, cache_k, cache_v, cache_lf, state_conv, state_ssm, page_table,
           ada_w, ada_b, norm1_w, w_in, b_f, conv_w, conv_b, dt_bias, A_log, D_skip, ssm_norm_w, w_out,
           norm2_w, router_w, router_b, w_gate_up, b_gate_up, w_down, b_down, final_norm_w):
    assert ada_w.shape[0] == 1, "single-layer trunk"
    B, L, D = x_prompt.shape
    Bd = x_sample.shape[0]
    assert x_sample.shape[1] == 1 and L % SSD_CHUNK == 0

    n_c = B + Bd
    rows = -(-n_c // SUBLANES) * SUBLANES
    c_all = jnp.concatenate([c_prompt, c_sample, jnp.zeros((rows - n_c, D), F32)], axis=0)
    mod = _modulation(c_all, ada_w[0], ada_b[0])
    mod_p = [m.reshape(B, 1, D) for m in jnp.split(mod[:B], 6, axis=-1)]
    mod_s = [m.reshape(1, Bd, D) for m in jnp.split(mod[B:n_c], 6, axis=-1)]

    w = w_in[0]
    o_f = 3 * ATT_WIDTH
    o_z = o_f + ATT_HEADS
    o_x = o_z + SSM_WIDTH
    o_dt = o_x + CONV_DIM
    w_small = jnp.concatenate([w[:, o_f:o_z], w[:, o_dt:o_dt + SSM_HEADS],
                               jnp.zeros((D, SMALL_W - ATT_HEADS - SSM_HEADS), F32)], axis=1)
    wts = (w[:, :ATT_WIDTH].astype(BF16), w[:, ATT_WIDTH:2 * ATT_WIDTH].astype(BF16),
           w[:, 2 * ATT_WIDTH:o_f].astype(BF16), w[:, o_z:o_x].astype(BF16), w[:, o_x:o_dt].astype(BF16), _split_weight(w_small))
    wa = w_out[0][:ATT_WIDTH].astype(BF16)
    wsm = w_out[0][ATT_WIDTH:].astype(BF16)
    rw = _split_weight(jnp.pad(router_w[0], ((0, 0), (0, LANES - N_EXPERTS))))
    rb = jnp.pad(router_b[0].reshape(1, N_EXPERTS), ((0, 0), (0, LANES - N_EXPERTS)), constant_values=NEG_BIG)
    dtb, alog, dx = _ssm_params(dt_bias[0], A_log[0], D_skip[0])
    e = jnp.asarray(_head_expander(), BF16)

    tm_p = min(512, L)
    qb, kb, vb, k_p, v_p, z_p, xbc_p, small_p = _in_proj(x_prompt, mod_p[0], mod_p[1], norm1_w[0], wts, tm_p)
    lf_p, qp, kp, vp = _attn_prep(qb, kb, vb, small_p, b_f[0], min(256, L))
    att_p = _flash_attention(qp, kp, vp, min(1024, L), min(512, L))
    ssm_p, st_p = _ssd_prompt(xbc_p, small_p, z_p, conv_w[0], conv_b[0], dtb, alog, dx, e, ssm_norm_w[0])
    zero_counts = jnp.zeros((SUBLANES, LANES), F32)
    x1_p, h2_p, rt_p, tg_p, counts_p = _out_proj(att_p, ssm_p, x_prompt, mod_p[2], mod_p[3], mod_p[4], norm2_w[0],
                                                 wa, wsm, rw, rb, zero_counts, min(512, L))

    xs_rows = x_sample.reshape(1, Bd, D)
    qb_s, _, _, k_s, v_s, z_s, xbc_s, small_s = _in_proj(xs_rows, mod_s[0], mod_s[1], norm1_w[0], wts, Bd)
    per_row = lambda a: a.reshape(Bd, 1, a.shape[-1])
    head_cols = lambda a: a.reshape(Bd, ATT_HEADS, HEAD_DIM).transpose(0, 2, 1).astype(F32)
    att_s4, lf_s = _decode_attention(
        page_table, head_cols(qb_s), head_cols(k_s), head_cols(v_s), per_row(small_s), b_f[0],
        cache_k[0].transpose(0, 2, 3, 1), cache_v[0].transpose(0, 2, 3, 1), cache_lf[0].transpose(0, 2, 1))
    att_s = att_s4.reshape(1, Bd, ATT_WIDTH).astype(BF16)
    ssm_s, st_s = _ssm_step(per_row(xbc_s), state_conv[0], per_row(small_s), per_row(z_s), state_ssm[0],
                            conv_w[0], conv_b[0], dtb, alog, dx, e, ssm_norm_w[0])
    x1_s, h2_s, rt_s, tg_s, counts = _out_proj(att_s, ssm_s.reshape(1, Bd, SSM_WIDTH), xs_rows,
                                               mod_s[2], mod_s[3], mod_s[4], norm2_w[0], wa, wsm, rw, rb, counts_p, Bd)

    n_tok = B * L + Bd
    n_blocks = (n_tok * TOP_K + N_EXPERTS * (MOE_ROWS - 1)) // MOE_ROWS
    pad_start, b_exp, b_first, b_rows = _moe_block_tables(counts[0, :N_EXPERTS].astype(I32), n_blocks)
    flat = lambda rt, lo: rt[..., lo:lo + TOP_K].reshape(-1)
    tt_p = min(ROUTE_TOKENS, L)
    x_sorted = jnp.zeros((n_blocks * MOE_ROWS, D), F32)
    x_sorted = _moe_dispatch(pad_start, flat(rt_p, 0), flat(rt_p, TOP_K), h2_p.reshape(B * L, D), x_sorted, tt_p)
    x_sorted = _moe_dispatch(pad_start, flat(rt_s, 0), flat(rt_s, TOP_K), h2_s.reshape(Bd, D), x_sorted, Bd)
    y_sorted = _moe_blocks(b_exp, b_first, b_rows, x_sorted, w_gate_up[0], b_gate_up[0], w_down[0], b_down[0])

    def slots_k_major(rt):
        e_idx = rt[..., :TOP_K].reshape(-1, TOP_K)
        first = jnp.sum(jnp.where(e_idx[..., None] == jnp.arange(N_EXPERTS, dtype=I32), pad_start, 0), axis=-1)
        return (first + rt[..., TOP_K:2 * TOP_K].reshape(-1, TOP_K)).T.reshape(-1)

    n_assign = n_tok * TOP_K
    chunk = SC_WINDOW * 32
    n_idx = -(-n_assign // chunk) * chunk
    slots = jnp.concatenate([slots_k_major(rt_p), slots_k_major(rt_s), jnp.zeros((n_idx - n_assign,), I32)])
    y_rows = _sc_gather_rows(y_sorted, slots)
    y_prompt = _moe_final(x1_p, tg_p, mod_p[5], final_norm_w, y_rows, 0, min(512, L))
    y_sample = _moe_final(x1_s, tg_s, mod_s[5], final_norm_w, y_rows, B * L * TOP_K, Bd).reshape(Bd, 1, D)

    conv_s = jnp.concatenate([state_conv[0][:, 1:], xbc_s.reshape(Bd, 1, CONV_DIM)], axis=1)
    ssm_state_p = st_p.reshape(B, D_STATE, SSM_HEADS, SSM_HEAD_DIM).transpose(0, 2, 3, 1)
    return (y_prompt, y_sample,
            k_p.reshape(1, B, L, ATT_HEADS, HEAD_DIM), v_p.reshape(1, B, L, ATT_HEADS, HEAD_DIM),
            lf_p.reshape(1, B, L, ATT_HEADS), xbc_p[:, L - (CONV_WIDTH - 1):].reshape(1, B, CONV_WIDTH - 1, CONV_DIM),
            ssm_state_p.reshape(1, B, SSM_HEADS, SSM_HEAD_DIM, D_STATE),
            k_s.reshape(1, Bd, 1, ATT_HEADS, HEAD_DIM), v_s.reshape(1, Bd, 1, ATT_HEADS, HEAD_DIM),
            lf_s.reshape(1, Bd, 1, ATT_HEADS), conv_s.reshape(1, Bd, CONV_WIDTH - 1, CONV_DIM),
            st_s.reshape(1, Bd, SSM_HEADS, SSM_HEAD_DIM, D_STATE))
```

```python
import functools
import math

import numpy as np
import jax
import jax.numpy as jnp
from jax import lax
from jax.experimental import pallas as pl
from jax.experimental.pallas import tpu as pltpu
from jax.experimental.pallas import tpu_sc as plsc

F32 = jnp.float32
BF16 = jnp.bfloat16
I32 = jnp.int32
HIGHEST = lax.Precision.HIGHEST

D_MODEL = 1024
ATT_HEADS = 16
HEAD_DIM = 64
ATT_WIDTH = ATT_HEADS * HEAD_DIM
SSM_HEADS = 16
SSM_HEAD_DIM = 64
SSM_WIDTH = SSM_HEADS * SSM_HEAD_DIM
SSM_GROUPS = 2
D_STATE = 128
CONV_WIDTH = 4
CONV_DIM = SSM_WIDTH + 2 * SSM_GROUPS * D_STATE
SSD_CHUNK = 128
N_EXPERTS = 32
TOP_K = 4
D_FF = D_MODEL
SWIGLU_LIMIT = 7.0
SWIGLU_ALPHA = 1.702
NORM_EPS = 1e-5

LANES = 128
SUBLANES = 8
SMALL_W = LANES
DT_COL = ATT_HEADS
NEG_BIG = -1e30
LOG2E = math.log2(math.e)
VMEM_LIMIT = 48 * 1024 * 1024
INPROJ_VMEM_LIMIT = 58 * 1024 * 1024
MOE_ROWS = 512


def _cparams(*sem):
    return pltpu.CompilerParams(dimension_semantics=sem, vmem_limit_bytes=VMEM_LIMIT)


def _silu(x):
    return x * jax.nn.sigmoid(x)


def _softplus(x):
    return jnp.maximum(x, 0.0) + jnp.log(1.0 + jnp.exp(-jnp.abs(x)))


def _log_sigmoid(x):
    return -_softplus(-x)


def _rmsnorm_rows(x, w):
    var = jnp.mean(x * x, axis=-1, keepdims=True)
    return x * lax.rsqrt(var + NORM_EPS) * w


def _split3_bf16(x):
    hi = x.astype(BF16)
    r = x - hi.astype(F32)
    mid = r.astype(BF16)
    lo = (r - mid.astype(F32)).astype(BF16)
    return hi, mid, lo


def _split_weight(w):
    hi = w.astype(BF16)
    lo = (w - hi.astype(F32)).astype(BF16)
    return jnp.concatenate([hi, lo], axis=1)


def _dot_split(x, x_hi, w_ref):
    x_lo = (x - x_hi.astype(F32)).astype(BF16)
    both = jnp.dot(x_hi, w_ref[...], preferred_element_type=F32)
    return (both[:, :LANES] + both[:, LANES:]
            + jnp.dot(x_lo, w_ref[:, :LANES], preferred_element_type=F32))


def _mod_kernel(c_ref, w_ref, b_ref, o_ref):
    s = _silu(c_ref[...]).astype(BF16)
    o_ref[...] = jnp.dot(s, w_ref[...].astype(BF16), preferred_element_type=F32) + b_ref[...]


def _modulation(c_all, ada_w, ada_b):
    rows = c_all.shape[0]
    n_out = ada_w.shape[1]
    tn = D_MODEL
    return pl.pallas_call(
        _mod_kernel,
        grid=(n_out // tn,),
        in_specs=[pl.BlockSpec((rows, D_MODEL), lambda j: (0, 0)),
                  pl.BlockSpec((D_MODEL, tn), lambda j: (0, j)),
                  pl.BlockSpec((1, tn), lambda j: (0, j))],
        out_specs=pl.BlockSpec((rows, tn), lambda j: (0, j)),
        out_shape=jax.ShapeDtypeStruct((rows, n_out), F32),
        compiler_params=_cparams("arbitrary"),
        name="adaln_mod",
    )(c_all, ada_w, ada_b.reshape(1, n_out))


def _free_half(h):
    return h * LANES + (HEAD_DIM if h % 2 == 0 else 0)


def _bias_layout():
    wide = ATT_HEADS * LANES
    sel = np.zeros((SMALL_W, wide), np.float32)
    rows = np.zeros((5, wide), np.float32)
    for h in range(ATT_HEADS):
        base = _free_half(h)
        rows[4, base] = 1.0
        for part in range(3):
            sel[part * ATT_HEADS + h, base + part] = 1.0
            sel[part * ATT_HEADS + h, base + 3 + part] = -1.0
            rows[0, base + part] = 1.0
            rows[3, base + part] = 1.0
            rows[1, base + 3 + part] = 1.0
            rows[2, base + 3 + part] = 1.0
    return sel, rows


def _inproj_kernel(x_ref, sh_ref, sc_ref, nw_ref, wq_ref, wk_ref, wv_ref, wz_ref, wx_ref, ws_ref, bf_ref,
                   sel_ref, rows_ref, qp_ref, kp_ref, vp_ref, k_ref, v_ref, z_ref, xbc_ref, sm_ref, lf_ref,
                   carry_ref):
    tm = x_ref.shape[1]

    @pl.when(pl.program_id(1) == 0)
    def _():
        carry_ref[...] = jnp.zeros_like(carry_ref)

    h = _rmsnorm_rows(x_ref[0], nw_ref[...]) * (1.0 + sc_ref[0]) + sh_ref[0]
    hb = h.astype(BF16)
    qb = (jnp.dot(hb, wq_ref[...], preferred_element_type=F32) * (HEAD_DIM ** -0.5 * LOG2E)).astype(BF16)
    k = jnp.dot(hb, wk_ref[...], preferred_element_type=F32)
    k_ref[0] = k
    kb = k.astype(BF16)
    v = jnp.dot(hb, wv_ref[...], preferred_element_type=F32)
    v_ref[0] = v
    vb = v.astype(BF16)
    z_ref[0] = jnp.dot(hb, wz_ref[...], preferred_element_type=F32).astype(BF16)
    xbc_ref[0] = jnp.dot(hb, wx_ref[...], preferred_element_type=F32)
    sm = _dot_split(h, hb, ws_ref)
    sm_ref[0] = sm

    lf = _log_sigmoid(sm + bf_ref[...])
    lf_ref[0] = lf[:, :ATT_HEADS]
    tri = (lax.broadcasted_iota(I32, (tm, tm), 1) <= lax.broadcasted_iota(I32, (tm, tm), 0)).astype(BF16)
    sums = jnp.dot(tri, jnp.concatenate(_split3_bf16(lf), axis=1), preferred_element_type=F32)
    fcum = sums[:, :LANES] + sums[:, LANES:2 * LANES] + sums[:, 2 * LANES:] + carry_ref[0:1, :]
    carry_ref[0:1, :] = fcum[tm - 1:tm, :]
    hi, mid, lo = (part.astype(F32) for part in _split3_bf16(fcum * LOG2E))
    lane = lax.broadcasted_iota(I32, (tm, LANES), 1)
    packed = jnp.where(lane < ATT_HEADS, hi,
                       jnp.where(lane < 2 * ATT_HEADS, pltpu.roll(mid, ATT_HEADS, 1),
                                 jnp.where(lane < 3 * ATT_HEADS, pltpu.roll(lo, 2 * ATT_HEADS, 1), 0.0)))
    spread = jnp.dot(packed.astype(BF16), sel_ref[...], preferred_element_type=F32)
    low = lane < HEAD_DIM
    for pair in range(ATT_HEADS // 2):
        ps = slice(pair * LANES, (pair + 1) * LANES)
        for hh in range(2):
            hd = 2 * pair + hh
            keep = low if hh == 0 else jnp.logical_not(low)
            sl = slice(hd * LANES, (hd + 1) * LANES)
            part = spread[:, sl]
            aug_q = (part * rows_ref[0:1, sl] + rows_ref[2:3, sl]).astype(BF16)
            aug_k = (part * rows_ref[1:2, sl] + rows_ref[3:4, sl]).astype(BF16)
            ones_lane = jnp.broadcast_to(rows_ref[4:5, sl], (tm, LANES)).astype(BF16)
            qp_ref[0, :, sl] = jnp.where(keep, qb[:, ps], aug_q)
            kp_ref[0, :, sl] = jnp.where(keep, kb[:, ps], aug_k)
            vp_ref[0, :, sl] = jnp.where(keep, vb[:, ps], ones_lane)


def _in_proj(x, sh, sc, norm_w, wts, b_f, tm):
    B, L, _ = x.shape
    per_row = sh.shape[1] != 1
    mod_spec = (pl.BlockSpec((1, tm, D_MODEL), lambda b, i: (b, i, 0)) if per_row
                else pl.BlockSpec((1, 1, D_MODEL), lambda b, i: (b, 0, 0)))
    wq, wk, wv, wz, wx, ws = wts
    sel, rows = _bias_layout()
    sel, rows = jnp.asarray(sel, BF16), jnp.asarray(rows)
    bf2 = jnp.pad(b_f.reshape(1, ATT_HEADS), ((0, 0), (0, SMALL_W - ATT_HEADS)))
    wide = ATT_HEADS * LANES

    def wspec(w):
        return pl.BlockSpec(w.shape, lambda b, i: (0, 0), pipeline_mode=pl.Buffered(1))

    def ospec(width):
        return pl.BlockSpec((1, tm, width), lambda b, i: (b, i, 0))

    def oshape(width, dt):
        return jax.ShapeDtypeStruct((B, L, width), dt)

    return pl.pallas_call(
        _inproj_kernel,
        grid=(B, L // tm),
        in_specs=[pl.BlockSpec((1, tm, D_MODEL), lambda b, i: (b, i, 0)), mod_spec, mod_spec,
                  pl.BlockSpec((1, D_MODEL), lambda b, i: (0, 0)),
                  wspec(wq), wspec(wk), wspec(wv), wspec(wz), wspec(wx), wspec(ws),
                  wspec(bf2), wspec(sel), wspec(rows)],
        out_specs=[ospec(wide), ospec(wide), ospec(wide), ospec(ATT_WIDTH), ospec(ATT_WIDTH),
                   ospec(SSM_WIDTH), ospec(CONV_DIM), ospec(SMALL_W), ospec(ATT_HEADS)],
        out_shape=[oshape(wide, BF16), oshape(wide, BF16), oshape(wide, BF16),
                   oshape(ATT_WIDTH, F32), oshape(ATT_WIDTH, F32),
                   oshape(SSM_WIDTH, BF16), oshape(CONV_DIM, F32), oshape(SMALL_W, F32), oshape(ATT_HEADS, F32)],
        scratch_shapes=[pltpu.VMEM((SUBLANES, LANES), F32)],
        compiler_params=pltpu.CompilerParams(dimension_semantics=("arbitrary", "arbitrary"),
                                             vmem_limit_bytes=INPROJ_VMEM_LIMIT),
        name="in_proj",
    )(x, sh, sc, norm_w.reshape(1, D_MODEL), wq, wk, wv, wz, wx, ws, bf2, sel, rows)


FLASH_HEADS = 8


def _flash_kernel(qi_ref, ki_ref, qp_ref, kp_ref, vp_ref, o_ref, m_ref, acc_ref, *, tq, tk):
    t = pl.program_id(2)
    qi = qi_ref[t]
    ki = ki_ref[t]
    last = ((qi + 1) * tq - 1) // tk

    @pl.when(ki == 0)
    def _():
        m_ref[...] = jnp.full_like(m_ref, NEG_BIG)
        acc_ref[...] = jnp.zeros_like(acc_ref)

    def step(masked):
        if masked:
            qpos = qi * tq + lax.broadcasted_iota(I32, (tq, tk), 0)
            kpos = ki * tk + lax.broadcasted_iota(I32, (tq, tk), 1)
            visible = kpos <= qpos
        for hh in range(FLASH_HEADS):
            q = qp_ref[0, :, hh * LANES:(hh + 1) * LANES]
            k = kp_ref[0, :, hh * LANES:(hh + 1) * LANES]
            s = lax.dot_general(q, k, (((1,), (1,)), ((), ())), preferred_element_type=F32)
            if masked:
                s = jnp.where(visible, s, NEG_BIG)
            m_prev = m_ref[hh]
            m_new = jnp.maximum(m_prev, jnp.max(s, axis=1, keepdims=True))
            p = jnp.exp2(s - jnp.concatenate([m_new] * (tk // LANES), axis=1))
            acc_ref[hh] = (jnp.exp2(m_prev - m_new) * acc_ref[hh]
                           + jnp.dot(p.astype(BF16), vp_ref[0, :, hh * LANES:(hh + 1) * LANES],
                                     preferred_element_type=F32))
            m_ref[hh] = m_new

    crosses = (ki + 1) * tk - 1 > qi * tq

    @pl.when(crosses)
    def _():
        step(True)

    @pl.when(jnp.logical_not(crosses))
    def _():
        step(False)

    @pl.when(ki == last)
    def _():
        lane = lax.broadcasted_iota(I32, (tq, LANES), 1)
        for pr in range(FLASH_HEADS // 2):
            a0 = acc_ref[2 * pr]
            a1 = acc_ref[2 * pr + 1]
            o0 = a0 / a0[:, HEAD_DIM:HEAD_DIM + 1]
            o1 = a1 / a1[:, 0:1]
            o_ref[0, :, pr * LANES:(pr + 1) * LANES] = jnp.where(lane < HEAD_DIM, o0, o1).astype(o_ref.dtype)


def _flash_attention(qp, kp, vp, tq, tk):
    B, L, _ = qp.shape
    pairs = ATT_HEADS // FLASH_HEADS
    hw = FLASH_HEADS * LANES
    qs, ks = [], []
    for qi in range(L // tq):
        for ki in range(((qi + 1) * tq - 1) // tk + 1):
            qs.append(qi)
            ks.append(ki)
    qi_tab = jnp.asarray(np.array(qs, np.int32))
    ki_tab = jnp.asarray(np.array(ks, np.int32))
    grid_spec = pltpu.PrefetchScalarGridSpec(
        num_scalar_prefetch=2,
        grid=(B, pairs, len(qs)),
        in_specs=[pl.BlockSpec((1, tq, hw), lambda b, p, t, qt, kt: (b, qt[t], p)),
                  pl.BlockSpec((1, tk, hw), lambda b, p, t, qt, kt: (b, kt[t], p)),
                  pl.BlockSpec((1, tk, hw), lambda b, p, t, qt, kt: (b, kt[t], p))],
        out_specs=pl.BlockSpec((1, tq, hw // 2), lambda b, p, t, qt, kt: (b, qt[t], p)),
        scratch_shapes=[pltpu.VMEM((FLASH_HEADS, tq, LANES), F32), pltpu.VMEM((FLASH_HEADS, tq, LANES), F32)],
    )
    return pl.pallas_call(
        functools.partial(_flash_kernel, tq=tq, tk=tk),
        grid_spec=grid_spec,
        out_shape=jax.ShapeDtypeStruct((B, L, ATT_WIDTH), BF16),
        compiler_params=_cparams("arbitrary", "arbitrary", "arbitrary"),
        name="fox_flash",
    )(qi_tab, ki_tab, qp, kp, vp)


def _head_expander():
    e = np.zeros((SMALL_W, SSM_WIDTH), np.float32)
    for h in range(SSM_HEADS):
        e[DT_COL + h, h * SSM_HEAD_DIM:(h + 1) * SSM_HEAD_DIM] = 1.0
    return e


def _expand_heads(vals, e_bf16):
    hi = vals.astype(BF16)
    lo = (vals - hi.astype(F32)).astype(BF16)
    return (jnp.dot(hi, e_bf16, preferred_element_type=F32)
            + jnp.dot(lo, e_bf16, preferred_element_type=F32))


def _conv_silu_rows(rows, cw_ref, cb_ref):
    acc = cb_ref[...] + cw_ref[CONV_WIDTH - 1:CONV_WIDTH, :] * rows[0]
    for j in range(1, CONV_WIDTH):
        acc = acc + cw_ref[CONV_WIDTH - 1 - j:CONV_WIDTH - j, :] * rows[j]
    return _silu(acc)


def _ssd_kernel(xbc_ref, sm_ref, z_ref, cw_ref, cb_ref, dtb_ref, alog_ref, e_ref, dx_ref, nw_ref,
                y_ref, st_ref, buf_ref, ht_ref):
    c = pl.program_id(1)
    nc = pl.num_programs(1)
    Q = SSD_CHUNK
    halo = SUBLANES

    @pl.when(c == 0)
    def _():
        buf_ref[0:halo, :] = jnp.zeros((halo, CONV_DIM), F32)
        ht_ref[...] = jnp.zeros_like(ht_ref)

    @pl.when(c > 0)
    def _():
        buf_ref[0:halo, :] = buf_ref[Q:Q + halo, :]

    buf_ref[halo:halo + Q, :] = xbc_ref[0]
    xc = _conv_silu_rows([buf_ref[halo - j:halo - j + Q, :] for j in range(CONV_WIDTH)], cw_ref, cb_ref)
    xs = xc[:, :SSM_WIDTH]
    e = e_ref[...]

    dt = _softplus(sm_ref[0] + dtb_ref[...])
    a = dt * (-jnp.exp(alog_ref[...]))
    row = lax.broadcasted_iota(I32, (Q, Q), 0)
    col = lax.broadcasted_iota(I32, (Q, Q), 1)
    causal = col <= row
    acum = jnp.dot(causal.astype(F32), a, precision=HIGHEST, preferred_element_type=F32)
    acum_t = acum.T
    dt_x = _expand_heads(dt, e)
    acum_x = _expand_heads(acum, e)
    last_x = acum_x[Q - 1:Q, :]
    xdt = xs * dt_x
    xdt_b = xdt.astype(BF16)
    x_end = (xdt * jnp.exp(last_x - acum_x)).astype(BF16)
    grow = jnp.exp(acum_x)
    cdecay = jnp.exp(last_x)

    lane = lax.broadcasted_iota(I32, (Q, LANES), 1)
    low = lane < SSM_HEAD_DIM
    hpg = SSM_HEADS // SSM_GROUPS
    gw = hpg * SSM_HEAD_DIM
    y_parts = []
    for g in range(SSM_GROUPS):
        bg = xc[:, SSM_WIDTH + g * D_STATE:SSM_WIDTH + (g + 1) * D_STATE].astype(BF16)
        cg = xc[:, SSM_WIDTH + (SSM_GROUPS + g) * D_STATE:SSM_WIDTH + (SSM_GROUPS + g + 1) * D_STATE].astype(BF16)
        scores = lax.dot_general(cg, bg, (((1,), (1,)), ((), ())), preferred_element_type=F32)
        gs = slice(g * gw, (g + 1) * gw)
        h_prev = ht_ref[:, gs]
        y_off = jnp.dot(cg, h_prev.astype(BF16), preferred_element_type=F32) * grow[:, gs]
        ht_ref[:, gs] = h_prev * cdecay[:, gs] + lax.dot_general(
            bg, x_end[:, gs], (((0,), (0,)), ((), ())), preferred_element_type=F32)
        for pr in range(hpg // 2):
            pair_lo = g * gw + pr * LANES
            xpair = xdt_b[:, pair_lo:pair_lo + LANES]
            halves = []
            for hh in range(2):
                h = g * hpg + 2 * pr + hh
                decay = jnp.where(causal, jnp.exp(acum[:, DT_COL + h:DT_COL + h + 1]
                                                  - acum_t[DT_COL + h:DT_COL + h + 1, :]), 0.0)
                halves.append(jnp.dot((scores * decay).astype(BF16), xpair, preferred_element_type=F32))
            y_parts.append(jnp.where(low, halves[0], halves[1]) + y_off[:, pr * LANES:(pr + 1) * LANES])
    y = jnp.concatenate(y_parts, axis=1) + dx_ref[...] * xs
    gated = y * _silu(z_ref[0].astype(F32))
    y_ref[0] = _rmsnorm_rows(gated, nw_ref[...]).astype(y_ref.dtype)

    @pl.when(c == nc - 1)
    def _():
        st_ref[0] = ht_ref[...]


def _ssm_params(dt_bias, A_log, D_skip):
    pad = (DT_COL, SMALL_W - DT_COL - SSM_HEADS)
    dtb = jnp.pad(dt_bias.astype(F32), pad).reshape(1, SMALL_W)
    alog = jnp.pad(A_log.astype(F32), pad).reshape(1, SMALL_W)
    dx = jnp.repeat(D_skip.astype(F32), SSM_HEAD_DIM).reshape(1, SSM_WIDTH)
    return dtb, alog, dx


def _ssd_prompt(xbc, small, z, conv_w, conv_b, dtb, alog, dx, e, ssm_norm_w):
    B, L, _ = xbc.shape
    Q = SSD_CHUNK
    full = lambda a: pl.BlockSpec(a.shape, lambda b, c: (0,) * a.ndim)
    row_spec = lambda w: pl.BlockSpec((1, Q, w), lambda b, c: (b, c, 0))
    cb = conv_b.reshape(1, CONV_DIM)
    nw = ssm_norm_w.reshape(1, SSM_WIDTH)
    return pl.pallas_call(
        _ssd_kernel,
        grid=(B, L // Q),
        in_specs=[row_spec(CONV_DIM), row_spec(SMALL_W), row_spec(SSM_WIDTH),
                  full(conv_w), full(cb), full(dtb), full(alog), full(e), full(dx), full(nw)],
        out_specs=[row_spec(SSM_WIDTH), pl.BlockSpec((1, D_STATE, SSM_WIDTH), lambda b, c: (b, 0, 0))],
        out_shape=[jax.ShapeDtypeStruct((B, L, SSM_WIDTH), BF16),
                   jax.ShapeDtypeStruct((B, D_STATE, SSM_WIDTH), F32)],
        scratch_shapes=[pltpu.VMEM((Q + SUBLANES, CONV_DIM), F32), pltpu.VMEM((D_STATE, SSM_WIDTH), F32)],
        compiler_params=_cparams("arbitrary", "arbitrary"),
        name="ssd_prompt",
    )(xbc, small, z, conv_w, cb, dtb, alog, e, dx, nw)


DECODE_PAGES = 8


def _decode_attn_kernel(pt_ref, qt_ref, knt_ref, vnt_ref, sm_ref, bf_ref, *refs, pps):
    k_refs = refs[0:pps]
    v_refs = refs[pps:2 * pps]
    lf_refs = refs[2 * pps:3 * pps]
    o_ref, lfo_ref, qrep_ref, m_ref, l_ref, acc_ref, carry_ref, bias_ref = refs[3 * pps:]
    blk = pl.program_id(1)
    nblk = pl.num_programs(1)
    H = ATT_HEADS
    page = k_refs[0].shape[3]
    lane_row = lax.broadcasted_iota(I32, (1, page), 1)

    @pl.when(blk == 0)
    def _():
        lf_new = _log_sigmoid(sm_ref[0] + bf_ref[...])
        lfo_ref[0] = lf_new[:, :H]
        diag = (lax.broadcasted_iota(I32, (H, SMALL_W), 0) == lax.broadcasted_iota(I32, (H, SMALL_W), 1))
        lf_col = jnp.sum(jnp.where(diag, jnp.broadcast_to(lf_new, (H, SMALL_W)), 0.0), axis=1, keepdims=True)
        carry_ref[...] = jnp.broadcast_to(lf_col, (H, page))
        qt = qt_ref[0]
        knt = knt_ref[0].astype(BF16).astype(F32)
        vnt = vnt_ref[0].astype(BF16).astype(F32)
        s_row = jnp.sum(qt * knt, axis=0, keepdims=True)
        lane = lax.broadcasted_iota(I32, (HEAD_DIM, page), 1)
        for h in range(H):
            qrep_ref[h] = jnp.broadcast_to(qt[:, h:h + 1], (HEAD_DIM, page))
            m_ref[h:h + 1, :] = jnp.where(lane_row == 0, jnp.broadcast_to(s_row[:, h:h + 1], (1, page)), NEG_BIG)
            acc_ref[h] = jnp.where(lane == 0, jnp.broadcast_to(vnt[:, h:h + 1], (HEAD_DIM, page)), 0.0)
        l_ref[...] = jnp.broadcast_to(jnp.where(lane_row == 0, 1.0, 0.0), (H, page))

    later = (lax.broadcasted_iota(I32, (page, page), 0) > lax.broadcasted_iota(I32, (page, page), 1)).astype(F32)
    carry = carry_ref[...]
    lf_all = jnp.concatenate([lf_refs[j][0] for j in range(pps)], axis=0)
    suffix = jnp.dot(lf_all, later, precision=HIGHEST, preferred_element_type=F32)
    for j in range(pps):
        bias_ref[j] = (suffix[j * H:(j + 1) * H, :] + carry) * LOG2E
        carry = carry + jnp.sum(lf_refs[j][0], axis=1, keepdims=True)
    carry_ref[...] = carry

    def head_body(h, _):
        q3 = qrep_ref[h]
        row = pl.ds(h, 1)
        m = m_ref[row, :]
        l = l_ref[row, :]
        acc = acc_ref[h]
        for j in range(pps):
            s = jnp.sum(q3 * k_refs[j][0, h], axis=0, keepdims=True) + bias_ref[j, row, :]
            m_new = jnp.maximum(m, s)
            alpha = jnp.exp2(m - m_new)
            p = jnp.exp2(s - m_new)
            l = alpha * l + p
            acc = alpha * acc + p * v_refs[j][0, h]
            m = m_new
        m_ref[row, :] = m
        l_ref[row, :] = l
        acc_ref[h] = acc
        return 0

    lax.fori_loop(0, H, head_body, 0)

    @pl.when(blk == nblk - 1)
    def _():
        m_all = m_ref[...]
        w = jnp.exp2(m_all - jnp.max(m_all, axis=1, keepdims=True))
        den = jnp.sum(l_ref[...] * w, axis=1, keepdims=True)
        for h in range(H):
            num = jnp.sum(acc_ref[h] * w[h:h + 1, :], axis=1, keepdims=True)
            o_ref[0, h] = num / den[h:h + 1, :]


def _decode_attention(page_table, q_t, kn_t, vn_t, small, b_f, cache_k_t, cache_v_t, cache_lf_t):
    Bd = q_t.shape[0]
    n_pages = page_table.shape[1]
    page = cache_k_t.shape[3]
    pps = math.gcd(DECODE_PAGES, n_pages)
    pt_flat = page_table.reshape(-1)
    bf2 = jnp.pad(b_f.reshape(1, ATT_HEADS), ((0, 0), (0, SMALL_W - ATT_HEADS)))

    def page_spec(shape, j):
        def imap(b, blk, pt):
            return (pt[b * n_pages + (n_pages - 1 - (blk * pps + j))],) + (0,) * (len(shape) - 1)
        return pl.BlockSpec(shape, imap)

    col_spec = pl.BlockSpec((1, HEAD_DIM, ATT_HEADS), lambda b, blk, pt: (b, 0, 0))
    grid_spec = pltpu.PrefetchScalarGridSpec(
        num_scalar_prefetch=1,
        grid=(Bd, n_pages // pps),
        in_specs=([col_spec, col_spec, col_spec,
                   pl.BlockSpec((1, 1, SMALL_W), lambda b, blk, pt: (b, 0, 0)),
                   pl.BlockSpec((1, SMALL_W), lambda b, blk, pt: (0, 0))]
                  + [page_spec((1, ATT_HEADS, HEAD_DIM, page), j) for j in range(pps)]
                  + [page_spec((1, ATT_HEADS, HEAD_DIM, page), j) for j in range(pps)]
                  + [page_spec((1, ATT_HEADS, page), j) for j in range(pps)]),
        out_specs=[pl.BlockSpec((1, ATT_HEADS, HEAD_DIM, 1), lambda b, blk, pt: (b, 0, 0, 0)),
                   pl.BlockSpec((1, 1, ATT_HEADS), lambda b, blk, pt: (b, 0, 0))],
        scratch_shapes=[pltpu.VMEM((ATT_HEADS, HEAD_DIM, page), F32), pltpu.VMEM((ATT_HEADS, page), F32),
                        pltpu.VMEM((ATT_HEADS, page), F32), pltpu.VMEM((ATT_HEADS, HEAD_DIM, page), F32),
                        pltpu.VMEM((ATT_HEADS, page), F32), pltpu.VMEM((pps, ATT_HEADS, page), F32)],
    )
    return pl.pallas_call(
        functools.partial(_decode_attn_kernel, pps=pps),
        grid_spec=grid_spec,
        out_shape=[jax.ShapeDtypeStruct((Bd, ATT_HEADS, HEAD_DIM, 1), F32),
                   jax.ShapeDtypeStruct((Bd, 1, ATT_HEADS), F32)],
        compiler_params=_cparams("arbitrary", "arbitrary"),
        name="fox_decode",
    )(pt_flat, q_t, kn_t, vn_t, small, bf2, *([cache_k_t] * pps), *([cache_v_t] * pps), *([cache_lf_t] * pps))


def _ssm_step_kernel(xbc_ref, sc_ref, sm_ref, z_ref, h0_ref, cw_ref, cb_ref, dtb_ref, alog_ref, e_ref,
                     dx_ref, nw_ref, y_ref, st_ref):
    H = SSM_HEADS
    rows = [xbc_ref[0]] + [sc_ref[0, CONV_WIDTH - 1 - j:CONV_WIDTH - j, :] for j in range(1, CONV_WIDTH)]
    xc = _conv_silu_rows(rows, cw_ref, cb_ref)
    xs = xc[:, :SSM_WIDTH]
    e = e_ref[...]
    dt = _softplus(sm_ref[0] + dtb_ref[...])
    da = jnp.exp(dt * (-jnp.exp(alog_ref[...])))
    both = _expand_heads(jnp.concatenate([jnp.broadcast_to(dt, (SUBLANES, SMALL_W)),
                                          jnp.broadcast_to(da, (SUBLANES, SMALL_W))], axis=0), e)
    dt_x = both[0:1, :]
    da_x = both[SUBLANES:SUBLANES + 1, :]
    xdt = xs * dt_x

    sub = lax.broadcasted_iota(I32, (H, SSM_WIDTH), 0)
    own = sub == lax.broadcasted_iota(I32, (H, SSM_WIDTH), 1) // SSM_HEAD_DIM

    def masked_parts(v):
        m = jnp.where(own, jnp.broadcast_to(v, (H, SSM_WIDTH)), 0.0)
        hi = m.astype(BF16)
        return hi, (m - hi.astype(F32)).astype(BF16)

    da_hi, da_lo = masked_parts(da_x)
    x_hi, x_lo = masked_parts(xdt)
    lhs = jnp.concatenate([da_hi, da_lo, x_hi, x_lo], axis=0)
    hpg = H // SSM_GROUPS
    grp = lax.broadcasted_iota(I32, (H, D_STATE), 0) // hpg
    b_rows = jnp.zeros((H, D_STATE), F32)
    c_rows = jnp.zeros((H, D_STATE), F32)
    for g in range(SSM_GROUPS):
        bg = xc[:, SSM_WIDTH + g * D_STATE:SSM_WIDTH + (g + 1) * D_STATE]
        cg = xc[:, SSM_WIDTH + (SSM_GROUPS + g) * D_STATE:SSM_WIDTH + (SSM_GROUPS + g + 1) * D_STATE]
        b_rows = jnp.where(grp == g, jnp.broadcast_to(bg, (H, D_STATE)), b_rows)
        c_rows = jnp.where(grp == g, jnp.broadcast_to(cg, (H, D_STATE)), c_rows)
    ones = jnp.ones((2 * H, D_STATE), BF16)
    zeros = jnp.zeros((2 * H, D_STATE), BF16)
    b_bf = b_rows.astype(BF16)
    rhs = jnp.concatenate([jnp.concatenate([ones, zeros], axis=1),
                           jnp.concatenate([zeros, jnp.concatenate([b_bf, b_bf], axis=0)], axis=1)], axis=0)
    mix = lax.dot_general(lhs, rhs, (((0,), (0,)), ((), ())), preferred_element_type=F32)
    h0 = h0_ref[0].reshape(SSM_WIDTH, D_STATE)
    h_new = mix[:, :D_STATE] * h0 + mix[:, D_STATE:]
    st_ref[0] = h_new.reshape(H, SSM_HEAD_DIM, D_STATE)
    y_t = lax.dot_general(c_rows.astype(BF16), h_new.astype(BF16), (((1,), (1,)), ((), ())),
                          preferred_element_type=F32)
    y = jnp.sum(jnp.where(own, y_t, 0.0), axis=0, keepdims=True) + dx_ref[...] * xs
    gated = y * _silu(z_ref[0].astype(F32))
    y_ref[0] = _rmsnorm_rows(gated, nw_ref[...]).astype(y_ref.dtype)


def _ssm_step(xbc, state_conv, small, z, state_ssm, conv_w, conv_b, dtb, alog, dx, e, ssm_norm_w):
    Bd = xbc.shape[0]
    full = lambda a: pl.BlockSpec(a.shape, lambda b: (0,) * a.ndim)
    row = lambda w: pl.BlockSpec((1, 1, w), lambda b: (b, 0, 0))
    st_spec = pl.BlockSpec((1, SSM_HEADS, SSM_HEAD_DIM, D_STATE), lambda b: (b, 0, 0, 0))
    cb = conv_b.reshape(1, CONV_DIM)
    nw = ssm_norm_w.reshape(1, SSM_WIDTH)
    return pl.pallas_call(
        _ssm_step_kernel,
        grid=(Bd,),
        in_specs=[row(CONV_DIM), pl.BlockSpec((1, CONV_WIDTH - 1, CONV_DIM), lambda b: (b, 0, 0)),
                  row(SMALL_W), row(SSM_WIDTH), st_spec,
                  full(conv_w), full(cb), full(dtb), full(alog), full(e), full(dx), full(nw)],
        out_specs=[row(SSM_WIDTH), st_spec],
        out_shape=[jax.ShapeDtypeStruct((Bd, 1, SSM_WIDTH), BF16),
                   jax.ShapeDtypeStruct(state_ssm.shape, F32)],
        compiler_params=_cparams("arbitrary"),
        name="ssm_step",
    )(xbc, state_conv, small, z, state_ssm, conv_w, cb, dtb, alog, e, dx, nw)


def _outproj_kernel(att_ref, ssm_ref, x_ref, g1_ref, sh2_ref, sc2_ref, n2_ref, wa_ref, wsm_ref, rw_ref, rb_ref,
                    cin_ref, h2_prev_ref, x1_ref, h2_ref, rt_ref, tg_ref, cout_ref, cnt_ref):
    del h2_prev_ref

    @pl.when((pl.program_id(0) == 0) & (pl.program_id(1) == 0))
    def _():
        cnt_ref[...] = cin_ref[...]

    y = (jnp.dot(att_ref[0], wa_ref[...], preferred_element_type=F32)
         + jnp.dot(ssm_ref[0], wsm_ref[...], preferred_element_type=F32))
    x1 = x_ref[0] + g1_ref[0] * y
    x1_ref[0] = x1
    h2 = _rmsnorm_rows(x1, n2_ref[...]) * (1.0 + sc2_ref[0]) + sh2_ref[0]
    h2_ref[...] = h2
    logits = _dot_split(h2, h2.astype(BF16), rw_ref) + rb_ref[...]
    tm = logits.shape[0]
    lane = lax.broadcasted_iota(I32, (tm, LANES), 1).astype(F32)
    cur = logits
    idxs = []
    val_tile = jnp.full((tm, LANES), NEG_BIG, F32)
    chosen = jnp.zeros((tm, LANES), F32)
    for k in range(TOP_K):
        m = jnp.max(cur, axis=1, keepdims=True)
        idx = jnp.min(jnp.where(cur == m, lane, float(LANES)), axis=1, keepdims=True)
        idxs.append(idx)
        val_tile = jnp.where(lane == float(k), m, val_tile)
        hit = lane == idx
        chosen = jnp.where(hit, 1.0, chosen)
        cur = jnp.where(hit, 2.0 * NEG_BIG, cur)
    top = jnp.max(val_tile, axis=1, keepdims=True)
    ex = jnp.exp2((val_tile - top) * LOG2E)
    tg_ref[0] = ex / jnp.sum(ex, axis=1, keepdims=True)

    before = (lax.broadcasted_iota(I32, (tm, tm), 1) < lax.broadcasted_iota(I32, (tm, tm), 0)).astype(BF16)
    rank = jnp.dot(before, chosen.astype(BF16), preferred_element_type=F32) + cnt_ref[0:1, :]
    cnt_ref[0:1, :] = cnt_ref[0:1, :] + jnp.sum(chosen, axis=0, keepdims=True)
    cout_ref[...] = cnt_ref[...]
    route = jnp.zeros((tm, LANES), F32)
    for k in range(TOP_K):
        rank_k = jnp.sum(jnp.where(lane == idxs[k], rank, 0.0), axis=1, keepdims=True)
        route = jnp.where(lane == float(k), idxs[k], route)
        route = jnp.where(lane == float(TOP_K + k), rank_k, route)
    rt_ref[0] = route.astype(I32)


def _out_proj(att, ssm, x, g1, sh2, sc2, norm2_w, wa, wsm, rw, rb, counts_in, h2_all, row0, tm):
    B, L, _ = x.shape
    steps = L // tm
    assert row0 % tm == 0
    per_row = g1.shape[1] != 1
    mod_spec = (pl.BlockSpec((1, tm, D_MODEL), lambda b, i: (b, i, 0)) if per_row
                else pl.BlockSpec((1, 1, D_MODEL), lambda b, i: (b, 0, 0)))
    full = lambda a: pl.BlockSpec(a.shape, lambda b, i: (0,) * a.ndim)
    row = lambda w: pl.BlockSpec((1, tm, w), lambda b, i: (b, i, 0))
    n2 = norm2_w.reshape(1, D_MODEL)
    cnt_spec = pl.BlockSpec((SUBLANES, LANES), lambda b, i: (0, 0))
    return pl.pallas_call(
        _outproj_kernel,
        grid=(B, L // tm),
        in_specs=[row(ATT_WIDTH), row(SSM_WIDTH), row(D_MODEL), mod_spec, mod_spec, mod_spec,
                  full(n2), full(wa), full(wsm), full(rw), full(rb), cnt_spec, pl.BlockSpec(memory_space=pl.ANY)],
        out_specs=[row(D_MODEL), pl.BlockSpec((tm, D_MODEL), lambda b, i: (row0 // tm + b * steps + i, 0)),
                   row(LANES), row(LANES), cnt_spec],
        out_shape=[jax.ShapeDtypeStruct((B, L, D_MODEL), F32), jax.ShapeDtypeStruct(h2_all.shape, F32),
                   jax.ShapeDtypeStruct((B, L, LANES), I32), jax.ShapeDtypeStruct((B, L, LANES), F32),
                   jax.ShapeDtypeStruct((SUBLANES, LANES), F32)],
        input_output_aliases={12: 1},
        scratch_shapes=[pltpu.VMEM((SUBLANES, LANES), F32)],
        compiler_params=_cparams("arbitrary", "arbitrary"),
        name="out_proj_route",
    )(att, ssm, x, g1, sh2, sc2, n2, wa, wsm, rw, rb, counts_in, h2_all)


def _moe_block_tables(counts, n_blocks):
    tb = MOE_ROWS
    nb = (counts + tb - 1) // tb
    cum = jnp.cumsum(nb)
    blk_start = cum - nb
    b = jnp.arange(n_blocks, dtype=I32)
    block_expert = jnp.minimum(jnp.sum((cum[None, :] <= b[:, None]).astype(I32), axis=1), N_EXPERTS - 1)
    rows_left = counts[block_expert] - (b - blk_start[block_expert]) * tb
    block_rows = jnp.where(b < cum[-1], jnp.clip(rows_left, 0, tb), 0).astype(I32)
    prev = jnp.concatenate([jnp.full((1,), -1, I32), block_expert[:-1]])
    block_first = ((block_expert != prev) & (block_rows > 0)).astype(I32)
    return (blk_start * tb).astype(I32), block_expert.astype(I32), block_first, block_rows


SC_WORKERS = 32
SC_SCATTER_WINDOW = 48
SC_WINDOW = 32


def _sc_scatter_rows(src, slots, n_rows):
    T, d = src.shape
    win = SC_SCATTER_WINDOW
    steps = T // win * TOP_K
    assert T % win == 0 and steps % SC_WORKERS == 0
    idx = slots.reshape(T // win, win, TOP_K).transpose(0, 2, 1).reshape(steps, win)
    idx_rows = jnp.pad(idx, ((0, 0), (0, LANES - win)))
    mesh = plsc.VectorSubcoreMesh(core_axis_name="c", subcore_axis_name="s")

    @functools.partial(pl.kernel, out_type=jax.ShapeDtypeStruct((n_rows, d), src.dtype), mesh=mesh, name="moe_scatter")
    def scatter(x_hbm, i_hbm, o_hbm):
        def body(x_vmem, i_vmem):
            pltpu.sync_copy(x_vmem, o_hbm.at[i_vmem.at[0, pl.ds(0, win)]])

        pltpu.emit_pipeline(body, grid=(steps,),
                            in_specs=[pl.BlockSpec((win, d), lambda i: (i // TOP_K, 0)),
                                      pl.BlockSpec((1, LANES), lambda i: (i, 0))],
                            out_specs=[],
                            core_axis_name=("c", "s"), dimension_semantics=(pltpu.PARALLEL,))(x_hbm, i_hbm)

    return scatter(src, idx_rows)


def _moe_kernel(be_ref, first_ref, rows_ref, x_ref, wgu_ref, bgu_ref, wd_ref, bd_ref, o_ref, wgu_s, wd_s):
    i = pl.program_id(0)

    @pl.when(first_ref[i] == 1)
    def _():
        wgu_s[...] = wgu_ref[0].astype(BF16)
        wd_s[...] = wd_ref[0].astype(BF16)

    @pl.when(rows_ref[i] > 0)
    def _():
        live = lax.broadcasted_iota(I32, x_ref.shape, 0) < rows_ref[i]
        x = jnp.where(live, x_ref[...], 0.0).astype(BF16)
        gu = jnp.dot(x, wgu_s[...], preferred_element_type=F32) + bgu_ref[0]
        g = jnp.minimum(gu[:, :D_FF], SWIGLU_LIMIT)
        u = jnp.clip(gu[:, D_FF:], -SWIGLU_LIMIT, SWIGLU_LIMIT)
        act = (u + 1.0) * (g * jax.nn.sigmoid(SWIGLU_ALPHA * g))
        o_ref[...] = jnp.dot(act.astype(BF16), wd_s[...], preferred_element_type=F32) + bd_ref[0]

    @pl.when(rows_ref[i] == 0)
    def _():
        o_ref[...] = jnp.zeros_like(o_ref)


def _moe_blocks(block_expert, block_first, block_rows, x_sorted, w_gate_up, b_gate_up, w_down, b_down):
    n_rows = x_sorted.shape[0]
    tb = MOE_ROWS
    grid_spec = pltpu.PrefetchScalarGridSpec(
        num_scalar_prefetch=3,
        grid=(n_rows // tb,),
        in_specs=[pl.BlockSpec((tb, D_MODEL), lambda i, be, bf, br: (i, 0)),
                  pl.BlockSpec((1, D_MODEL, 2 * D_FF), lambda i, be, bf, br: (be[i], 0, 0)),
                  pl.BlockSpec((1, 1, 2 * D_FF), lambda i, be, bf, br: (be[i], 0, 0)),
                  pl.BlockSpec((1, D_FF, D_MODEL), lambda i, be, bf, br: (be[i], 0, 0)),
                  pl.BlockSpec((1, 1, D_MODEL), lambda i, be, bf, br: (be[i], 0, 0))],
        out_specs=pl.BlockSpec((tb, D_MODEL), lambda i, be, bf, br: (i, 0)),
        scratch_shapes=[pltpu.VMEM((D_MODEL, 2 * D_FF), BF16), pltpu.VMEM((D_FF, D_MODEL), BF16)],
    )
    return pl.pallas_call(
        _moe_kernel,
        grid_spec=grid_spec,
        out_shape=jax.ShapeDtypeStruct((n_rows, D_MODEL), F32),
        compiler_params=_cparams("arbitrary"),
        name="moe_experts",
    )(block_expert, block_first, block_rows, x_sorted, w_gate_up,
      b_gate_up.reshape(N_EXPERTS, 1, 2 * D_FF), w_down, b_down.reshape(N_EXPERTS, 1, D_MODEL))


def _sc_gather_rows(table, idx):
    n = idx.shape[0]
    d = table.shape[1]
    win = SC_WINDOW
    idx_rows = jnp.pad(idx.reshape(n // win, win), ((0, 0), (0, LANES - win)))
    mesh = plsc.VectorSubcoreMesh(core_axis_name="c", subcore_axis_name="s")

    @functools.partial(pl.kernel, out_type=jax.ShapeDtypeStruct((n, d), table.dtype), mesh=mesh, name="moe_gather")
    def gather(t_hbm, i_hbm, o_hbm):
        def body(i_vmem, o_vmem):
            pltpu.sync_copy(t_hbm.at[i_vmem.at[0, pl.ds(0, win)]], o_vmem)

        pltpu.emit_pipeline(body, grid=(n // win,),
                            in_specs=[pl.BlockSpec((1, LANES), lambda i: (i, 0))],
                            out_specs=[pl.BlockSpec((win, d), lambda i: (i, 0))],
                            core_axis_name=("c", "s"), dimension_semantics=(pltpu.PARALLEL,))(i_hbm, o_hbm)

    return gather(table, idx_rows)


def _final_kernel(x1_ref, tg_ref, g2_ref, nw_ref, y0_ref, y1_ref, y2_ref, y3_ref, o_ref):
    gates = tg_ref[0]
    moe = gates[:, 0:1] * y0_ref[...]
    for k, y_ref in enumerate((y1_ref, y2_ref, y3_ref), start=1):
        moe = moe + gates[:, k:k + 1] * y_ref[...]
    o_ref[0] = _rmsnorm_rows(x1_ref[0] + g2_ref[0] * moe, nw_ref[...])


def _moe_final(x1, gates, g2, final_norm_w, y_rows, row0, tt):
    B, L, _ = x1.shape
    steps = L // tt
    blocks_per_k = B * steps
    base = row0 // tt
    assert row0 % tt == 0 and TOP_K == 4
    per_row = g2.shape[1] != 1
    mod_spec = (pl.BlockSpec((1, tt, D_MODEL), lambda b, i: (b, i, 0)) if per_row
                else pl.BlockSpec((1, 1, D_MODEL), lambda b, i: (b, 0, 0)))
    y_specs = [pl.BlockSpec((tt, D_MODEL), functools.partial(lambda b, i, k: (base + k * blocks_per_k + b * steps + i, 0), k=k))
               for k in range(TOP_K)]
    nw = final_norm_w.reshape(1, D_MODEL)
    return pl.pallas_call(
        _final_kernel,
        grid=(B, steps),
        in_specs=[pl.BlockSpec((1, tt, D_MODEL), lambda b, i: (b, i, 0)),
                  pl.BlockSpec((1, tt, LANES), lambda b, i: (b, i, 0)),
                  mod_spec, pl.BlockSpec((1, D_MODEL), lambda b, i: (0, 0))] + y_specs,
        out_specs=pl.BlockSpec((1, tt, D_MODEL), lambda b, i: (b, i, 0)),
        out_shape=jax.ShapeDtypeStruct((B, L, D_MODEL), F32),
        compiler_params=_cparams("arbitrary", "arbitrary"),
        name="moe_final",
    )(x1, gates, g2, nw, y_rows, y_rows, y_rows, y_rows)


def kernel(x_prompt, x_sample, c_prompt, c_sample, cache_k, cache_v, cache_lf, state_conv, state_ssm, page_table,
           ada_w, ada_b, norm1_w, w_in, b_f, conv_w, conv_b, dt_bias, A_log, D_skip, ssm_norm_w, w_out,
           norm2_w, router_w, router_b, w_gate_up, b_gate_up, w_down, b_down, final_norm_w):
    assert ada_w.shape[0] == 1, "single-layer trunk"
    B, L, D = x_prompt.shape
    Bd = x_sample.shape[0]
    assert x_sample.shape[1] == 1 and L % SSD_CHUNK == 0

    n_c = B + Bd
    rows = -(-n_c // SUBLANES) * SUBLANES
    c_all = jnp.concatenate([c_prompt, c_sample, jnp.zeros((rows - n_c, D), F32)], axis=0)
    mod = _modulation(c_all, ada_w[0], ada_b[0])
    mod_p = [m.reshape(B, 1, D) for m in jnp.split(mod[:B], 6, axis=-1)]
    mod_s = [m.reshape(1, Bd, D) for m in jnp.split(mod[B:n_c], 6, axis=-1)]

    w = w_in[0]
    o_f = 3 * ATT_WIDTH
    o_z = o_f + ATT_HEADS
    o_x = o_z + SSM_WIDTH
    o_dt = o_x + CONV_DIM
    w_small = jnp.concatenate([w[:, o_f:o_z], w[:, o_dt:o_dt + SSM_HEADS],
                               jnp.zeros((D, SMALL_W - ATT_HEADS - SSM_HEADS), F32)], axis=1)
    wts = (w[:, :ATT_WIDTH].astype(BF16), w[:, ATT_WIDTH:2 * ATT_WIDTH].astype(BF16),
           w[:, 2 * ATT_WIDTH:o_f].astype(BF16), w[:, o_z:o_x].astype(BF16), w[:, o_x:o_dt].astype(BF16), _split_weight(w_small))
    wa = w_out[0][:ATT_WIDTH].astype(BF16)
    wsm = w_out[0][ATT_WIDTH:].astype(BF16)
    rw = _split_weight(jnp.pad(router_w[0], ((0, 0), (0, LANES - N_EXPERTS))))
    rb = jnp.pad(router_b[0].reshape(1, N_EXPERTS), ((0, 0), (0, LANES - N_EXPERTS)), constant_values=NEG_BIG)
    dtb, alog, dx = _ssm_params(dt_bias[0], A_log[0], D_skip[0])
    e = jnp.asarray(_head_expander(), BF16)

    tm_p = min(512, L)
    qp, kp, vp, k_p, v_p, z_p, xbc_p, small_p, lf_p = _in_proj(x_prompt, mod_p[0], mod_p[1], norm1_w[0], wts,
                                                              b_f[0], tm_p)
    att_p = _flash_attention(qp, kp, vp, min(1024, L), min(512, L))
    ssm_p, st_p = _ssd_prompt(xbc_p, small_p, z_p, conv_w[0], conv_b[0], dtb, alog, dx, e, ssm_norm_w[0])
    n_tok = B * L + Bd
    zero_counts = jnp.zeros((SUBLANES, LANES), F32)
    h2_all = jnp.zeros((n_tok, D), F32)
    x1_p, h2_all, rt_p, tg_p, counts_p = _out_proj(att_p, ssm_p, x_prompt, mod_p[2], mod_p[3], mod_p[4], norm2_w[0],
                                                   wa, wsm, rw, rb, zero_counts, h2_all, 0, min(512, L))

    xs_rows = x_sample.reshape(1, Bd, D)
    qp_s, _, _, k_s, v_s, z_s, xbc_s, small_s, _ = _in_proj(xs_rows, mod_s[0], mod_s[1], norm1_w[0], wts, b_f[0], Bd)
    per_row = lambda a: a.reshape(Bd, 1, a.shape[-1])
    head_cols = lambda a: a.reshape(Bd, ATT_HEADS, HEAD_DIM).transpose(0, 2, 1).astype(F32)
    q_halves = qp_s.astype(F32).reshape(Bd, ATT_HEADS, 2, HEAD_DIM)
    odd_head = (jnp.arange(ATT_HEADS) % 2 == 1)[None, :, None]
    q_s = jnp.where(odd_head, q_halves[:, :, 1, :], q_halves[:, :, 0, :])
    att_s4, lf_s = _decode_attention(
        page_table, head_cols(q_s), head_cols(k_s), head_cols(v_s), per_row(small_s), b_f[0],
        cache_k[0].transpose(0, 2, 3, 1), cache_v[0].transpose(0, 2, 3, 1), cache_lf[0].transpose(0, 2, 1))
    att_s = att_s4.reshape(1, Bd, ATT_WIDTH).astype(BF16)
    ssm_s, st_s = _ssm_step(per_row(xbc_s), state_conv[0], per_row(small_s), per_row(z_s), state_ssm[0],
                            conv_w[0], conv_b[0], dtb, alog, dx, e, ssm_norm_w[0])
    x1_s, h2_all, rt_s, tg_s, counts = _out_proj(att_s, ssm_s.reshape(1, Bd, SSM_WIDTH), xs_rows, mod_s[2], mod_s[3],
                                                 mod_s[4], norm2_w[0], wa, wsm, rw, rb, counts_p, h2_all, B * L, Bd)

    n_blocks = (n_tok * TOP_K + N_EXPERTS * (MOE_ROWS - 1)) // MOE_ROWS
    pad_start, b_exp, b_first, b_rows = _moe_block_tables(counts[0, :N_EXPERTS].astype(I32), n_blocks)

    def token_slots(rt):
        e_idx = rt[..., :TOP_K].reshape(-1, TOP_K)
        first = jnp.sum(jnp.where(e_idx[..., None] == jnp.arange(N_EXPERTS, dtype=I32), pad_start, 0), axis=-1)
        return first + rt[..., TOP_K:2 * TOP_K].reshape(-1, TOP_K)

    slots_p, slots_s = token_slots(rt_p), token_slots(rt_s)
    x_sorted = _sc_scatter_rows(h2_all, jnp.concatenate([slots_p, slots_s], axis=0), n_blocks * MOE_ROWS)
    y_sorted = _moe_blocks(b_exp, b_first, b_rows, x_sorted, w_gate_up[0], b_gate_up[0], w_down[0], b_down[0])

    n_assign = n_tok * TOP_K
    chunk = SC_WINDOW * SC_WORKERS
    n_idx = -(-n_assign // chunk) * chunk
    slots = jnp.concatenate([slots_p.T.reshape(-1), slots_s.T.reshape(-1), jnp.zeros((n_idx - n_assign,), I32)])
    y_rows = _sc_gather_rows(y_sorted, slots)
    y_prompt = _moe_final(x1_p, tg_p, mod_p[5], final_norm_w, y_rows, 0, min(512, L))
    y_sample = _moe_final(x1_s, tg_s, mod_s[5], final_norm_w, y_rows, B * L * TOP_K, Bd).reshape(Bd, 1, D)

    conv_s = jnp.concatenate([state_conv[0][:, 1:], xbc_s.reshape(Bd, 1, CONV_DIM)], axis=1)
    ssm_state_p = st_p.reshape(B, D_STATE, SSM_HEADS, SSM_HEAD_DIM).transpose(0, 2, 3, 1)
    return (y_prompt, y_sample,
            k_p.reshape(1, B, L, ATT_HEADS, HEAD_DIM), v_p.reshape(1, B, L, ATT_HEADS, HEAD_DIM),
            lf_p.reshape(1, B, L, ATT_HEADS), xbc_p[:, L - (CONV_WIDTH - 1):].reshape(1, B, CONV_WIDTH - 1, CONV_DIM),
            ssm_state_p.reshape(1, B, SSM_HEADS, SSM_HEAD_DIM, D_STATE),
            k_s.reshape(1, Bd, 1, ATT_HEADS, HEAD_DIM), v_s.reshape(1, Bd, 1, ATT_HEADS, HEAD_DIM),
            lf_s.reshape(1, Bd, 1, ATT_HEADS), conv_s.reshape(1, Bd, CONV_WIDTH - 1, CONV_DIM),
            st_s.reshape(1, Bd, SSM_HEADS, SSM_HEAD_DIM, D_STATE))
```

```python
import functools
import math

import numpy as np
import jax
import jax.numpy as jnp
from jax import lax
from jax.experimental import pallas as pl
from jax.experimental.pallas import tpu as pltpu
from jax.experimental.pallas import tpu_sc as plsc

F32 = jnp.float32
BF16 = jnp.bfloat16
I32 = jnp.int32
HIGHEST = lax.Precision.HIGHEST

D_MODEL = 1024
ATT_HEADS = 16
HEAD_DIM = 64
ATT_WIDTH = ATT_HEADS * HEAD_DIM
SSM_HEADS = 16
SSM_HEAD_DIM = 64
SSM_WIDTH = SSM_HEADS * SSM_HEAD_DIM
SSM_GROUPS = 2
D_STATE = 128
CONV_WIDTH = 4
CONV_DIM = SSM_WIDTH + 2 * SSM_GROUPS * D_STATE
SSD_CHUNK = 128
N_EXPERTS = 32
TOP_K = 4
D_FF = D_MODEL
SWIGLU_LIMIT = 7.0
SWIGLU_ALPHA = 1.702
NORM_EPS = 1e-5

LANES = 128
SUBLANES = 8
SMALL_W = LANES
DT_COL = ATT_HEADS
NEG_BIG = -1e30
LOG2E = math.log2(math.e)
VMEM_LIMIT = 48 * 1024 * 1024
INPROJ_VMEM_LIMIT = 58 * 1024 * 1024
MOE_ROWS = 512


def _cparams(*sem):
    return pltpu.CompilerParams(dimension_semantics=sem, vmem_limit_bytes=VMEM_LIMIT)


def _silu(x):
    return x * jax.nn.sigmoid(x)


def _softplus(x):
    return jnp.maximum(x, 0.0) + jnp.log(1.0 + jnp.exp(-jnp.abs(x)))


def _log_sigmoid(x):
    return -_softplus(-x)


def _rmsnorm_rows(x, w):
    var = jnp.mean(x * x, axis=-1, keepdims=True)
    return x * lax.rsqrt(var + NORM_EPS) * w


def _split3_bf16(x):
    hi = x.astype(BF16)
    r = x - hi.astype(F32)
    mid = r.astype(BF16)
    lo = (r - mid.astype(F32)).astype(BF16)
    return hi, mid, lo


def _pack_bf16_pairs(x):
    half = x.shape[1] // 2
    lo = pltpu.bitcast(x[:, :half].astype(BF16).astype(F32), jnp.uint32)
    hi = pltpu.bitcast(x[:, half:].astype(BF16).astype(F32), jnp.uint32)
    return hi | lax.shift_right_logical(lo, jnp.uint32(16))


def _unpack_bf16_pairs(w):
    lo = pltpu.bitcast(lax.shift_left(w, jnp.uint32(16)), F32)
    hi = pltpu.bitcast(w & jnp.uint32(0xFFFF0000), F32)
    return jnp.concatenate([lo, hi], axis=1)


def _split_weight(w):
    hi = w.astype(BF16)
    lo = (w - hi.astype(F32)).astype(BF16)
    return jnp.concatenate([hi, lo], axis=1)


def _dot_split(x, x_hi, w_ref):
    x_lo = (x - x_hi.astype(F32)).astype(BF16)
    both = jnp.dot(x_hi, w_ref[...], preferred_element_type=F32)
    return (both[:, :LANES] + both[:, LANES:]
            + jnp.dot(x_lo, w_ref[:, :LANES], preferred_element_type=F32))


def _mod_kernel(c_ref, w_ref, b_ref, o_ref):
    s = _silu(c_ref[...]).astype(BF16)
    o_ref[...] = jnp.dot(s, w_ref[...].astype(BF16), preferred_element_type=F32) + b_ref[...]


def _modulation(c_all, ada_w, ada_b):
    rows = c_all.shape[0]
    n_out = ada_w.shape[1]
    tn = D_MODEL
    return pl.pallas_call(
        _mod_kernel,
        grid=(n_out // tn,),
        in_specs=[pl.BlockSpec((rows, D_MODEL), lambda j: (0, 0)),
                  pl.BlockSpec((D_MODEL, tn), lambda j: (0, j)),
                  pl.BlockSpec((1, tn), lambda j: (0, j))],
        out_specs=pl.BlockSpec((rows, tn), lambda j: (0, j)),
        out_shape=jax.ShapeDtypeStruct((rows, n_out), F32),
        compiler_params=_cparams("arbitrary"),
        name="adaln_mod",
    )(c_all, ada_w, ada_b.reshape(1, n_out))


def _free_half(h):
    return h * LANES + (HEAD_DIM if h % 2 == 0 else 0)


def _bias_layout():
    wide = ATT_HEADS * LANES
    sel = np.zeros((SMALL_W, wide), np.float32)
    rows = np.zeros((5, wide), np.float32)
    for h in range(ATT_HEADS):
        base = _free_half(h)
        rows[4, base] = 1.0
        for part in range(3):
            sel[part * ATT_HEADS + h, base + part] = 1.0
            sel[part * ATT_HEADS + h, base + 3 + part] = -1.0
            rows[0, base + part] = 1.0
            rows[3, base + part] = 1.0
            rows[1, base + 3 + part] = 1.0
            rows[2, base + 3 + part] = 1.0
    return sel, rows


def _inproj_kernel(x_ref, sh_ref, sc_ref, nw_ref, wq_ref, wk_ref, wv_ref, wz_ref, wx_ref, ws_ref, bf_ref,
                   sel_ref, rows_ref, qp_ref, kp_ref, vp_ref, k_ref, v_ref, z_ref, xbc_ref, sm_ref, lf_ref,
                   carry_ref):
    tm = x_ref.shape[1]

    @pl.when(pl.program_id(1) == 0)
    def _():
        carry_ref[...] = jnp.zeros_like(carry_ref)

    h = _rmsnorm_rows(x_ref[0], nw_ref[...]) * (1.0 + sc_ref[0]) + sh_ref[0]
    hb = h.astype(BF16)
    qb = (jnp.dot(hb, wq_ref[...], preferred_element_type=F32) * (HEAD_DIM ** -0.5 * LOG2E)).astype(BF16)
    k = jnp.dot(hb, wk_ref[...], preferred_element_type=F32)
    k_ref[0] = k
    kb = k.astype(BF16)
    v = jnp.dot(hb, wv_ref[...], preferred_element_type=F32)
    v_ref[0] = v
    vb = v.astype(BF16)
    z_ref[0] = jnp.dot(hb, wz_ref[...], preferred_element_type=F32).astype(BF16)
    xbc_ref[0] = jnp.dot(hb, wx_ref[...], preferred_element_type=F32)
    sm = _dot_split(h, hb, ws_ref)
    sm_ref[0] = sm

    lf = _log_sigmoid(sm + bf_ref[...])
    lf_ref[0] = lf[:, :ATT_HEADS]
    tri = (lax.broadcasted_iota(I32, (tm, tm), 1) <= lax.broadcasted_iota(I32, (tm, tm), 0)).astype(BF16)
    sums = jnp.dot(tri, jnp.concatenate(_split3_bf16(lf), axis=1), preferred_element_type=F32)
    fcum = sums[:, :LANES] + sums[:, LANES:2 * LANES] + sums[:, 2 * LANES:] + carry_ref[0:1, :]
    carry_ref[0:1, :] = fcum[tm - 1:tm, :]
    hi, mid, lo = (part.astype(F32) for part in _split3_bf16(fcum * LOG2E))
    lane = lax.broadcasted_iota(I32, (tm, LANES), 1)
    packed = jnp.where(lane < ATT_HEADS, hi,
                       jnp.where(lane < 2 * ATT_HEADS, pltpu.roll(mid, ATT_HEADS, 1),
                                 jnp.where(lane < 3 * ATT_HEADS, pltpu.roll(lo, 2 * ATT_HEADS, 1), 0.0)))
    spread = jnp.dot(packed.astype(BF16), sel_ref[...], preferred_element_type=F32)
    low = lane < HEAD_DIM
    for pair in range(ATT_HEADS // 2):
        ps = slice(pair * LANES, (pair + 1) * LANES)
        for hh in range(2):
            hd = 2 * pair + hh
            keep = low if hh == 0 else jnp.logical_not(low)
            sl = slice(hd * LANES, (hd + 1) * LANES)
            part = spread[:, sl]
            aug_q = (part * rows_ref[0:1, sl] + rows_ref[2:3, sl]).astype(BF16)
            aug_k = (part * rows_ref[1:2, sl] + rows_ref[3:4, sl]).astype(BF16)
            ones_lane = jnp.broadcast_to(rows_ref[4:5, sl], (tm, LANES)).astype(BF16)
            qp_ref[0, :, sl] = jnp.where(keep, qb[:, ps], aug_q)
            kp_ref[0, :, sl] = jnp.where(keep, kb[:, ps], aug_k)
            vp_ref[0, :, sl] = jnp.where(keep, vb[:, ps], ones_lane)


def _in_proj(x, sh, sc, norm_w, wts, b_f, tm):
    B, L, _ = x.shape
    per_row = sh.shape[1] != 1
    mod_spec = (pl.BlockSpec((1, tm, D_MODEL), lambda b, i: (b, i, 0)) if per_row
                else pl.BlockSpec((1, 1, D_MODEL), lambda b, i: (b, 0, 0)))
    wq, wk, wv, wz, wx, ws = wts
    sel, rows = _bias_layout()
    sel, rows = jnp.asarray(sel, BF16), jnp.asarray(rows)
    bf2 = jnp.pad(b_f.reshape(1, ATT_HEADS), ((0, 0), (0, SMALL_W - ATT_HEADS)))
    wide = ATT_HEADS * LANES

    def wspec(w):
        return pl.BlockSpec(w.shape, lambda b, i: (0, 0), pipeline_mode=pl.Buffered(1))

    def ospec(width):
        return pl.BlockSpec((1, tm, width), lambda b, i: (b, i, 0))

    def oshape(width, dt):
        return jax.ShapeDtypeStruct((B, L, width), dt)

    return pl.pallas_call(
        _inproj_kernel,
        grid=(B, L // tm),
        in_specs=[pl.BlockSpec((1, tm, D_MODEL), lambda b, i: (b, i, 0)), mod_spec, mod_spec,
                  pl.BlockSpec((1, D_MODEL), lambda b, i: (0, 0)),
                  wspec(wq), wspec(wk), wspec(wv), wspec(wz), wspec(wx), wspec(ws),
                  wspec(bf2), wspec(sel), wspec(rows)],
        out_specs=[ospec(wide), ospec(wide), ospec(wide), ospec(ATT_WIDTH), ospec(ATT_WIDTH),
                   ospec(SSM_WIDTH), ospec(CONV_DIM), ospec(SMALL_W), ospec(ATT_HEADS)],
        out_shape=[oshape(wide, BF16), oshape(wide, BF16), oshape(wide, BF16),
                   oshape(ATT_WIDTH, F32), oshape(ATT_WIDTH, F32),
                   oshape(SSM_WIDTH, BF16), oshape(CONV_DIM, F32), oshape(SMALL_W, F32), oshape(ATT_HEADS, F32)],
        scratch_shapes=[pltpu.VMEM((SUBLANES, LANES), F32)],
        compiler_params=pltpu.CompilerParams(dimension_semantics=("arbitrary", "arbitrary"),
                                             vmem_limit_bytes=INPROJ_VMEM_LIMIT),
        name="in_proj",
    )(x, sh, sc, norm_w.reshape(1, D_MODEL), wq, wk, wv, wz, wx, ws, bf2, sel, rows)


FLASH_HEADS = 8


def _flash_kernel(qi_ref, ki_ref, qp_ref, kp_ref, vp_ref, o_ref, m_ref, acc_ref, *, tq, tk):
    t = pl.program_id(2)
    qi = qi_ref[t]
    ki = ki_ref[t]
    last = ((qi + 1) * tq - 1) // tk

    @pl.when(ki == 0)
    def _():
        m_ref[...] = jnp.full_like(m_ref, NEG_BIG)
        acc_ref[...] = jnp.zeros_like(acc_ref)

    def step(masked):
        if masked:
            qpos = qi * tq + lax.broadcasted_iota(I32, (tq, tk), 0)
            kpos = ki * tk + lax.broadcasted_iota(I32, (tq, tk), 1)
            visible = kpos <= qpos
        for hh in range(FLASH_HEADS):
            q = qp_ref[0, :, hh * LANES:(hh + 1) * LANES]
            k = kp_ref[0, :, hh * LANES:(hh + 1) * LANES]
            s = lax.dot_general(q, k, (((1,), (1,)), ((), ())), preferred_element_type=F32)
            if masked:
                s = jnp.where(visible, s, NEG_BIG)
            m_prev = m_ref[hh]
            m_new = jnp.maximum(m_prev, jnp.max(s, axis=1, keepdims=True))
            p = jnp.exp2(s - jnp.concatenate([m_new] * (tk // LANES), axis=1))
            acc_ref[hh] = (jnp.exp2(m_prev - m_new) * acc_ref[hh]
                           + jnp.dot(p.astype(BF16), vp_ref[0, :, hh * LANES:(hh + 1) * LANES],
                                     preferred_element_type=F32))
            m_ref[hh] = m_new

    crosses = (ki + 1) * tk - 1 > qi * tq

    @pl.when(crosses)
    def _():
        step(True)

    @pl.when(jnp.logical_not(crosses))
    def _():
        step(False)

    @pl.when(ki == last)
    def _():
        lane = lax.broadcasted_iota(I32, (tq, LANES), 1)
        for pr in range(FLASH_HEADS // 2):
            a0 = acc_ref[2 * pr]
            a1 = acc_ref[2 * pr + 1]
            o0 = a0 / a0[:, HEAD_DIM:HEAD_DIM + 1]
            o1 = a1 / a1[:, 0:1]
            o_ref[0, :, pr * LANES:(pr + 1) * LANES] = jnp.where(lane < HEAD_DIM, o0, o1).astype(o_ref.dtype)


def _flash_attention(qp, kp, vp, tq, tk):
    B, L, _ = qp.shape
    pairs = ATT_HEADS // FLASH_HEADS
    hw = FLASH_HEADS * LANES
    qs, ks = [], []
    for qi in range(L // tq):
        for ki in range(((qi + 1) * tq - 1) // tk + 1):
            qs.append(qi)
            ks.append(ki)
    qi_tab = jnp.asarray(np.array(qs, np.int32))
    ki_tab = jnp.asarray(np.array(ks, np.int32))
    grid_spec = pltpu.PrefetchScalarGridSpec(
        num_scalar_prefetch=2,
        grid=(B, pairs, len(qs)),
        in_specs=[pl.BlockSpec((1, tq, hw), lambda b, p, t, qt, kt: (b, qt[t], p)),
                  pl.BlockSpec((1, tk, hw), lambda b, p, t, qt, kt: (b, kt[t], p)),
                  pl.BlockSpec((1, tk, hw), lambda b, p, t, qt, kt: (b, kt[t], p))],
        out_specs=pl.BlockSpec((1, tq, hw // 2), lambda b, p, t, qt, kt: (b, qt[t], p)),
        scratch_shapes=[pltpu.VMEM((FLASH_HEADS, tq, LANES), F32), pltpu.VMEM((FLASH_HEADS, tq, LANES), F32)],
    )
    return pl.pallas_call(
        functools.partial(_flash_kernel, tq=tq, tk=tk),
        grid_spec=grid_spec,
        out_shape=jax.ShapeDtypeStruct((B, L, ATT_WIDTH), BF16),
        compiler_params=_cparams("arbitrary", "arbitrary", "arbitrary"),
        name="fox_flash",
    )(qi_tab, ki_tab, qp, kp, vp)


def _head_expander():
    e = np.zeros((SMALL_W, SSM_WIDTH), np.float32)
    for h in range(SSM_HEADS):
        e[DT_COL + h, h * SSM_HEAD_DIM:(h + 1) * SSM_HEAD_DIM] = 1.0
    return e


def _expand_heads(vals, e_bf16):
    hi = vals.astype(BF16)
    lo = (vals - hi.astype(F32)).astype(BF16)
    return (jnp.dot(hi, e_bf16, preferred_element_type=F32)
            + jnp.dot(lo, e_bf16, preferred_element_type=F32))


def _conv_silu_rows(rows, cw_ref, cb_ref):
    acc = cb_ref[...] + cw_ref[CONV_WIDTH - 1:CONV_WIDTH, :] * rows[0]
    for j in range(1, CONV_WIDTH):
        acc = acc + cw_ref[CONV_WIDTH - 1 - j:CONV_WIDTH - j, :] * rows[j]
    return _silu(acc)


def _ssd_kernel(xbc_ref, sm_ref, z_ref, cw_ref, cb_ref, dtb_ref, alog_ref, e_ref, dx_ref, nw_ref,
                y_ref, st_ref, buf_ref, ht_ref):
    c = pl.program_id(1)
    nc = pl.num_programs(1)
    Q = SSD_CHUNK
    halo = SUBLANES

    @pl.when(c == 0)
    def _():
        buf_ref[0:halo, :] = jnp.zeros((halo, CONV_DIM), F32)
        ht_ref[...] = jnp.zeros_like(ht_ref)

    @pl.when(c > 0)
    def _():
        buf_ref[0:halo, :] = buf_ref[Q:Q + halo, :]

    buf_ref[halo:halo + Q, :] = xbc_ref[0]
    xc = _conv_silu_rows([buf_ref[halo - j:halo - j + Q, :] for j in range(CONV_WIDTH)], cw_ref, cb_ref)
    xs = xc[:, :SSM_WIDTH]
    e = e_ref[...]

    dt = _softplus(sm_ref[0] + dtb_ref[...])
    a = dt * (-jnp.exp(alog_ref[...]))
    row = lax.broadcasted_iota(I32, (Q, Q), 0)
    col = lax.broadcasted_iota(I32, (Q, Q), 1)
    causal = col <= row
    acum = jnp.dot(causal.astype(F32), a, precision=HIGHEST, preferred_element_type=F32)
    acum_t = acum.T
    dt_x = _expand_heads(dt, e)
    acum_x = _expand_heads(acum, e)
    last_x = acum_x[Q - 1:Q, :]
    xdt = xs * dt_x
    xdt_b = xdt.astype(BF16)
    x_end = (xdt * jnp.exp(last_x - acum_x)).astype(BF16)
    grow = jnp.exp(acum_x)
    cdecay = jnp.exp(last_x)

    lane = lax.broadcasted_iota(I32, (Q, LANES), 1)
    low = lane < SSM_HEAD_DIM
    hpg = SSM_HEADS // SSM_GROUPS
    gw = hpg * SSM_HEAD_DIM
    y_parts = []
    for g in range(SSM_GROUPS):
        bg = xc[:, SSM_WIDTH + g * D_STATE:SSM_WIDTH + (g + 1) * D_STATE].astype(BF16)
        cg = xc[:, SSM_WIDTH + (SSM_GROUPS + g) * D_STATE:SSM_WIDTH + (SSM_GROUPS + g + 1) * D_STATE].astype(BF16)
        scores = lax.dot_general(cg, bg, (((1,), (1,)), ((), ())), preferred_element_type=F32)
        gs = slice(g * gw, (g + 1) * gw)
        h_prev = ht_ref[:, gs]
        y_off = jnp.dot(cg, h_prev.astype(BF16), preferred_element_type=F32) * grow[:, gs]
        ht_ref[:, gs] = h_prev * cdecay[:, gs] + lax.dot_general(
            bg, x_end[:, gs], (((0,), (0,)), ((), ())), preferred_element_type=F32)
        for pr in range(hpg // 2):
            pair_lo = g * gw + pr * LANES
            xpair = xdt_b[:, pair_lo:pair_lo + LANES]
            halves = []
            for hh in range(2):
                h = g * hpg + 2 * pr + hh
                decay = jnp.where(causal, jnp.exp(acum[:, DT_COL + h:DT_COL + h + 1]
                                                  - acum_t[DT_COL + h:DT_COL + h + 1, :]), 0.0)
                halves.append(jnp.dot((scores * decay).astype(BF16), xpair, preferred_element_type=F32))
            y_parts.append(jnp.where(low, halves[0], halves[1]) + y_off[:, pr * LANES:(pr + 1) * LANES])
    y = jnp.concatenate(y_parts, axis=1) + dx_ref[...] * xs
    gated = y * _silu(z_ref[0].astype(F32))
    y_ref[0] = _rmsnorm_rows(gated, nw_ref[...]).astype(y_ref.dtype)

    @pl.when(c == nc - 1)
    def _():
        st_ref[0] = ht_ref[...]


def _ssm_params(dt_bias, A_log, D_skip):
    pad = (DT_COL, SMALL_W - DT_COL - SSM_HEADS)
    dtb = jnp.pad(dt_bias.astype(F32), pad).reshape(1, SMALL_W)
    alog = jnp.pad(A_log.astype(F32), pad).reshape(1, SMALL_W)
    dx = jnp.repeat(D_skip.astype(F32), SSM_HEAD_DIM).reshape(1, SSM_WIDTH)
    return dtb, alog, dx


def _ssd_prompt(xbc, small, z, conv_w, conv_b, dtb, alog, dx, e, ssm_norm_w):
    B, L, _ = xbc.shape
    Q = SSD_CHUNK
    full = lambda a: pl.BlockSpec(a.shape, lambda b, c: (0,) * a.ndim)
    row_spec = lambda w: pl.BlockSpec((1, Q, w), lambda b, c: (b, c, 0))
    cb = conv_b.reshape(1, CONV_DIM)
    nw = ssm_norm_w.reshape(1, SSM_WIDTH)
    return pl.pallas_call(
        _ssd_kernel,
        grid=(B, L // Q),
        in_specs=[row_spec(CONV_DIM), row_spec(SMALL_W), row_spec(SSM_WIDTH),
                  full(conv_w), full(cb), full(dtb), full(alog), full(e), full(dx), full(nw)],
        out_specs=[row_spec(SSM_WIDTH), pl.BlockSpec((1, D_STATE, SSM_WIDTH), lambda b, c: (b, 0, 0))],
        out_shape=[jax.ShapeDtypeStruct((B, L, SSM_WIDTH), BF16),
                   jax.ShapeDtypeStruct((B, D_STATE, SSM_WIDTH), F32)],
        scratch_shapes=[pltpu.VMEM((Q + SUBLANES, CONV_DIM), F32), pltpu.VMEM((D_STATE, SSM_WIDTH), F32)],
        compiler_params=_cparams("arbitrary", "arbitrary"),
        name="ssd_prompt",
    )(xbc, small, z, conv_w, cb, dtb, alog, e, dx, nw)


DECODE_PAGES = 8


def _decode_attn_kernel(pt_ref, qt_ref, knt_ref, vnt_ref, sm_ref, bf_ref, *refs, pps):
    k_refs = refs[0:pps]
    v_refs = refs[pps:2 * pps]
    lf_refs = refs[2 * pps:3 * pps]
    o_ref, lfo_ref, qrep_ref, m_ref, l_ref, acc_ref, carry_ref, bias_ref = refs[3 * pps:]
    blk = pl.program_id(1)
    nblk = pl.num_programs(1)
    H = ATT_HEADS
    page = k_refs[0].shape[3]
    lane_row = lax.broadcasted_iota(I32, (1, page), 1)

    @pl.when(blk == 0)
    def _():
        lf_new = _log_sigmoid(sm_ref[0] + bf_ref[...])
        lfo_ref[0] = lf_new[:, :H]
        diag = (lax.broadcasted_iota(I32, (H, SMALL_W), 0) == lax.broadcasted_iota(I32, (H, SMALL_W), 1))
        lf_col = jnp.sum(jnp.where(diag, jnp.broadcast_to(lf_new, (H, SMALL_W)), 0.0), axis=1, keepdims=True)
        carry_ref[...] = jnp.broadcast_to(lf_col, (H, page))
        qt = qt_ref[0]
        knt = knt_ref[0].astype(BF16).astype(F32)
        vnt = vnt_ref[0].astype(BF16).astype(F32)
        s_row = jnp.sum(qt * knt, axis=0, keepdims=True)
        lane = lax.broadcasted_iota(I32, (HEAD_DIM, page), 1)
        for h in range(H):
            qrep_ref[h] = jnp.broadcast_to(qt[:, h:h + 1], (HEAD_DIM, page))
            m_ref[h:h + 1, :] = jnp.where(lane_row == 0, jnp.broadcast_to(s_row[:, h:h + 1], (1, page)), NEG_BIG)
            acc_ref[h] = jnp.where(lane == 0, jnp.broadcast_to(vnt[:, h:h + 1], (HEAD_DIM, page)), 0.0)
        l_ref[...] = jnp.broadcast_to(jnp.where(lane_row == 0, 1.0, 0.0), (H, page))

    later = (lax.broadcasted_iota(I32, (page, page), 0) > lax.broadcasted_iota(I32, (page, page), 1)).astype(F32)
    carry = carry_ref[...]
    lf_all = jnp.concatenate([lf_refs[j][0] for j in range(pps)], axis=0)
    suffix = jnp.dot(lf_all, later, precision=HIGHEST, preferred_element_type=F32)
    for j in range(pps):
        bias_ref[j] = (suffix[j * H:(j + 1) * H, :] + carry) * LOG2E
        carry = carry + jnp.sum(lf_refs[j][0], axis=1, keepdims=True)
    carry_ref[...] = carry

    def head_body(h, _):
        q3 = qrep_ref[h]
        row = pl.ds(h, 1)
        m = m_ref[row, :]
        l = l_ref[row, :]
        acc = acc_ref[h]
        for j in range(pps):
            s = jnp.sum(q3 * k_refs[j][0, h], axis=0, keepdims=True) + bias_ref[j, row, :]
            m_new = jnp.maximum(m, s)
            alpha = jnp.exp2(m - m_new)
            p = jnp.exp2(s - m_new)
            l = alpha * l + p
            acc = alpha * acc + p * v_refs[j][0, h]
            m = m_new
        m_ref[row, :] = m
        l_ref[row, :] = l
        acc_ref[h] = acc
        return 0

    lax.fori_loop(0, H, head_body, 0)

    @pl.when(blk == nblk - 1)
    def _():
        m_all = m_ref[...]
        w = jnp.exp2(m_all - jnp.max(m_all, axis=1, keepdims=True))
        den = jnp.sum(l_ref[...] * w, axis=1, keepdims=True)
        for h in range(H):
            num = jnp.sum(acc_ref[h] * w[h:h + 1, :], axis=1, keepdims=True)
            o_ref[0, h] = num / den[h:h + 1, :]


def _decode_attention(page_table, q_t, kn_t, vn_t, small, b_f, cache_k_t, cache_v_t, cache_lf_t):
    Bd = q_t.shape[0]
    n_pages = page_table.shape[1]
    page = cache_k_t.shape[3]
    pps = math.gcd(DECODE_PAGES, n_pages)
    pt_flat = page_table.reshape(-1)
    bf2 = jnp.pad(b_f.reshape(1, ATT_HEADS), ((0, 0), (0, SMALL_W - ATT_HEADS)))

    def page_spec(shape, j):
        def imap(b, blk, pt):
            return (pt[b * n_pages + (n_pages - 1 - (blk * pps + j))],) + (0,) * (len(shape) - 1)
        return pl.BlockSpec(shape, imap)

    col_spec = pl.BlockSpec((1, HEAD_DIM, ATT_HEADS), lambda b, blk, pt: (b, 0, 0))
    grid_spec = pltpu.PrefetchScalarGridSpec(
        num_scalar_prefetch=1,
        grid=(Bd, n_pages // pps),
        in_specs=([col_spec, col_spec, col_spec,
                   pl.BlockSpec((1, 1, SMALL_W), lambda b, blk, pt: (b, 0, 0)),
                   pl.BlockSpec((1, SMALL_W), lambda b, blk, pt: (0, 0))]
                  + [page_spec((1, ATT_HEADS, HEAD_DIM, page), j) for j in range(pps)]
                  + [page_spec((1, ATT_HEADS, HEAD_DIM, page), j) for j in range(pps)]
                  + [page_spec((1, ATT_HEADS, page), j) for j in range(pps)]),
        out_specs=[pl.BlockSpec((1, ATT_HEADS, HEAD_DIM, 1), lambda b, blk, pt: (b, 0, 0, 0)),
                   pl.BlockSpec((1, 1, ATT_HEADS), lambda b, blk, pt: (b, 0, 0))],
        scratch_shapes=[pltpu.VMEM((ATT_HEADS, HEAD_DIM, page), F32), pltpu.VMEM((ATT_HEADS, page), F32),
                        pltpu.VMEM((ATT_HEADS, page), F32), pltpu.VMEM((ATT_HEADS, HEAD_DIM, page), F32),
                        pltpu.VMEM((ATT_HEADS, page), F32), pltpu.VMEM((pps, ATT_HEADS, page), F32)],
    )
    return pl.pallas_call(
        functools.partial(_decode_attn_kernel, pps=pps),
        grid_spec=grid_spec,
        out_shape=[jax.ShapeDtypeStruct((Bd, ATT_HEADS, HEAD_DIM, 1), F32),
                   jax.ShapeDtypeStruct((Bd, 1, ATT_HEADS), F32)],
        compiler_params=_cparams("arbitrary", "arbitrary"),
        name="fox_decode",
    )(pt_flat, q_t, kn_t, vn_t, small, bf2, *([cache_k_t] * pps), *([cache_v_t] * pps), *([cache_lf_t] * pps))


SSM_STEP_BATCH = 4


def _ssm_step_kernel(xbc_ref, sc_ref, sm_ref, z_ref, h0_ref, cw_ref, cb_ref, dtb_ref, alog_ref, e_ref,
                     dx_ref, nw_ref, y_ref, st_ref):
    for i in range(xbc_ref.shape[0]):
        _ssm_step_one(i, xbc_ref, sc_ref, sm_ref, z_ref, h0_ref, cw_ref, cb_ref, dtb_ref, alog_ref, e_ref,
                      dx_ref, nw_ref, y_ref, st_ref)


def _ssm_step_one(i, xbc_ref, sc_ref, sm_ref, z_ref, h0_ref, cw_ref, cb_ref, dtb_ref, alog_ref, e_ref,
                  dx_ref, nw_ref, y_ref, st_ref):
    H = SSM_HEADS
    rows = [xbc_ref[i]] + [sc_ref[i, CONV_WIDTH - 1 - j:CONV_WIDTH - j, :] for j in range(1, CONV_WIDTH)]
    xc = _conv_silu_rows(rows, cw_ref, cb_ref)
    xs = xc[:, :SSM_WIDTH]
    e = e_ref[...]
    dt = _softplus(sm_ref[i] + dtb_ref[...])
    da = jnp.exp(dt * (-jnp.exp(alog_ref[...])))
    both = _expand_heads(jnp.concatenate([jnp.broadcast_to(dt, (SUBLANES, SMALL_W)),
                                          jnp.broadcast_to(da, (SUBLANES, SMALL_W))], axis=0), e)
    dt_x = both[0:1, :]
    da_x = both[SUBLANES:SUBLANES + 1, :]
    xdt = xs * dt_x

    sub = lax.broadcasted_iota(I32, (H, SSM_WIDTH), 0)
    own = sub == lax.broadcasted_iota(I32, (H, SSM_WIDTH), 1) // SSM_HEAD_DIM

    def masked_parts(v):
        m = jnp.where(own, jnp.broadcast_to(v, (H, SSM_WIDTH)), 0.0)
        hi = m.astype(BF16)
        return hi, (m - hi.astype(F32)).astype(BF16)

    da_hi, da_lo = masked_parts(da_x)
    x_hi, x_lo = masked_parts(xdt)
    lhs = jnp.concatenate([da_hi, da_lo, x_hi, x_lo], axis=0)
    hpg = H // SSM_GROUPS
    grp = lax.broadcasted_iota(I32, (H, D_STATE), 0) // hpg
    b_rows = jnp.zeros((H, D_STATE), F32)
    c_rows = jnp.zeros((H, D_STATE), F32)
    for g in range(SSM_GROUPS):
        bg = xc[:, SSM_WIDTH + g * D_STATE:SSM_WIDTH + (g + 1) * D_STATE]
        cg = xc[:, SSM_WIDTH + (SSM_GROUPS + g) * D_STATE:SSM_WIDTH + (SSM_GROUPS + g + 1) * D_STATE]
        b_rows = jnp.where(grp == g, jnp.broadcast_to(bg, (H, D_STATE)), b_rows)
        c_rows = jnp.where(grp == g, jnp.broadcast_to(cg, (H, D_STATE)), c_rows)
    ones = jnp.ones((2 * H, D_STATE), BF16)
    zeros = jnp.zeros((2 * H, D_STATE), BF16)
    b_bf = b_rows.astype(BF16)
    rhs = jnp.concatenate([jnp.concatenate([ones, zeros], axis=1),
                           jnp.concatenate([zeros, jnp.concatenate([b_bf, b_bf], axis=0)], axis=1)], axis=0)
    mix = lax.dot_general(lhs, rhs, (((0,), (0,)), ((), ())), preferred_element_type=F32)
    h0 = h0_ref[i].reshape(SSM_WIDTH, D_STATE)
    h_new = mix[:, :D_STATE] * h0 + mix[:, D_STATE:]
    st_ref[i] = h_new.reshape(H, SSM_HEAD_DIM, D_STATE)
    y_t = lax.dot_general(c_rows.astype(BF16), h_new.astype(BF16), (((1,), (1,)), ((), ())),
                          preferred_element_type=F32)
    y = jnp.sum(jnp.where(own, y_t, 0.0), axis=0, keepdims=True) + dx_ref[...] * xs
    gated = y * _silu(z_ref[i].astype(F32))
    y_ref[i] = _rmsnorm_rows(gated, nw_ref[...]).astype(y_ref.dtype)


def _ssm_step(xbc, state_conv, small, z, state_ssm, conv_w, conv_b, dtb, alog, dx, e, ssm_norm_w):
    Bd = xbc.shape[0]
    bb = math.gcd(SSM_STEP_BATCH, Bd)
    full = lambda a: pl.BlockSpec(a.shape, lambda b: (0,) * a.ndim)
    row = lambda w: pl.BlockSpec((bb, 1, w), lambda b: (b, 0, 0))
    st_spec = pl.BlockSpec((bb, SSM_HEADS, SSM_HEAD_DIM, D_STATE), lambda b: (b, 0, 0, 0))
    cb = conv_b.reshape(1, CONV_DIM)
    nw = ssm_norm_w.reshape(1, SSM_WIDTH)
    return pl.pallas_call(
        _ssm_step_kernel,
        grid=(Bd // bb,),
        in_specs=[row(CONV_DIM), pl.BlockSpec((bb, CONV_WIDTH - 1, CONV_DIM), lambda b: (b, 0, 0)),
                  row(SMALL_W), row(SSM_WIDTH), st_spec,
                  full(conv_w), full(cb), full(dtb), full(alog), full(e), full(dx), full(nw)],
        out_specs=[row(SSM_WIDTH), st_spec],
        out_shape=[jax.ShapeDtypeStruct((Bd, 1, SSM_WIDTH), BF16),
                   jax.ShapeDtypeStruct(state_ssm.shape, F32)],
        compiler_params=_cparams("arbitrary"),
        name="ssm_step",
    )(xbc, state_conv, small, z, state_ssm, conv_w, cb, dtb, alog, e, dx, nw)


def _outproj_kernel(att_ref, ssm_ref, x_ref, g1_ref, sh2_ref, sc2_ref, n2_ref, wa_ref, wsm_ref, rw_ref, rb_ref,
                    cin_ref, h2_prev_ref, x1_ref, h2_ref, rt_ref, tg_ref, cout_ref, cnt_ref):
    del h2_prev_ref

    @pl.when((pl.program_id(0) == 0) & (pl.program_id(1) == 0))
    def _():
        cnt_ref[...] = cin_ref[...]

    y = (jnp.dot(att_ref[0], wa_ref[...], preferred_element_type=F32)
         + jnp.dot(ssm_ref[0], wsm_ref[...], preferred_element_type=F32))
    x1 = x_ref[0] + g1_ref[0] * y
    x1_ref[0] = x1
    h2 = _rmsnorm_rows(x1, n2_ref[...]) * (1.0 + sc2_ref[0]) + sh2_ref[0]
    h2_ref[...] = _pack_bf16_pairs(h2)
    logits = _dot_split(h2, h2.astype(BF16), rw_ref) + rb_ref[...]
    tm = logits.shape[0]
    lane = lax.broadcasted_iota(I32, (tm, LANES), 1).astype(F32)
    cur = logits
    idxs = []
    val_tile = jnp.full((tm, LANES), NEG_BIG, F32)
    chosen = jnp.zeros((tm, LANES), F32)
    for k in range(TOP_K):
        m = jnp.max(cur, axis=1, keepdims=True)
        idx = jnp.min(jnp.where(cur == m, lane, float(LANES)), axis=1, keepdims=True)
        idxs.append(idx)
        val_tile = jnp.where(lane == float(k), m, val_tile)
        hit = lane == idx
        chosen = jnp.where(hit, 1.0, chosen)
        cur = jnp.where(hit, 2.0 * NEG_BIG, cur)
    top = jnp.max(val_tile, axis=1, keepdims=True)
    ex = jnp.exp2((val_tile - top) * LOG2E)
    tg_ref[0] = ex / jnp.sum(ex, axis=1, keepdims=True)

    before = (lax.broadcasted_iota(I32, (tm, tm), 1) < lax.broadcasted_iota(I32, (tm, tm), 0)).astype(BF16)
    rank = jnp.dot(before, chosen.astype(BF16), preferred_element_type=F32) + cnt_ref[0:1, :]
    cnt_ref[0:1, :] = cnt_ref[0:1, :] + jnp.sum(chosen, axis=0, keepdims=True)
    cout_ref[...] = cnt_ref[...]
    route = jnp.zeros((tm, LANES), F32)
    for k in range(TOP_K):
        rank_k = jnp.sum(jnp.where(lane == idxs[k], rank, 0.0), axis=1, keepdims=True)
        route = jnp.where(lane == float(k), idxs[k], route)
        route = jnp.where(lane == float(TOP_K + k), rank_k, route)
    rt_ref[0] = route.astype(I32)


def _out_proj(att, ssm, x, g1, sh2, sc2, norm2_w, wa, wsm, rw, rb, counts_in, h2_all, row0, tm):
    B, L, _ = x.shape
    steps = L // tm
    assert row0 % tm == 0
    per_row = g1.shape[1] != 1
    mod_spec = (pl.BlockSpec((1, tm, D_MODEL), lambda b, i: (b, i, 0)) if per_row
                else pl.BlockSpec((1, 1, D_MODEL), lambda b, i: (b, 0, 0)))
    full = lambda a: pl.BlockSpec(a.shape, lambda b, i: (0,) * a.ndim)
    row = lambda w: pl.BlockSpec((1, tm, w), lambda b, i: (b, i, 0))
    n2 = norm2_w.reshape(1, D_MODEL)
    cnt_spec = pl.BlockSpec((SUBLANES, LANES), lambda b, i: (0, 0))
    return pl.pallas_call(
        _outproj_kernel,
        grid=(B, L // tm),
        in_specs=[row(ATT_WIDTH), row(SSM_WIDTH), row(D_MODEL), mod_spec, mod_spec, mod_spec,
                  full(n2), full(wa), full(wsm), full(rw), full(rb), cnt_spec, pl.BlockSpec(memory_space=pl.ANY)],
        out_specs=[row(D_MODEL), pl.BlockSpec((tm, D_MODEL // 2), lambda b, i: (row0 // tm + b * steps + i, 0)),
                   row(LANES), row(LANES), cnt_spec],
        out_shape=[jax.ShapeDtypeStruct((B, L, D_MODEL), F32), jax.ShapeDtypeStruct(h2_all.shape, h2_all.dtype),
                   jax.ShapeDtypeStruct((B, L, LANES), I32), jax.ShapeDtypeStruct((B, L, LANES), F32),
                   jax.ShapeDtypeStruct((SUBLANES, LANES), F32)],
        input_output_aliases={12: 1},
        scratch_shapes=[pltpu.VMEM((SUBLANES, LANES), F32)],
        compiler_params=_cparams("arbitrary", "arbitrary"),
        name="out_proj_route",
    )(att, ssm, x, g1, sh2, sc2, n2, wa, wsm, rw, rb, counts_in, h2_all)


def _moe_block_tables(counts, n_blocks):
    tb = MOE_ROWS
    nb = (counts + tb - 1) // tb
    cum = jnp.cumsum(nb)
    blk_start = cum - nb
    b = jnp.arange(n_blocks, dtype=I32)
    block_expert = jnp.minimum(jnp.sum((cum[None, :] <= b[:, None]).astype(I32), axis=1), N_EXPERTS - 1)
    rows_left = counts[block_expert] - (b - blk_start[block_expert]) * tb
    block_rows = jnp.where(b < cum[-1], jnp.clip(rows_left, 0, tb), 0).astype(I32)
    prev = jnp.concatenate([jnp.full((1,), -1, I32), block_expert[:-1]])
    block_first = ((block_expert != prev) & (block_rows > 0)).astype(I32)
    return (blk_start * tb).astype(I32), block_expert.astype(I32), block_first, block_rows


SC_WORKERS = 32
SC_SCATTER_WINDOW = 48
SC_WINDOW = 64


def _sc_scatter_rows(src, slots, n_rows):
    T, d = src.shape
    win = SC_SCATTER_WINDOW
    steps = T // win * TOP_K
    assert T % win == 0 and steps % SC_WORKERS == 0
    idx = slots.reshape(T // win, win, TOP_K).transpose(0, 2, 1).reshape(steps, win)
    idx_rows = jnp.pad(idx, ((0, 0), (0, LANES - win)))
    mesh = plsc.VectorSubcoreMesh(core_axis_name="c", subcore_axis_name="s")

    @functools.partial(pl.kernel, out_type=jax.ShapeDtypeStruct((n_rows, d), src.dtype), mesh=mesh, name="moe_scatter")
    def scatter(x_hbm, i_hbm, o_hbm):
        def body(x_vmem, i_vmem):
            pltpu.sync_copy(x_vmem, o_hbm.at[i_vmem.at[0, pl.ds(0, win)]])

        pltpu.emit_pipeline(body, grid=(steps,),
                            in_specs=[pl.BlockSpec((win, d), lambda i: (i // TOP_K, 0)),
                                      pl.BlockSpec((1, LANES), lambda i: (i, 0))],
                            out_specs=[],
                            core_axis_name=("c", "s"), dimension_semantics=(pltpu.PARALLEL,))(x_hbm, i_hbm)

    return scatter(src, idx_rows)


def _moe_kernel(be_ref, first_ref, rows_ref, x_ref, wgu_ref, bgu_ref, wd_ref, bd_ref, o_ref, wgu_s, wd_s):
    i = pl.program_id(0)

    @pl.when(first_ref[i] == 1)
    def _():
        wgu_s[...] = wgu_ref[0].astype(BF16)
        wd_s[...] = wd_ref[0].astype(BF16)

    @pl.when(rows_ref[i] > 0)
    def _():
        live = lax.broadcasted_iota(I32, x_ref.shape, 0) < rows_ref[i]
        x = _unpack_bf16_pairs(jnp.where(live, x_ref[...], jnp.uint32(0))).astype(BF16)
        gu = jnp.dot(x, wgu_s[...], preferred_element_type=F32) + bgu_ref[0]
        g = jnp.minimum(gu[:, :D_FF], SWIGLU_LIMIT)
        u = jnp.clip(gu[:, D_FF:], -SWIGLU_LIMIT, SWIGLU_LIMIT)
        act = (u + 1.0) * (g * jax.nn.sigmoid(SWIGLU_ALPHA * g))
        o_ref[...] = _pack_bf16_pairs(jnp.dot(act.astype(BF16), wd_s[...], preferred_element_type=F32) + bd_ref[0])

    @pl.when(rows_ref[i] == 0)
    def _():
        o_ref[...] = jnp.zeros_like(o_ref)


def _moe_blocks(block_expert, block_first, block_rows, x_sorted, w_gate_up, b_gate_up, w_down, b_down):
    n_rows = x_sorted.shape[0]
    tb = MOE_ROWS
    grid_spec = pltpu.PrefetchScalarGridSpec(
        num_scalar_prefetch=3,
        grid=(n_rows // tb,),
        in_specs=[pl.BlockSpec((tb, D_MODEL // 2), lambda i, be, bf, br: (i, 0)),
                  pl.BlockSpec((1, D_MODEL, 2 * D_FF), lambda i, be, bf, br: (be[i], 0, 0)),
                  pl.BlockSpec((1, 1, 2 * D_FF), lambda i, be, bf, br: (be[i], 0, 0)),
                  pl.BlockSpec((1, D_FF, D_MODEL), lambda i, be, bf, br: (be[i], 0, 0)),
                  pl.BlockSpec((1, 1, D_MODEL), lambda i, be, bf, br: (be[i], 0, 0))],
        out_specs=pl.BlockSpec((tb, D_MODEL // 2), lambda i, be, bf, br: (i, 0)),
        scratch_shapes=[pltpu.VMEM((D_MODEL, 2 * D_FF), BF16), pltpu.VMEM((D_FF, D_MODEL), BF16)],
    )
    return pl.pallas_call(
        _moe_kernel,
        grid_spec=grid_spec,
        out_shape=jax.ShapeDtypeStruct((n_rows, D_MODEL // 2), jnp.uint32),
        compiler_params=_cparams("arbitrary"),
        name="moe_experts",
    )(block_expert, block_first, block_rows, x_sorted, w_gate_up,
      b_gate_up.reshape(N_EXPERTS, 1, 2 * D_FF), w_down, b_down.reshape(N_EXPERTS, 1, D_MODEL))


def _sc_gather_rows(table, idx):
    n = idx.shape[0]
    d = table.shape[1]
    win = SC_WINDOW
    idx_rows = jnp.pad(idx.reshape(n // win, win), ((0, 0), (0, LANES - win)))
    mesh = plsc.VectorSubcoreMesh(core_axis_name="c", subcore_axis_name="s")

    @functools.partial(pl.kernel, out_type=jax.ShapeDtypeStruct((n, d), table.dtype), mesh=mesh, name="moe_gather")
    def gather(t_hbm, i_hbm, o_hbm):
        def body(i_vmem, o_vmem):
            pltpu.sync_copy(t_hbm.at[i_vmem.at[0, pl.ds(0, win)]], o_vmem)

        pltpu.emit_pipeline(body, grid=(n // win,),
                            in_specs=[pl.BlockSpec((1, LANES), lambda i: (i, 0))],
                            out_specs=[pl.BlockSpec((win, d), lambda i: (i, 0))],
                            core_axis_name=("c", "s"), dimension_semantics=(pltpu.PARALLEL,))(i_hbm, o_hbm)

    return gather(table, idx_rows)


def _final_kernel(x1_ref, tg_ref, g2_ref, nw_ref, y0_ref, y1_ref, y2_ref, y3_ref, o_ref):
    gates = tg_ref[0]
    moe = gates[:, 0:1] * _unpack_bf16_pairs(y0_ref[...])
    for k, y_ref in enumerate((y1_ref, y2_ref, y3_ref), start=1):
        moe = moe + gates[:, k:k + 1] * _unpack_bf16_pairs(y_ref[...])
    o_ref[0] = _rmsnorm_rows(x1_ref[0] + g2_ref[0] * moe, nw_ref[...])


def _moe_final(x1, gates, g2, final_norm_w, y_rows, row0, tt):
    B, L, _ = x1.shape
    steps = L // tt
    blocks_per_k = B * steps
    base = row0 // tt
    assert row0 % tt == 0 and TOP_K == 4
    per_row = g2.shape[1] != 1
    mod_spec = (pl.BlockSpec((1, tt, D_MODEL), lambda b, i: (b, i, 0)) if per_row
                else pl.BlockSpec((1, 1, D_MODEL), lambda b, i: (b, 0, 0)))
    y_specs = [pl.BlockSpec((tt, D_MODEL // 2), functools.partial(lambda b, i, k: (base + k * blocks_per_k + b * steps + i, 0), k=k))
               for k in range(TOP_K)]
    nw = final_norm_w.reshape(1, D_MODEL)
    return pl.pallas_call(
        _final_kernel,
        grid=(B, steps),
        in_specs=[pl.BlockSpec((1, tt, D_MODEL), lambda b, i: (b, i, 0)),
                  pl.BlockSpec((1, tt, LANES), lambda b, i: (b, i, 0)),
                  mod_spec, pl.BlockSpec((1, D_MODEL), lambda b, i: (0, 0))] + y_specs,
        out_specs=pl.BlockSpec((1, tt, D_MODEL), lambda b, i: (b, i, 0)),
        out_shape=jax.ShapeDtypeStruct((B, L, D_MODEL), F32),
        compiler_params=_cparams("arbitrary", "arbitrary"),
        name="moe_final",
    )(x1, gates, g2, nw, y_rows, y_rows, y_rows, y_rows)


def kernel(x_prompt, x_sample, c_prompt, c_sample, cache_k, cache_v, cache_lf, state_conv, state_ssm, page_table,
           ada_w, ada_b, norm1_w, w_in, b_f, conv_w, conv_b, dt_bias, A_log, D_skip, ssm_norm_w, w_out,
           norm2_w, router_w, router_b, w_gate_up, b_gate_up, w_down, b_down, final_norm_w):
    assert ada_w.shape[0] == 1, "single-layer trunk"
    B, L, D = x_prompt.shape
    Bd = x_sample.shape[0]
    assert x_sample.shape[1] == 1 and L % SSD_CHUNK == 0

    n_c = B + Bd
    rows = -(-n_c // SUBLANES) * SUBLANES
    c_all = jnp.concatenate([c_prompt, c_sample, jnp.zeros((rows - n_c, D), F32)], axis=0)
    mod = _modulation(c_all, ada_w[0], ada_b[0])
    mod_p = [m.reshape(B, 1, D) for m in jnp.split(mod[:B], 6, axis=-1)]
    mod_s = [m.reshape(1, Bd, D) for m in jnp.split(mod[B:n_c], 6, axis=-1)]

    w = w_in[0]
    o_f = 3 * ATT_WIDTH
    o_z = o_f + ATT_HEADS
    o_x = o_z + SSM_WIDTH
    o_dt = o_x + CONV_DIM
    w_small = jnp.concatenate([w[:, o_f:o_z], w[:, o_dt:o_dt + SSM_HEADS],
                               jnp.zeros((D, SMALL_W - ATT_HEADS - SSM_HEADS), F32)], axis=1)
    wts = (w[:, :ATT_WIDTH].astype(BF16), w[:, ATT_WIDTH:2 * ATT_WIDTH].astype(BF16),
           w[:, 2 * ATT_WIDTH:o_f].astype(BF16), w[:, o_z:o_x].astype(BF16), w[:, o_x:o_dt].astype(BF16), _split_weight(w_small))
    wa = w_out[0][:ATT_WIDTH].astype(BF16)
    wsm = w_out[0][ATT_WIDTH:].astype(BF16)
    rw = _split_weight(jnp.pad(router_w[0], ((0, 0), (0, LANES - N_EXPERTS))))
    rb = jnp.pad(router_b[0].reshape(1, N_EXPERTS), ((0, 0), (0, LANES - N_EXPERTS)), constant_values=NEG_BIG)
    dtb, alog, dx = _ssm_params(dt_bias[0], A_log[0], D_skip[0])
    e = jnp.asarray(_head_expander(), BF16)

    tm_p = min(512, L)
    qp, kp, vp, k_p, v_p, z_p, xbc_p, small_p, lf_p = _in_proj(x_prompt, mod_p[0], mod_p[1], norm1_w[0], wts,
                                                              b_f[0], tm_p)
    att_p = _flash_attention(qp, kp, vp, min(1024, L), min(512, L))
    ssm_p, st_p = _ssd_prompt(xbc_p, small_p, z_p, conv_w[0], conv_b[0], dtb, alog, dx, e, ssm_norm_w[0])
    n_tok = B * L + Bd
    zero_counts = jnp.zeros((SUBLANES, LANES), F32)
    h2_all = jnp.zeros((n_tok, D // 2), jnp.uint32)
    x1_p, h2_all, rt_p, tg_p, counts_p = _out_proj(att_p, ssm_p, x_prompt, mod_p[2], mod_p[3], mod_p[4], norm2_w[0],
                                                   wa, wsm, rw, rb, zero_counts, h2_all, 0, min(512, L))

    xs_rows = x_sample.reshape(1, Bd, D)
    qp_s, _, _, k_s, v_s, z_s, xbc_s, small_s, _ = _in_proj(xs_rows, mod_s[0], mod_s[1], norm1_w[0], wts, b_f[0], Bd)
    per_row = lambda a: a.reshape(Bd, 1, a.shape[-1])
    head_cols = lambda a: a.reshape(Bd, ATT_HEADS, HEAD_DIM).transpose(0, 2, 1).astype(F32)
    q_halves = qp_s.astype(F32).reshape(Bd, ATT_HEADS, 2, HEAD_DIM)
    odd_head = (jnp.arange(ATT_HEADS) % 2 == 1)[None, :, None]
    q_s = jnp.where(odd_head, q_halves[:, :, 1, :], q_halves[:, :, 0, :])
    att_s4, lf_s = _decode_attention(
        page_table, head_cols(q_s), head_cols(k_s), head_cols(v_s), per_row(small_s), b_f[0],
        cache_k[0].transpose(0, 2, 3, 1), cache_v[0].transpose(0, 2, 3, 1), cache_lf[0].transpose(0, 2, 1))
    att_s = att_s4.reshape(1, Bd, ATT_WIDTH).astype(BF16)
    ssm_s, st_s = _ssm_step(per_row(xbc_s), state_conv[0], per_row(small_s), per_row(z_s), state_ssm[0],
                            conv_w[0], conv_b[0], dtb, alog, dx, e, ssm_norm_w[0])
    x1_s, h2_all, rt_s, tg_s, counts = _out_proj(att_s, ssm_s.reshape(1, Bd, SSM_WIDTH), xs_rows, mod_s[2], mod_s[3],
                                                 mod_s[4], norm2_w[0], wa, wsm, rw, rb, counts_p, h2_all, B * L, Bd)

    n_blocks = (n_tok * TOP_K + N_EXPERTS * (MOE_ROWS - 1)) // MOE_ROWS
    pad_start, b_exp, b_first, b_rows = _moe_block_tables(counts[0, :N_EXPERTS].astype(I32), n_blocks)

    def token_slots(rt):
        e_idx = rt[..., :TOP_K].reshape(-1, TOP_K)
        first = jnp.sum(jnp.where(e_idx[..., None] == jnp.arange(N_EXPERTS, dtype=I32), pad_start, 0), axis=-1)
        return first + rt[..., TOP_K:2 * TOP_K].reshape(-1, TOP_K)

    slots_p, slots_s = token_slots(rt_p), token_slots(rt_s)
    x_sorted = _sc_scatter_rows(h2_all, jnp.concatenate([slots_p, slots_s], axis=0), n_blocks * MOE_ROWS)
    y_sorted = _moe_blocks(b_exp, b_first, b_rows, x_sorted, w_gate_up[0], b_gate_up[0], w_down[0], b_down[0])

    n_assign = n_tok * TOP_K
    chunk = SC_WINDOW * SC_WORKERS
    n_idx = -(-n_assign // chunk) * chunk
    slots = jnp.concatenate([slots_p.T.reshape(-1), slots_s.T.reshape(-1), jnp.zeros((n_idx - n_assign,), I32)])
    y_rows = _sc_gather_rows(y_sorted, slots)
    y_prompt = _moe_final(x1_p, tg_p, mod_p[5], final_norm_w, y_rows, 0, min(512, L))
    y_sample = _moe_final(x1_s, tg_s, mod_s[5], final_norm_w, y_rows, B * L * TOP_K, Bd).reshape(Bd, 1, D)

    conv_s = jnp.concatenate([state_conv[0][:, 1:], xbc_s.reshape(Bd, 1, CONV_DIM)], axis=1)
    ssm_state_p = st_p.reshape(B, D_STATE, SSM_HEADS, SSM_HEAD_DIM).transpose(0, 2, 3, 1)
    return (y_prompt, y_sample,
            k_p.reshape(1, B, L, ATT_HEADS, HEAD_DIM), v_p.reshape(1, B, L, ATT_HEADS, HEAD_DIM),
            lf_p.reshape(1, B, L, ATT_HEADS), xbc_p[:, L - (CONV_WIDTH - 1):].reshape(1, B, CONV_WIDTH - 1, CONV_DIM),
            ssm_state_p.reshape(1, B, SSM_HEADS, SSM_HEAD_DIM, D_STATE),
            k_s.reshape(1, Bd, 1, ATT_HEADS, HEAD_DIM), v_s.reshape(1, Bd, 1, ATT_HEADS, HEAD_DIM),
            lf_s.reshape(1, Bd, 1, ATT_HEADS), conv_s.reshape(1, Bd, CONV_WIDTH - 1, CONV_DIM),
            st_s.reshape(1, Bd, SSM_HEADS, SSM_HEAD_DIM, D_STATE))
```

```python
import functools
import math

import numpy as np
import jax
import jax.numpy as jnp
from jax import lax
from jax.experimental import pallas as pl
from jax.experimental.pallas import tpu as pltpu
from jax.experimental.pallas import tpu_sc as plsc

F32 = jnp.float32
BF16 = jnp.bfloat16
I32 = jnp.int32
HIGHEST = lax.Precision.HIGHEST

D_MODEL = 1024
ATT_HEADS = 16
HEAD_DIM = 64
ATT_WIDTH = ATT_HEADS * HEAD_DIM
SSM_HEADS = 16
SSM_HEAD_DIM = 64
SSM_WIDTH = SSM_HEADS * SSM_HEAD_DIM
SSM_GROUPS = 2
D_STATE = 128
CONV_WIDTH = 4
CONV_DIM = SSM_WIDTH + 2 * SSM_GROUPS * D_STATE
SSD_CHUNK = 128
N_EXPERTS = 32
TOP_K = 4
D_FF = D_MODEL
SWIGLU_LIMIT = 7.0
SWIGLU_ALPHA = 1.702
NORM_EPS = 1e-5

LANES = 128
SUBLANES = 8
SMALL_W = LANES
DT_COL = ATT_HEADS
NEG_BIG = -1e30
LOG2E = math.log2(math.e)
VMEM_LIMIT = 48 * 1024 * 1024
INPROJ_VMEM_LIMIT = 58 * 1024 * 1024
MOE_ROWS = 512


def _cparams(*sem):
    return pltpu.CompilerParams(dimension_semantics=sem, vmem_limit_bytes=VMEM_LIMIT)


def _silu(x):
    return x * jax.nn.sigmoid(x)


def _softplus(x):
    return jnp.maximum(x, 0.0) + jnp.log(1.0 + jnp.exp(-jnp.abs(x)))


def _log_sigmoid(x):
    return -_softplus(-x)


def _rmsnorm_rows(x, w):
    var = jnp.mean(x * x, axis=-1, keepdims=True)
    return x * lax.rsqrt(var + NORM_EPS) * w


def _split3_bf16(x):
    hi = x.astype(BF16)
    r = x - hi.astype(F32)
    mid = r.astype(BF16)
    lo = (r - mid.astype(F32)).astype(BF16)
    return hi, mid, lo


def _pack_bf16_pairs(x):
    half = x.shape[1] // 2
    lo = pltpu.bitcast(x[:, :half].astype(BF16).astype(F32), jnp.uint32)
    hi = pltpu.bitcast(x[:, half:].astype(BF16).astype(F32), jnp.uint32)
    return hi | lax.shift_right_logical(lo, jnp.uint32(16))


def _unpack_bf16_pairs(w):
    lo = pltpu.bitcast(lax.shift_left(w, jnp.uint32(16)), F32)
    hi = pltpu.bitcast(w & jnp.uint32(0xFFFF0000), F32)
    return jnp.concatenate([lo, hi], axis=1)


def _split_weight(w):
    hi = w.astype(BF16)
    lo = (w - hi.astype(F32)).astype(BF16)
    return jnp.concatenate([hi, lo], axis=1)


def _dot_split(x, x_hi, w_ref):
    x_lo = (x - x_hi.astype(F32)).astype(BF16)
    both = jnp.dot(x_hi, w_ref[...], preferred_element_type=F32)
    return (both[:, :LANES] + both[:, LANES:]
            + jnp.dot(x_lo, w_ref[:, :LANES], preferred_element_type=F32))


def _mod_kernel(c_ref, w_ref, b_ref, o_ref):
    s = _silu(c_ref[...]).astype(BF16)
    o_ref[...] = jnp.dot(s, w_ref[...].astype(BF16), preferred_element_type=F32) + b_ref[...]


def _modulation(c_all, ada_w, ada_b):
    rows = c_all.shape[0]
    n_out = ada_w.shape[1]
    tn = D_MODEL
    return pl.pallas_call(
        _mod_kernel,
        grid=(n_out // tn,),
        in_specs=[pl.BlockSpec((rows, D_MODEL), lambda j: (0, 0)),
                  pl.BlockSpec((D_MODEL, tn), lambda j: (0, j)),
                  pl.BlockSpec((1, tn), lambda j: (0, j))],
        out_specs=pl.BlockSpec((rows, tn), lambda j: (0, j)),
        out_shape=jax.ShapeDtypeStruct((rows, n_out), F32),
        compiler_params=_cparams("arbitrary"),
        name="adaln_mod",
    )(c_all, ada_w, ada_b.reshape(1, n_out))


def _free_half(h):
    return h * LANES + (HEAD_DIM if h % 2 == 0 else 0)


def _bias_layout():
    wide = ATT_HEADS * LANES
    sel = np.zeros((SMALL_W, wide), np.float32)
    rows = np.zeros((5, wide), np.float32)
    for h in range(ATT_HEADS):
        base = _free_half(h)
        rows[4, base] = 1.0
        for part in range(3):
            sel[part * ATT_HEADS + h, base + part] = 1.0
            sel[part * ATT_HEADS + h, base + 3 + part] = -1.0
            rows[0, base + part] = 1.0
            rows[3, base + part] = 1.0
            rows[1, base + 3 + part] = 1.0
            rows[2, base + 3 + part] = 1.0
    return sel, rows


def _inproj_kernel(x_ref, sh_ref, sc_ref, nw_ref, wq_ref, wk_ref, wv_ref, wz_ref, wx_ref, ws_ref, bf_ref,
                   sel_ref, rows_ref, qp_ref, kp_ref, vp_ref, k_ref, v_ref, z_ref, xbc_ref, sm_ref, lf_ref,
                   carry_ref):
    tm = x_ref.shape[1]

    @pl.when(pl.program_id(1) == 0)
    def _():
        carry_ref[...] = jnp.zeros_like(carry_ref)

    h = _rmsnorm_rows(x_ref[0], nw_ref[...]) * (1.0 + sc_ref[0]) + sh_ref[0]
    hb = h.astype(BF16)
    qb = (jnp.dot(hb, wq_ref[...], preferred_element_type=F32) * (HEAD_DIM ** -0.5 * LOG2E)).astype(BF16)
    k = jnp.dot(hb, wk_ref[...], preferred_element_type=F32)
    k_ref[0] = k
    kb = k.astype(BF16)
    v = jnp.dot(hb, wv_ref[...], preferred_element_type=F32)
    v_ref[0] = v
    vb = v.astype(BF16)
    z_ref[0] = jnp.dot(hb, wz_ref[...], preferred_element_type=F32).astype(BF16)
    xbc_ref[0] = jnp.dot(hb, wx_ref[...], preferred_element_type=F32)
    sm = _dot_split(h, hb, ws_ref)
    sm_ref[0] = sm

    lf = _log_sigmoid(sm + bf_ref[...])
    lf_ref[0] = lf[:, :ATT_HEADS]
    tri = (lax.broadcasted_iota(I32, (tm, tm), 1) <= lax.broadcasted_iota(I32, (tm, tm), 0)).astype(BF16)
    sums = jnp.dot(tri, jnp.concatenate(_split3_bf16(lf), axis=1), preferred_element_type=F32)
    fcum = sums[:, :LANES] + sums[:, LANES:2 * LANES] + sums[:, 2 * LANES:] + carry_ref[0:1, :]
    carry_ref[0:1, :] = fcum[tm - 1:tm, :]
    hi, mid, lo = (part.astype(F32) for part in _split3_bf16(fcum * LOG2E))
    lane = lax.broadcasted_iota(I32, (tm, LANES), 1)
    packed = jnp.where(lane < ATT_HEADS, hi,
                       jnp.where(lane < 2 * ATT_HEADS, pltpu.roll(mid, ATT_HEADS, 1),
                                 jnp.where(lane < 3 * ATT_HEADS, pltpu.roll(lo, 2 * ATT_HEADS, 1), 0.0)))
    spread = jnp.dot(packed.astype(BF16), sel_ref[...], preferred_element_type=F32)
    low = lane < HEAD_DIM
    for pair in range(ATT_HEADS // 2):
        ps = slice(pair * LANES, (pair + 1) * LANES)
        for hh in range(2):
            hd = 2 * pair + hh
            keep = low if hh == 0 else jnp.logical_not(low)
            sl = slice(hd * LANES, (hd + 1) * LANES)
            part = spread[:, sl]
            aug_q = (part * rows_ref[0:1, sl] + rows_ref[2:3, sl]).astype(BF16)
            aug_k = (part * rows_ref[1:2, sl] + rows_ref[3:4, sl]).astype(BF16)
            ones_lane = jnp.broadcast_to(rows_ref[4:5, sl], (tm, LANES)).astype(BF16)
            qp_ref[0, :, sl] = jnp.where(keep, qb[:, ps], aug_q)
            kp_ref[0, :, sl] = jnp.where(keep, kb[:, ps], aug_k)
            vp_ref[0, :, sl] = jnp.where(keep, vb[:, ps], ones_lane)


def _in_proj(x, sh, sc, norm_w, wts, b_f, tm):
    B, L, _ = x.shape
    per_row = sh.shape[1] != 1
    mod_spec = (pl.BlockSpec((1, tm, D_MODEL), lambda b, i: (b, i, 0)) if per_row
                else pl.BlockSpec((1, 1, D_MODEL), lambda b, i: (b, 0, 0)))
    wq, wk, wv, wz, wx, ws = wts
    sel, rows = _bias_layout()
    sel, rows = jnp.asarray(sel, BF16), jnp.asarray(rows)
    bf2 = jnp.pad(b_f.reshape(1, ATT_HEADS), ((0, 0), (0, SMALL_W - ATT_HEADS)))
    wide = ATT_HEADS * LANES

    def wspec(w):
        return pl.BlockSpec(w.shape, lambda b, i: (0, 0), pipeline_mode=pl.Buffered(1))

    def ospec(width):
        return pl.BlockSpec((1, tm, width), lambda b, i: (b, i, 0))

    def oshape(width, dt):
        return jax.ShapeDtypeStruct((B, L, width), dt)

    return pl.pallas_call(
        _inproj_kernel,
        grid=(B, L // tm),
        in_specs=[pl.BlockSpec((1, tm, D_MODEL), lambda b, i: (b, i, 0)), mod_spec, mod_spec,
                  pl.BlockSpec((1, D_MODEL), lambda b, i: (0, 0)),
                  wspec(wq), wspec(wk), wspec(wv), wspec(wz), wspec(wx), wspec(ws),
                  wspec(bf2), wspec(sel), wspec(rows)],
        out_specs=[ospec(wide), ospec(wide), ospec(wide), ospec(ATT_WIDTH), ospec(ATT_WIDTH),
                   ospec(SSM_WIDTH), ospec(CONV_DIM), ospec(SMALL_W), ospec(ATT_HEADS)],
        out_shape=[oshape(wide, BF16), oshape(wide, BF16), oshape(wide, BF16),
                   oshape(ATT_WIDTH, F32), oshape(ATT_WIDTH, F32),
                   oshape(SSM_WIDTH, BF16), oshape(CONV_DIM, F32), oshape(SMALL_W, F32), oshape(ATT_HEADS, F32)],
        scratch_shapes=[pltpu.VMEM((SUBLANES, LANES), F32)],
        compiler_params=pltpu.CompilerParams(dimension_semantics=("arbitrary", "arbitrary"),
                                             vmem_limit_bytes=INPROJ_VMEM_LIMIT),
        name="in_proj",
    )(x, sh, sc, norm_w.reshape(1, D_MODEL), wq, wk, wv, wz, wx, ws, bf2, sel, rows)


FLASH_HEADS = 8


def _flash_kernel(qi_ref, ki_ref, qp_ref, kp_ref, vp_ref, o_ref, m_ref, acc_ref, *, tq, tk):
    t = pl.program_id(2)
    qi = qi_ref[t]
    ki = ki_ref[t]
    last = ((qi + 1) * tq - 1) // tk

    @pl.when(ki == 0)
    def _():
        m_ref[...] = jnp.full_like(m_ref, NEG_BIG)
        acc_ref[...] = jnp.zeros_like(acc_ref)

    def step(masked):
        if masked:
            qpos = qi * tq + lax.broadcasted_iota(I32, (tq, tk), 0)
            kpos = ki * tk + lax.broadcasted_iota(I32, (tq, tk), 1)
            visible = kpos <= qpos
        for hh in range(FLASH_HEADS):
            q = qp_ref[0, :, hh * LANES:(hh + 1) * LANES]
            k = kp_ref[0, :, hh * LANES:(hh + 1) * LANES]
            s = lax.dot_general(q, k, (((1,), (1,)), ((), ())), preferred_element_type=F32)
            if masked:
                s = jnp.where(visible, s, NEG_BIG)
            m_prev = m_ref[hh]
            m_new = jnp.maximum(m_prev, jnp.max(s, axis=1, keepdims=True))
            p = jnp.exp2(s - jnp.concatenate([m_new] * (tk // LANES), axis=1))
            acc_ref[hh] = (jnp.exp2(m_prev - m_new) * acc_ref[hh]
                           + jnp.dot(p.astype(BF16), vp_ref[0, :, hh * LANES:(hh + 1) * LANES],
                                     preferred_element_type=F32))
            m_ref[hh] = m_new

    crosses = (ki + 1) * tk - 1 > qi * tq

    @pl.when(crosses)
    def _():
        step(True)

    @pl.when(jnp.logical_not(crosses))
    def _():
        step(False)

    @pl.when(ki == last)
    def _():
        lane = lax.broadcasted_iota(I32, (tq, LANES), 1)
        for pr in range(FLASH_HEADS // 2):
            a0 = acc_ref[2 * pr]
            a1 = acc_ref[2 * pr + 1]
            o0 = a0 / a0[:, HEAD_DIM:HEAD_DIM + 1]
            o1 = a1 / a1[:, 0:1]
            o_ref[0, :, pr * LANES:(pr + 1) * LANES] = jnp.where(lane < HEAD_DIM, o0, o1).astype(o_ref.dtype)


def _flash_attention(qp, kp, vp, tq, tk):
    B, L, _ = qp.shape
    pairs = ATT_HEADS // FLASH_HEADS
    hw = FLASH_HEADS * LANES
    qs, ks = [], []
    for qi in range(L // tq):
        for ki in range(((qi + 1) * tq - 1) // tk + 1):
            qs.append(qi)
            ks.append(ki)
    qi_tab = jnp.asarray(np.array(qs, np.int32))
    ki_tab = jnp.asarray(np.array(ks, np.int32))
    grid_spec = pltpu.PrefetchScalarGridSpec(
        num_scalar_prefetch=2,
        grid=(B, pairs, len(qs)),
        in_specs=[pl.BlockSpec((1, tq, hw), lambda b, p, t, qt, kt: (b, qt[t], p)),
                  pl.BlockSpec((1, tk, hw), lambda b, p, t, qt, kt: (b, kt[t], p)),
                  pl.BlockSpec((1, tk, hw), lambda b, p, t, qt, kt: (b, kt[t], p))],
        out_specs=pl.BlockSpec((1, tq, hw // 2), lambda b, p, t, qt, kt: (b, qt[t], p)),
        scratch_shapes=[pltpu.VMEM((FLASH_HEADS, tq, LANES), F32), pltpu.VMEM((FLASH_HEADS, tq, LANES), F32)],
    )
    return pl.pallas_call(
        functools.partial(_flash_kernel, tq=tq, tk=tk),
        grid_spec=grid_spec,
        out_shape=jax.ShapeDtypeStruct((B, L, ATT_WIDTH), BF16),
        compiler_params=_cparams("arbitrary", "arbitrary", "arbitrary"),
        name="fox_flash",
    )(qi_tab, ki_tab, qp, kp, vp)


def _head_expander():
    e = np.zeros((SMALL_W, SSM_WIDTH), np.float32)
    for h in range(SSM_HEADS):
        e[DT_COL + h, h * SSM_HEAD_DIM:(h + 1) * SSM_HEAD_DIM] = 1.0
    return e


def _expand_heads(vals, e_bf16):
    hi = vals.astype(BF16)
    lo = (vals - hi.astype(F32)).astype(BF16)
    return (jnp.dot(hi, e_bf16, preferred_element_type=F32)
            + jnp.dot(lo, e_bf16, preferred_element_type=F32))


def _conv_silu_rows(rows, cw_ref, cb_ref):
    acc = cb_ref[...] + cw_ref[CONV_WIDTH - 1:CONV_WIDTH, :] * rows[0]
    for j in range(1, CONV_WIDTH):
        acc = acc + cw_ref[CONV_WIDTH - 1 - j:CONV_WIDTH - j, :] * rows[j]
    return _silu(acc)


def _ssd_kernel(xbc_ref, sm_ref, z_ref, cw_ref, cb_ref, dtb_ref, alog_ref, e_ref, dx_ref, nw_ref,
                y_ref, st_ref, buf_ref, ht_ref):
    c = pl.program_id(1)
    nc = pl.num_programs(1)
    Q = SSD_CHUNK
    halo = SUBLANES

    @pl.when(c == 0)
    def _():
        buf_ref[0:halo, :] = jnp.zeros((halo, CONV_DIM), F32)
        ht_ref[...] = jnp.zeros_like(ht_ref)

    @pl.when(c > 0)
    def _():
        buf_ref[0:halo, :] = buf_ref[Q:Q + halo, :]

    buf_ref[halo:halo + Q, :] = xbc_ref[0]
    xc = _conv_silu_rows([buf_ref[halo - j:halo - j + Q, :] for j in range(CONV_WIDTH)], cw_ref, cb_ref)
    xs = xc[:, :SSM_WIDTH]
    e = e_ref[...]

    dt = _softplus(sm_ref[0] + dtb_ref[...])
    a = dt * (-jnp.exp(alog_ref[...]))
    row = lax.broadcasted_iota(I32, (Q, Q), 0)
    col = lax.broadcasted_iota(I32, (Q, Q), 1)
    causal = col <= row
    acum = jnp.dot(causal.astype(F32), a, precision=HIGHEST, preferred_element_type=F32)
    acum_t = acum.T
    dt_x = _expand_heads(dt, e)
    acum_x = _expand_heads(acum, e)
    last_x = acum_x[Q - 1:Q, :]
    xdt = xs * dt_x
    xdt_b = xdt.astype(BF16)
    x_end = (xdt * jnp.exp(last_x - acum_x)).astype(BF16)
    grow = jnp.exp(acum_x)
    cdecay = jnp.exp(last_x)

    lane = lax.broadcasted_iota(I32, (Q, LANES), 1)
    low = lane < SSM_HEAD_DIM
    hpg = SSM_HEADS // SSM_GROUPS
    gw = hpg * SSM_HEAD_DIM
    y_parts = []
    for g in range(SSM_GROUPS):
        bg = xc[:, SSM_WIDTH + g * D_STATE:SSM_WIDTH + (g + 1) * D_STATE].astype(BF16)
        cg = xc[:, SSM_WIDTH + (SSM_GROUPS + g) * D_STATE:SSM_WIDTH + (SSM_GROUPS + g + 1) * D_STATE].astype(BF16)
        scores = lax.dot_general(cg, bg, (((1,), (1,)), ((), ())), preferred_element_type=F32)
        gs = slice(g * gw, (g + 1) * gw)
        h_prev = ht_ref[:, gs]
        y_off = jnp.dot(cg, h_prev.astype(BF16), preferred_element_type=F32) * grow[:, gs]
        ht_ref[:, gs] = h_prev * cdecay[:, gs] + lax.dot_general(
            bg, x_end[:, gs], (((0,), (0,)), ((), ())), preferred_element_type=F32)
        for pr in range(hpg // 2):
            pair_lo = g * gw + pr * LANES
            xpair = xdt_b[:, pair_lo:pair_lo + LANES]
            halves = []
            for hh in range(2):
                h = g * hpg + 2 * pr + hh
                decay = jnp.where(causal, jnp.exp(acum[:, DT_COL + h:DT_COL + h + 1]
                                                  - acum_t[DT_COL + h:DT_COL + h + 1, :]), 0.0)
                halves.append(jnp.dot((scores * decay).astype(BF16), xpair, preferred_element_type=F32))
            y_parts.append(jnp.where(low, halves[0], halves[1]) + y_off[:, pr * LANES:(pr + 1) * LANES])
    y = jnp.concatenate(y_parts, axis=1) + dx_ref[...] * xs
    gated = y * _silu(z_ref[0].astype(F32))
    y_ref[0] = _rmsnorm_rows(gated, nw_ref[...]).astype(y_ref.dtype)

    @pl.when(c == nc - 1)
    def _():
        st_ref[0] = ht_ref[...]


def _ssm_params(dt_bias, A_log, D_skip):
    pad = (DT_COL, SMALL_W - DT_COL - SSM_HEADS)
    dtb = jnp.pad(dt_bias.astype(F32), pad).reshape(1, SMALL_W)
    alog = jnp.pad(A_log.astype(F32), pad).reshape(1, SMALL_W)
    dx = jnp.repeat(D_skip.astype(F32), SSM_HEAD_DIM).reshape(1, SSM_WIDTH)
    return dtb, alog, dx


def _ssd_prompt(xbc, small, z, conv_w, conv_b, dtb, alog, dx, e, ssm_norm_w):
    B, L, _ = xbc.shape
    Q = SSD_CHUNK
    full = lambda a: pl.BlockSpec(a.shape, lambda b, c: (0,) * a.ndim)
    row_spec = lambda w: pl.BlockSpec((1, Q, w), lambda b, c: (b, c, 0))
    cb = conv_b.reshape(1, CONV_DIM)
    nw = ssm_norm_w.reshape(1, SSM_WIDTH)
    return pl.pallas_call(
        _ssd_kernel,
        grid=(B, L // Q),
        in_specs=[row_spec(CONV_DIM), row_spec(SMALL_W), row_spec(SSM_WIDTH),
                  full(conv_w), full(cb), full(dtb), full(alog), full(e), full(dx), full(nw)],
        out_specs=[row_spec(SSM_WIDTH), pl.BlockSpec((1, D_STATE, SSM_WIDTH), lambda b, c: (b, 0, 0))],
        out_shape=[jax.ShapeDtypeStruct((B, L, SSM_WIDTH), BF16),
                   jax.ShapeDtypeStruct((B, D_STATE, SSM_WIDTH), F32)],
        scratch_shapes=[pltpu.VMEM((Q + SUBLANES, CONV_DIM), F32), pltpu.VMEM((D_STATE, SSM_WIDTH), F32)],
        compiler_params=_cparams("arbitrary", "arbitrary"),
        name="ssd_prompt",
    )(xbc, small, z, conv_w, cb, dtb, alog, e, dx, nw)


DECODE_PAGES = 16


def _decode_attn_kernel(pt_ref, qt_ref, knt_ref, vnt_ref, sm_ref, bf_ref, *refs, pps):
    k_refs = refs[0:pps]
    v_refs = refs[pps:2 * pps]
    lf_refs = refs[2 * pps:3 * pps]
    o_ref, lfo_ref, qrep_ref, m_ref, l_ref, acc_ref, carry_ref, bias_ref = refs[3 * pps:]
    blk = pl.program_id(1)
    nblk = pl.num_programs(1)
    H = ATT_HEADS
    page = k_refs[0].shape[3]
    lane_row = lax.broadcasted_iota(I32, (1, page), 1)

    @pl.when(blk == 0)
    def _():
        lf_new = _log_sigmoid(sm_ref[0] + bf_ref[...])
        lfo_ref[0] = lf_new[:, :H]
        diag = (lax.broadcasted_iota(I32, (H, SMALL_W), 0) == lax.broadcasted_iota(I32, (H, SMALL_W), 1))
        lf_col = jnp.sum(jnp.where(diag, jnp.broadcast_to(lf_new, (H, SMALL_W)), 0.0), axis=1, keepdims=True)
        carry_ref[...] = jnp.broadcast_to(lf_col, (H, page))
        qt = qt_ref[0]
        knt = knt_ref[0].astype(BF16).astype(F32)
        vnt = vnt_ref[0].astype(BF16).astype(F32)
        s_row = jnp.sum(qt * knt, axis=0, keepdims=True)
        lane = lax.broadcasted_iota(I32, (HEAD_DIM, page), 1)
        for h in range(H):
            qrep_ref[h] = jnp.broadcast_to(qt[:, h:h + 1], (HEAD_DIM, page))
            m_ref[h:h + 1, :] = jnp.where(lane_row == 0, jnp.broadcast_to(s_row[:, h:h + 1], (1, page)), NEG_BIG)
            acc_ref[h] = jnp.where(lane == 0, jnp.broadcast_to(vnt[:, h:h + 1], (HEAD_DIM, page)), 0.0)
        l_ref[...] = jnp.broadcast_to(jnp.where(lane_row == 0, 1.0, 0.0), (H, page))

    later = (lax.broadcasted_iota(I32, (page, page), 0) > lax.broadcasted_iota(I32, (page, page), 1)).astype(F32)
    carry = carry_ref[...]
    lf_all = jnp.concatenate([lf_refs[j][0] for j in range(pps)], axis=0)
    suffix = jnp.dot(lf_all, later, precision=HIGHEST, preferred_element_type=F32)
    for j in range(pps):
        bias_ref[j] = (suffix[j * H:(j + 1) * H, :] + carry) * LOG2E
        carry = carry + jnp.sum(lf_refs[j][0], axis=1, keepdims=True)
    carry_ref[...] = carry

    def head_body(h, _):
        q3 = qrep_ref[h]
        row = pl.ds(h, 1)
        m = m_ref[row, :]
        l = l_ref[row, :]
        acc = acc_ref[h]
        for j in range(pps):
            s = jnp.sum(q3 * k_refs[j][0, h], axis=0, keepdims=True) + bias_ref[j, row, :]
            m_new = jnp.maximum(m, s)
            alpha = jnp.exp2(m - m_new)
            p = jnp.exp2(s - m_new)
            l = alpha * l + p
            acc = alpha * acc + p * v_refs[j][0, h]
            m = m_new
        m_ref[row, :] = m
        l_ref[row, :] = l
        acc_ref[h] = acc
        return 0

    lax.fori_loop(0, H, head_body, 0)

    @pl.when(blk == nblk - 1)
    def _():
        m_all = m_ref[...]
        w = jnp.exp2(m_all - jnp.max(m_all, axis=1, keepdims=True))
        den = jnp.sum(l_ref[...] * w, axis=1, keepdims=True)
        heads = []
        for h in range(H):
            w_rows = jnp.broadcast_to(w[h:h + 1, :], (SUBLANES, page))
            num = lax.dot_general(w_rows, acc_ref[h], (((1,), (1,)), ((), ())), precision=HIGHEST,
                                  preferred_element_type=F32)
            heads.append(num[0:1, :] / den[h:h + 1, :])
        o_ref[0] = jnp.concatenate(heads, axis=1)


def _decode_attention(page_table, q_t, kn_t, vn_t, small, b_f, cache_k_t, cache_v_t, cache_lf_t):
    Bd = q_t.shape[0]
    n_pages = page_table.shape[1]
    page = cache_k_t.shape[3]
    pps = math.gcd(DECODE_PAGES, n_pages)
    pt_flat = page_table.reshape(-1)
    bf2 = jnp.pad(b_f.reshape(1, ATT_HEADS), ((0, 0), (0, SMALL_W - ATT_HEADS)))

    def page_spec(shape, j):
        def imap(b, blk, pt):
            return (pt[b * n_pages + (n_pages - 1 - (blk * pps + j))],) + (0,) * (len(shape) - 1)
        return pl.BlockSpec(shape, imap)

    col_spec = pl.BlockSpec((1, HEAD_DIM, ATT_HEADS), lambda b, blk, pt: (b, 0, 0))
    grid_spec = pltpu.PrefetchScalarGridSpec(
        num_scalar_prefetch=1,
        grid=(Bd, n_pages // pps),
        in_specs=([col_spec, col_spec, col_spec,
                   pl.BlockSpec((1, 1, SMALL_W), lambda b, blk, pt: (b, 0, 0)),
                   pl.BlockSpec((1, SMALL_W), lambda b, blk, pt: (0, 0))]
                  + [page_spec((1, ATT_HEADS, HEAD_DIM, page), j) for j in range(pps)]
                  + [page_spec((1, ATT_HEADS, HEAD_DIM, page), j) for j in range(pps)]
                  + [page_spec((1, ATT_HEADS, page), j) for j in range(pps)]),
        out_specs=[pl.BlockSpec((1, 1, ATT_WIDTH), lambda b, blk, pt: (b, 0, 0)),
                   pl.BlockSpec((1, 1, ATT_HEADS), lambda b, blk, pt: (b, 0, 0))],
        scratch_shapes=[pltpu.VMEM((ATT_HEADS, HEAD_DIM, page), F32), pltpu.VMEM((ATT_HEADS, page), F32),
                        pltpu.VMEM((ATT_HEADS, page), F32), pltpu.VMEM((ATT_HEADS, HEAD_DIM, page), F32),
                        pltpu.VMEM((ATT_HEADS, page), F32), pltpu.VMEM((pps, ATT_HEADS, page), F32)],
    )
    return pl.pallas_call(
        functools.partial(_decode_attn_kernel, pps=pps),
        grid_spec=grid_spec,
        out_shape=[jax.ShapeDtypeStruct((Bd, 1, ATT_WIDTH), F32),
                   jax.ShapeDtypeStruct((Bd, 1, ATT_HEADS), F32)],
        compiler_params=_cparams("arbitrary", "arbitrary"),
        name="fox_decode",
    )(pt_flat, q_t, kn_t, vn_t, small, bf2, *([cache_k_t] * pps), *([cache_v_t] * pps), *([cache_lf_t] * pps))


SSM_STEP_BATCH = 4


def _ssm_step_kernel(xbc_ref, sc_ref, sm_ref, z_ref, h0_ref, cw_ref, cb_ref, dtb_ref, alog_ref, e_ref,
                     dx_ref, nw_ref, y_ref, st_ref):
    for i in range(xbc_ref.shape[0]):
        _ssm_step_one(i, xbc_ref, sc_ref, sm_ref, z_ref, h0_ref, cw_ref, cb_ref, dtb_ref, alog_ref, e_ref,
                      dx_ref, nw_ref, y_ref, st_ref)


def _ssm_step_one(i, xbc_ref, sc_ref, sm_ref, z_ref, h0_ref, cw_ref, cb_ref, dtb_ref, alog_ref, e_ref,
                  dx_ref, nw_ref, y_ref, st_ref):
    H = SSM_HEADS
    rows = [xbc_ref[i]] + [sc_ref[i, CONV_WIDTH - 1 - j:CONV_WIDTH - j, :] for j in range(1, CONV_WIDTH)]
    xc = _conv_silu_rows(rows, cw_ref, cb_ref)
    xs = xc[:, :SSM_WIDTH]
    e = e_ref[...]
    dt = _softplus(sm_ref[i] + dtb_ref[...])
    da = jnp.exp(dt * (-jnp.exp(alog_ref[...])))
    both = _expand_heads(jnp.concatenate([jnp.broadcast_to(dt, (SUBLANES, SMALL_W)),
                                          jnp.broadcast_to(da, (SUBLANES, SMALL_W))], axis=0), e)
    dt_x = both[0:1, :]
    da_x = both[SUBLANES:SUBLANES + 1, :]
    xdt = xs * dt_x

    sub = lax.broadcasted_iota(I32, (H, SSM_WIDTH), 0)
    own = sub == lax.broadcasted_iota(I32, (H, SSM_WIDTH), 1) // SSM_HEAD_DIM

    def masked_parts(v):
        m = jnp.where(own, jnp.broadcast_to(v, (H, SSM_WIDTH)), 0.0)
        hi = m.astype(BF16)
        return hi, (m - hi.astype(F32)).astype(BF16)

    da_hi, da_lo = masked_parts(da_x)
    x_hi, x_lo = masked_parts(xdt)
    lhs = jnp.concatenate([da_hi, da_lo, x_hi, x_lo], axis=0)
    hpg = H // SSM_GROUPS
    grp = lax.broadcasted_iota(I32, (H, D_STATE), 0) // hpg
    b_rows = jnp.zeros((H, D_STATE), F32)
    c_rows = jnp.zeros((H, D_STATE), F32)
    for g in range(SSM_GROUPS):
        bg = xc[:, SSM_WIDTH + g * D_STATE:SSM_WIDTH + (g + 1) * D_STATE]
        cg = xc[:, SSM_WIDTH + (SSM_GROUPS + g) * D_STATE:SSM_WIDTH + (SSM_GROUPS + g + 1) * D_STATE]
        b_rows = jnp.where(grp == g, jnp.broadcast_to(bg, (H, D_STATE)), b_rows)
        c_rows = jnp.where(grp == g, jnp.broadcast_to(cg, (H, D_STATE)), c_rows)
    ones = jnp.ones((2 * H, D_STATE), BF16)
    zeros = jnp.zeros((2 * H, D_STATE), BF16)
    b_bf = b_rows.astype(BF16)
    rhs = jnp.concatenate([jnp.concatenate([ones, zeros], axis=1),
                           jnp.concatenate([zeros, jnp.concatenate([b_bf, b_bf], axis=0)], axis=1)], axis=0)
    mix = lax.dot_general(lhs, rhs, (((0,), (0,)), ((), ())), preferred_element_type=F32)
    h0 = h0_ref[i].reshape(SSM_WIDTH, D_STATE)
    h_new = mix[:, :D_STATE] * h0 + mix[:, D_STATE:]
    st_ref[i] = h_new.reshape(H, SSM_HEAD_DIM, D_STATE)
    y_t = lax.dot_general(c_rows.astype(BF16), h_new.astype(BF16), (((1,), (1,)), ((), ())),
                          preferred_element_type=F32)
    y = jnp.sum(jnp.where(own, y_t, 0.0), axis=0, keepdims=True) + dx_ref[...] * xs
    gated = y * _silu(z_ref[i].astype(F32))
    y_ref[i] = _rmsnorm_rows(gated, nw_ref[...]).astype(y_ref.dtype)


def _ssm_step(xbc, state_conv, small, z, state_ssm, conv_w, conv_b, dtb, alog, dx, e, ssm_norm_w):
    Bd = xbc.shape[0]
    bb = math.gcd(SSM_STEP_BATCH, Bd)
    full = lambda a: pl.BlockSpec(a.shape, lambda b: (0,) * a.ndim)
    row = lambda w: pl.BlockSpec((bb, 1, w), lambda b: (b, 0, 0))
    st_spec = pl.BlockSpec((bb, SSM_HEADS, SSM_HEAD_DIM, D_STATE), lambda b: (b, 0, 0, 0))
    cb = conv_b.reshape(1, CONV_DIM)
    nw = ssm_norm_w.reshape(1, SSM_WIDTH)
    return pl.pallas_call(
        _ssm_step_kernel,
        grid=(Bd // bb,),
        in_specs=[row(CONV_DIM), pl.BlockSpec((bb, CONV_WIDTH - 1, CONV_DIM), lambda b: (b, 0, 0)),
                  row(SMALL_W), row(SSM_WIDTH), st_spec,
                  full(conv_w), full(cb), full(dtb), full(alog), full(e), full(dx), full(nw)],
        out_specs=[row(SSM_WIDTH), st_spec],
        out_shape=[jax.ShapeDtypeStruct((Bd, 1, SSM_WIDTH), BF16),
                   jax.ShapeDtypeStruct(state_ssm.shape, F32)],
        compiler_params=_cparams("arbitrary"),
        name="ssm_step",
    )(xbc, state_conv, small, z, state_ssm, conv_w, cb, dtb, alog, e, dx, nw)


def _outproj_kernel(att_ref, ssm_ref, x_ref, g1_ref, sh2_ref, sc2_ref, n2_ref, wa_ref, wsm_ref, rw_ref, rb_ref,
                    cin_ref, h2_prev_ref, x1_ref, h2_ref, rt_ref, tg_ref, cout_ref, cnt_ref):
    del h2_prev_ref

    @pl.when((pl.program_id(0) == 0) & (pl.program_id(1) == 0))
    def _():
        cnt_ref[...] = cin_ref[...]

    y = (jnp.dot(att_ref[0], wa_ref[...], preferred_element_type=F32)
         + jnp.dot(ssm_ref[0], wsm_ref[...], preferred_element_type=F32))
    x1 = x_ref[0] + g1_ref[0] * y
    x1_ref[0] = x1
    h2 = _rmsnorm_rows(x1, n2_ref[...]) * (1.0 + sc2_ref[0]) + sh2_ref[0]
    h2_ref[...] = _pack_bf16_pairs(h2)
    logits = _dot_split(h2, h2.astype(BF16), rw_ref) + rb_ref[...]
    tm = logits.shape[0]
    lane = lax.broadcasted_iota(I32, (tm, LANES), 1).astype(F32)
    cur = logits
    idxs = []
    val_tile = jnp.full((tm, LANES), NEG_BIG, F32)
    chosen = jnp.zeros((tm, LANES), F32)
    for k in range(TOP_K):
        m = jnp.max(cur, axis=1, keepdims=True)
        idx = jnp.min(jnp.where(cur == m, lane, float(LANES)), axis=1, keepdims=True)
        idxs.append(idx)
        val_tile = jnp.where(lane == float(k), m, val_tile)
        hit = lane == idx
        chosen = jnp.where(hit, 1.0, chosen)
        cur = jnp.where(hit, 2.0 * NEG_BIG, cur)
    top = jnp.max(val_tile, axis=1, keepdims=True)
    ex = jnp.exp2((val_tile - top) * LOG2E)
    tg_ref[0] = ex / jnp.sum(ex, axis=1, keepdims=True)

    before = (lax.broadcasted_iota(I32, (tm, tm), 1) < lax.broadcasted_iota(I32, (tm, tm), 0)).astype(BF16)
    rank = jnp.dot(before, chosen.astype(BF16), preferred_element_type=F32) + cnt_ref[0:1, :]
    cnt_ref[0:1, :] = cnt_ref[0:1, :] + jnp.sum(chosen, axis=0, keepdims=True)
    cout_ref[...] = cnt_ref[...]
    route = jnp.zeros((tm, LANES), F32)
    for k in range(TOP_K):
        rank_k = jnp.sum(jnp.where(lane == idxs[k], rank, 0.0), axis=1, keepdims=True)
        route = jnp.where(lane == float(k), idxs[k], route)
        route = jnp.where(lane == float(TOP_K + k), rank_k, route)
    rt_ref[0] = route.astype(I32)


def _out_proj(att, ssm, x, g1, sh2, sc2, norm2_w, wa, wsm, rw, rb, counts_in, h2_all, row0, tm):
    B, L, _ = x.shape
    steps = L // tm
    assert row0 % tm == 0
    per_row = g1.shape[1] != 1
    mod_spec = (pl.BlockSpec((1, tm, D_MODEL), lambda b, i: (b, i, 0)) if per_row
                else pl.BlockSpec((1, 1, D_MODEL), lambda b, i: (b, 0, 0)))
    full = lambda a: pl.BlockSpec(a.shape, lambda b, i: (0,) * a.ndim)
    row = lambda w: pl.BlockSpec((1, tm, w), lambda b, i: (b, i, 0))
    n2 = norm2_w.reshape(1, D_MODEL)
    cnt_spec = pl.BlockSpec((SUBLANES, LANES), lambda b, i: (0, 0))
    return pl.pallas_call(
        _outproj_kernel,
        grid=(B, L // tm),
        in_specs=[row(ATT_WIDTH), row(SSM_WIDTH), row(D_MODEL), mod_spec, mod_spec, mod_spec,
                  full(n2), full(wa), full(wsm), full(rw), full(rb), cnt_spec, pl.BlockSpec(memory_space=pl.ANY)],
        out_specs=[row(D_MODEL), pl.BlockSpec((tm, D_MODEL // 2), lambda b, i: (row0 // tm + b * steps + i, 0)),
                   row(LANES), row(LANES), cnt_spec],
        out_shape=[jax.ShapeDtypeStruct((B, L, D_MODEL), F32), jax.ShapeDtypeStruct(h2_all.shape, h2_all.dtype),
                   jax.ShapeDtypeStruct((B, L, LANES), I32), jax.ShapeDtypeStruct((B, L, LANES), F32),
                   jax.ShapeDtypeStruct((SUBLANES, LANES), F32)],
        input_output_aliases={12: 1},
        scratch_shapes=[pltpu.VMEM((SUBLANES, LANES), F32)],
        compiler_params=_cparams("arbitrary", "arbitrary"),
        name="out_proj_route",
    )(att, ssm, x, g1, sh2, sc2, n2, wa, wsm, rw, rb, counts_in, h2_all)


def _moe_block_tables(counts, n_blocks):
    tb = MOE_ROWS
    nb = (counts + tb - 1) // tb
    cum = jnp.cumsum(nb)
    blk_start = cum - nb
    b = jnp.arange(n_blocks, dtype=I32)
    block_expert = jnp.minimum(jnp.sum((cum[None, :] <= b[:, None]).astype(I32), axis=1), N_EXPERTS - 1)
    rows_left = counts[block_expert] - (b - blk_start[block_expert]) * tb
    block_rows = jnp.where(b < cum[-1], jnp.clip(rows_left, 0, tb), 0).astype(I32)
    prev = jnp.concatenate([jnp.full((1,), -1, I32), block_expert[:-1]])
    block_first = ((block_expert != prev) & (block_rows > 0)).astype(I32)
    return (blk_start * tb).astype(I32), block_expert.astype(I32), block_first, block_rows


SC_WORKERS = 32
SC_SCATTER_WINDOW = 48
SC_WINDOW = 64


def _sc_scatter_rows(src, slots, n_rows):
    T, d = src.shape
    win = SC_SCATTER_WINDOW
    steps = T // win * TOP_K
    assert T % win == 0 and steps % SC_WORKERS == 0
    idx = slots.reshape(T // win, win, TOP_K).transpose(0, 2, 1).reshape(steps, win)
    idx_rows = jnp.pad(idx, ((0, 0), (0, LANES - win)))
    mesh = plsc.VectorSubcoreMesh(core_axis_name="c", subcore_axis_name="s")

    @functools.partial(pl.kernel, out_type=jax.ShapeDtypeStruct((n_rows, d), src.dtype), mesh=mesh, name="moe_scatter")
    def scatter(x_hbm, i_hbm, o_hbm):
        def body(x_vmem, i_vmem):
            pltpu.sync_copy(x_vmem, o_hbm.at[i_vmem.at[0, pl.ds(0, win)]])

        pltpu.emit_pipeline(body, grid=(steps,),
                            in_specs=[pl.BlockSpec((win, d), lambda i: (i // TOP_K, 0)),
                                      pl.BlockSpec((1, LANES), lambda i: (i, 0))],
                            out_specs=[],
                            core_axis_name=("c", "s"), dimension_semantics=(pltpu.PARALLEL,))(x_hbm, i_hbm)

    return scatter(src, idx_rows)


def _moe_kernel(be_ref, first_ref, rows_ref, x_ref, wgu_ref, bgu_ref, wd_ref, bd_ref, o_ref, wgu_s, wd_s):
    i = pl.program_id(0)

    @pl.when(first_ref[i] == 1)
    def _():
        wgu_s[...] = wgu_ref[0].astype(BF16)
        wd_s[...] = wd_ref[0].astype(BF16)

    @pl.when(rows_ref[i] > 0)
    def _():
        live = lax.broadcasted_iota(I32, x_ref.shape, 0) < rows_ref[i]
        x = _unpack_bf16_pairs(jnp.where(live, x_ref[...], jnp.uint32(0))).astype(BF16)
        gu = jnp.dot(x, wgu_s[...], preferred_element_type=F32) + bgu_ref[0]
        g = jnp.minimum(gu[:, :D_FF], SWIGLU_LIMIT)
        u = jnp.clip(gu[:, D_FF:], -SWIGLU_LIMIT, SWIGLU_LIMIT)
        act = (u + 1.0) * (g * jax.nn.sigmoid(SWIGLU_ALPHA * g))
        o_ref[...] = _pack_bf16_pairs(jnp.dot(act.astype(BF16), wd_s[...], preferred_element_type=F32) + bd_ref[0])

    @pl.when(rows_ref[i] == 0)
    def _():
        o_ref[...] = jnp.zeros_like(o_ref)


def _moe_blocks(block_expert, block_first, block_rows, x_sorted, w_gate_up, b_gate_up, w_down, b_down):
    n_rows = x_sorted.shape[0]
    tb = MOE_ROWS
    grid_spec = pltpu.PrefetchScalarGridSpec(
        num_scalar_prefetch=3,
        grid=(n_rows // tb,),
        in_specs=[pl.BlockSpec((tb, D_MODEL // 2), lambda i, be, bf, br: (i, 0)),
                  pl.BlockSpec((1, D_MODEL, 2 * D_FF), lambda i, be, bf, br: (be[i], 0, 0)),
                  pl.BlockSpec((1, 1, 2 * D_FF), lambda i, be, bf, br: (be[i], 0, 0)),
                  pl.BlockSpec((1, D_FF, D_MODEL), lambda i, be, bf, br: (be[i], 0, 0)),
                  pl.BlockSpec((1, 1, D_MODEL), lambda i, be, bf, br: (be[i], 0, 0))],
        out_specs=pl.BlockSpec((tb, D_MODEL // 2), lambda i, be, bf, br: (i, 0)),
        scratch_shapes=[pltpu.VMEM((D_MODEL, 2 * D_FF), BF16), pltpu.VMEM((D_FF, D_MODEL), BF16)],
    )
    return pl.pallas_call(
        _moe_kernel,
        grid_spec=grid_spec,
        out_shape=jax.ShapeDtypeStruct((n_rows, D_MODEL // 2), jnp.uint32),
        compiler_params=_cparams("arbitrary"),
        name="moe_experts",
    )(block_expert, block_first, block_rows, x_sorted, w_gate_up,
      b_gate_up.reshape(N_EXPERTS, 1, 2 * D_FF), w_down, b_down.reshape(N_EXPERTS, 1, D_MODEL))


def _sc_gather_rows(table, idx):
    n = idx.shape[0]
    d = table.shape[1]
    win = SC_WINDOW
    idx_rows = jnp.pad(idx.reshape(n // win, win), ((0, 0), (0, LANES - win)))
    mesh = plsc.VectorSubcoreMesh(core_axis_name="c", subcore_axis_name="s")

    @functools.partial(pl.kernel, out_type=jax.ShapeDtypeStruct((n, d), table.dtype), mesh=mesh, name="moe_gather")
    def gather(t_hbm, i_hbm, o_hbm):
        def body(i_vmem, o_vmem):
            pltpu.sync_copy(t_hbm.at[i_vmem.at[0, pl.ds(0, win)]], o_vmem)

        pltpu.emit_pipeline(body, grid=(n // win,),
                            in_specs=[pl.BlockSpec((1, LANES), lambda i: (i, 0))],
                            out_specs=[pl.BlockSpec((win, d), lambda i: (i, 0))],
                            core_axis_name=("c", "s"), dimension_semantics=(pltpu.PARALLEL,))(i_hbm, o_hbm)

    return gather(table, idx_rows)


def _final_kernel(x1_ref, tg_ref, g2_ref, nw_ref, y0_ref, y1_ref, y2_ref, y3_ref, o_ref):
    gates = tg_ref[0]
    moe = gates[:, 0:1] * _unpack_bf16_pairs(y0_ref[...])
    for k, y_ref in enumerate((y1_ref, y2_ref, y3_ref), start=1):
        moe = moe + gates[:, k:k + 1] * _unpack_bf16_pairs(y_ref[...])
    o_ref[0] = _rmsnorm_rows(x1_ref[0] + g2_ref[0] * moe, nw_ref[...])


def _moe_final(x1, gates, g2, final_norm_w, y_rows, row0, tt):
    B, L, _ = x1.shape
    steps = L // tt
    blocks_per_k = B * steps
    base = row0 // tt
    assert row0 % tt == 0 and TOP_K == 4
    per_row = g2.shape[1] != 1
    mod_spec = (pl.BlockSpec((1, tt, D_MODEL), lambda b, i: (b, i, 0)) if per_row
                else pl.BlockSpec((1, 1, D_MODEL), lambda b, i: (b, 0, 0)))
    y_specs = [pl.BlockSpec((tt, D_MODEL // 2), functools.partial(lambda b, i, k: (base + k * blocks_per_k + b * steps + i, 0), k=k))
               for k in range(TOP_K)]
    nw = final_norm_w.reshape(1, D_MODEL)
    return pl.pallas_call(
        _final_kernel,
        grid=(B, steps),
        in_specs=[pl.BlockSpec((1, tt, D_MODEL), lambda b, i: (b, i, 0)),
                  pl.BlockSpec((1, tt, LANES), lambda b, i: (b, i, 0)),
                  mod_spec, pl.BlockSpec((1, D_MODEL), lambda b, i: (0, 0))] + y_specs,
        out_specs=pl.BlockSpec((1, tt, D_MODEL), lambda b, i: (b, i, 0)),
        out_shape=jax.ShapeDtypeStruct((B, L, D_MODEL), F32),
        compiler_params=_cparams("arbitrary", "arbitrary"),
        name="moe_final",
    )(x1, gates, g2, nw, y_rows, y_rows, y_rows, y_rows)


def kernel(x_prompt, x_sample, c_prompt, c_sample, cache_k, cache_v, cache_lf, state_conv, state_ssm, page_table,
           ada_w, ada_b, norm1_w, w_in, b_f, conv_w, conv_b, dt_bias, A_log, D_skip, ssm_norm_w, w_out,
           norm2_w, router_w, router_b, w_gate_up, b_gate_up, w_down, b_down, final_norm_w):
    assert ada_w.shape[0] == 1, "single-layer trunk"
    B, L, D = x_prompt.shape
    Bd = x_sample.shape[0]
    assert x_sample.shape[1] == 1 and L % SSD_CHUNK == 0

    n_c = B + Bd
    rows = -(-n_c // SUBLANES) * SUBLANES
    c_all = jnp.concatenate([c_prompt, c_sample, jnp.zeros((rows - n_c, D), F32)], axis=0)
    mod = _modulation(c_all, ada_w[0], ada_b[0])
    mod_p = [m.reshape(B, 1, D) for m in jnp.split(mod[:B], 6, axis=-1)]
    mod_s = [m.reshape(1, Bd, D) for m in jnp.split(mod[B:n_c], 6, axis=-1)]

    w = w_in[0]
    o_f = 3 * ATT_WIDTH
    o_z = o_f + ATT_HEADS
    o_x = o_z + SSM_WIDTH
    o_dt = o_x + CONV_DIM
    w_small = jnp.concatenate([w[:, o_f:o_z], w[:, o_dt:o_dt + SSM_HEADS],
                               jnp.zeros((D, SMALL_W - ATT_HEADS - SSM_HEADS), F32)], axis=1)
    wts = (w[:, :ATT_WIDTH].astype(BF16), w[:, ATT_WIDTH:2 * ATT_WIDTH].astype(BF16),
           w[:, 2 * ATT_WIDTH:o_f].astype(BF16), w[:, o_z:o_x].astype(BF16), w[:, o_x:o_dt].astype(BF16), _split_weight(w_small))
    wa = w_out[0][:ATT_WIDTH].astype(BF16)
    wsm = w_out[0][ATT_WIDTH:].astype(BF16)
    rw = _split_weight(jnp.pad(router_w[0], ((0, 0), (0, LANES - N_EXPERTS))))
    rb = jnp.pad(router_b[0].reshape(1, N_EXPERTS), ((0, 0), (0, LANES - N_EXPERTS)), constant_values=NEG_BIG)
    dtb, alog, dx = _ssm_params(dt_bias[0], A_log[0], D_skip[0])
    e = jnp.asarray(_head_expander(), BF16)

    tm_p = min(512, L)
    qp, kp, vp, k_p, v_p, z_p, xbc_p, small_p, lf_p = _in_proj(x_prompt, mod_p[0], mod_p[1], norm1_w[0], wts,
                                                              b_f[0], tm_p)
    att_p = _flash_attention(qp, kp, vp, min(1024, L), min(512, L))
    ssm_p, st_p = _ssd_prompt(xbc_p, small_p, z_p, conv_w[0], conv_b[0], dtb, alog, dx, e, ssm_norm_w[0])
    n_tok = B * L + Bd
    zero_counts = jnp.zeros((SUBLANES, LANES), F32)
    h2_all = jnp.zeros((n_tok, D // 2), jnp.uint32)
    x1_p, h2_all, rt_p, tg_p, counts_p = _out_proj(att_p, ssm_p, x_prompt, mod_p[2], mod_p[3], mod_p[4], norm2_w[0],
                                                   wa, wsm, rw, rb, zero_counts, h2_all, 0, min(512, L))

    xs_rows = x_sample.reshape(1, Bd, D)
    qp_s, _, _, k_s, v_s, z_s, xbc_s, small_s, _ = _in_proj(xs_rows, mod_s[0], mod_s[1], norm1_w[0], wts, b_f[0], Bd)
    per_row = lambda a: a.reshape(Bd, 1, a.shape[-1])
    head_cols = lambda a: a.reshape(Bd, ATT_HEADS, HEAD_DIM).transpose(0, 2, 1).astype(F32)
    q_halves = qp_s.astype(F32).reshape(Bd, ATT_HEADS, 2, HEAD_DIM)
    odd_head = (jnp.arange(ATT_HEADS) % 2 == 1)[None, :, None]
    q_s = jnp.where(odd_head, q_halves[:, :, 1, :], q_halves[:, :, 0, :])
    att_s4, lf_s = _decode_attention(
        page_table, head_cols(q_s), head_cols(k_s), head_cols(v_s), per_row(small_s), b_f[0],
        cache_k[0].transpose(0, 2, 3, 1), cache_v[0].transpose(0, 2, 3, 1), cache_lf[0].transpose(0, 2, 1))
    att_s = att_s4.reshape(1, Bd, ATT_WIDTH).astype(BF16)
    ssm_s, st_s = _ssm_step(per_row(xbc_s), state_conv[0], per_row(small_s), per_row(z_s), state_ssm[0],
                            conv_w[0], conv_b[0], dtb, alog, dx, e, ssm_norm_w[0])
    x1_s, h2_all, rt_s, tg_s, counts = _out_proj(att_s, ssm_s.reshape(1, Bd, SSM_WIDTH), xs_rows, mod_s[2], mod_s[3],
                                                 mod_s[4], norm2_w[0], wa, wsm, rw, rb, counts_p, h2_all, B * L, Bd)

    n_blocks = (n_tok * TOP_K + N_EXPERTS * (MOE_ROWS - 1)) // MOE_ROWS
    pad_start, b_exp, b_first, b_rows = _moe_block_tables(counts[0, :N_EXPERTS].astype(I32), n_blocks)

    def token_slots(rt):
        e_idx = rt[..., :TOP_K].reshape(-1, TOP_K)
        first = jnp.sum(jnp.where(e_idx[..., None] == jnp.arange(N_EXPERTS, dtype=I32), pad_start, 0), axis=-1)
        return first + rt[..., TOP_K:2 * TOP_K].reshape(-1, TOP_K)

    slots_p, slots_s = token_slots(rt_p), token_slots(rt_s)
    x_sorted = _sc_scatter_rows(h2_all, jnp.concatenate([slots_p, slots_s], axis=0), n_blocks * MOE_ROWS)
    y_sorted = _moe_blocks(b_exp, b_first, b_rows, x_sorted, w_gate_up[0], b_gate_up[0], w_down[0], b_down[0])

    n_assign = n_tok * TOP_K
    chunk = SC_WINDOW * SC_WORKERS
    n_idx = -(-n_assign // chunk) * chunk
    slots = jnp.concatenate([slots_p.T.reshape(-1), slots_s.T.reshape(-1), jnp.zeros((n_idx - n_assign,), I32)])
    y_rows = _sc_gather_rows(y_sorted, slots)
    y_prompt = _moe_final(x1_p, tg_p, mod_p[5], final_norm_w, y_rows, 0, min(512, L))
    y_sample = _moe_final(x1_s, tg_s, mod_s[5], final_norm_w, y_rows, B * L * TOP_K, Bd).reshape(Bd, 1, D)

    conv_s = jnp.concatenate([state_conv[0][:, 1:], xbc_s.reshape(Bd, 1, CONV_DIM)], axis=1)
    ssm_state_p = st_p.reshape(B, D_STATE, SSM_HEADS, SSM_HEAD_DIM).transpose(0, 2, 3, 1)
    return (y_prompt, y_sample,
            k_p.reshape(1, B, L, ATT_HEADS, HEAD_DIM), v_p.reshape(1, B, L, ATT_HEADS, HEAD_DIM),
            lf_p.reshape(1, B, L, ATT_HEADS), xbc_p[:, L - (CONV_WIDTH - 1):].reshape(1, B, CONV_WIDTH - 1, CONV_DIM),
            ssm_state_p.reshape(1, B, SSM_HEADS, SSM_HEAD_DIM, D_STATE),
            k_s.reshape(1, Bd, 1, ATT_HEADS, HEAD_DIM), v_s.reshape(1, Bd, 1, ATT_HEADS, HEAD_DIM),
            lf_s.reshape(1, Bd, 1, ATT_HEADS), conv_s.reshape(1, Bd, CONV_WIDTH - 1, CONV_DIM),
            st_s.reshape(1, Bd, SSM_HEADS, SSM_HEAD_DIM, D_STATE))
```

```python
import functools
import math

import numpy as np
import jax
import jax.numpy as jnp
from jax import lax
from jax.experimental import pallas as pl
from jax.experimental.pallas import tpu as pltpu
from jax.experimental.pallas import tpu_sc as plsc

F32 = jnp.float32
BF16 = jnp.bfloat16
I32 = jnp.int32
HIGHEST = lax.Precision.HIGHEST

D_MODEL = 1024
ATT_HEADS = 16
HEAD_DIM = 64
ATT_WIDTH = ATT_HEADS * HEAD_DIM
SSM_HEADS = 16
SSM_HEAD_DIM = 64
SSM_WIDTH = SSM_HEADS * SSM_HEAD_DIM
SSM_GROUPS = 2
D_STATE = 128
CONV_WIDTH = 4
CONV_DIM = SSM_WIDTH + 2 * SSM_GROUPS * D_STATE
SSD_CHUNK = 128
N_EXPERTS = 32
TOP_K = 4
D_FF = D_MODEL
SWIGLU_LIMIT = 7.0
SWIGLU_ALPHA = 1.702
NORM_EPS = 1e-5

LANES = 128
SUBLANES = 8
SMALL_W = LANES
DT_COL = ATT_HEADS
NEG_BIG = -1e30
LOG2E = math.log2(math.e)
VMEM_LIMIT = 48 * 1024 * 1024
INPROJ_VMEM_LIMIT = 58 * 1024 * 1024
MOE_ROWS = 512


def _cparams(*sem):
    return pltpu.CompilerParams(dimension_semantics=sem, vmem_limit_bytes=VMEM_LIMIT)


def _silu(x):
    return x * jax.nn.sigmoid(x)


def _softplus(x):
    return jnp.maximum(x, 0.0) + jnp.log(1.0 + jnp.exp(-jnp.abs(x)))


def _log_sigmoid(x):
    return -_softplus(-x)


def _rmsnorm_rows(x, w):
    var = jnp.mean(x * x, axis=-1, keepdims=True)
    return x * lax.rsqrt(var + NORM_EPS) * w


def _split3_bf16(x):
    hi = x.astype(BF16)
    r = x - hi.astype(F32)
    mid = r.astype(BF16)
    lo = (r - mid.astype(F32)).astype(BF16)
    return hi, mid, lo


def _pack_bf16_pairs(x):
    half = x.shape[1] // 2
    lo = pltpu.bitcast(x[:, :half].astype(BF16).astype(F32), jnp.uint32)
    hi = pltpu.bitcast(x[:, half:].astype(BF16).astype(F32), jnp.uint32)
    return hi | lax.shift_right_logical(lo, jnp.uint32(16))


def _unpack_bf16_pairs(w):
    lo = pltpu.bitcast(lax.shift_left(w, jnp.uint32(16)), F32)
    hi = pltpu.bitcast(w & jnp.uint32(0xFFFF0000), F32)
    return jnp.concatenate([lo, hi], axis=1)


def _split_weight(w):
    hi = w.astype(BF16)
    lo = (w - hi.astype(F32)).astype(BF16)
    return jnp.concatenate([hi, lo], axis=1)


def _dot_split(x, x_hi, w_ref):
    x_lo = (x - x_hi.astype(F32)).astype(BF16)
    both = jnp.dot(x_hi, w_ref[...], preferred_element_type=F32)
    return (both[:, :LANES] + both[:, LANES:]
            + jnp.dot(x_lo, w_ref[:, :LANES], preferred_element_type=F32))


def _mod_kernel(c_ref, w_ref, b_ref, o_ref):
    s = _silu(c_ref[...]).astype(BF16)
    o_ref[...] = jnp.dot(s, w_ref[...].astype(BF16), preferred_element_type=F32) + b_ref[...]


def _modulation(c_all, ada_w, ada_b):
    rows = c_all.shape[0]
    n_out = ada_w.shape[1]
    tn = D_MODEL
    return pl.pallas_call(
        _mod_kernel,
        grid=(n_out // tn,),
        in_specs=[pl.BlockSpec((rows, D_MODEL), lambda j: (0, 0)),
                  pl.BlockSpec((D_MODEL, tn), lambda j: (0, j)),
                  pl.BlockSpec((1, tn), lambda j: (0, j))],
        out_specs=pl.BlockSpec((rows, tn), lambda j: (0, j)),
        out_shape=jax.ShapeDtypeStruct((rows, n_out), F32),
        compiler_params=_cparams("arbitrary"),
        name="adaln_mod",
    )(c_all, ada_w, ada_b.reshape(1, n_out))


def _free_half(h):
    return h * LANES + (HEAD_DIM if h % 2 == 0 else 0)


def _bias_layout():
    wide = ATT_HEADS * LANES
    sel = np.zeros((SMALL_W, wide), np.float32)
    rows = np.zeros((5, wide), np.float32)
    for h in range(ATT_HEADS):
        base = _free_half(h)
        rows[4, base] = 1.0
        for part in range(3):
            sel[part * ATT_HEADS + h, base + part] = 1.0
            sel[part * ATT_HEADS + h, base + 3 + part] = -1.0
            rows[0, base + part] = 1.0
            rows[3, base + part] = 1.0
            rows[1, base + 3 + part] = 1.0
            rows[2, base + 3 + part] = 1.0
    return sel, rows


def _inproj_kernel(x_ref, sh_ref, sc_ref, nw_ref, wq_ref, wk_ref, wv_ref, wz_ref, wx_ref, ws_ref, bf_ref,
                   sel_ref, rows_ref, qp_ref, kp_ref, vp_ref, k_ref, v_ref, z_ref, xbc_ref, sm_ref, lf_ref,
                   carry_ref):
    tm = x_ref.shape[1]

    @pl.when(pl.program_id(1) == 0)
    def _():
        carry_ref[...] = jnp.zeros_like(carry_ref)

    h = _rmsnorm_rows(x_ref[0], nw_ref[...]) * (1.0 + sc_ref[0]) + sh_ref[0]
    hb = h.astype(BF16)
    qb = (jnp.dot(hb, wq_ref[...], preferred_element_type=F32) * (HEAD_DIM ** -0.5 * LOG2E)).astype(BF16)
    k = jnp.dot(hb, wk_ref[...], preferred_element_type=F32)
    k_ref[0] = k
    kb = k.astype(BF16)
    v = jnp.dot(hb, wv_ref[...], preferred_element_type=F32)
    v_ref[0] = v
    vb = v.astype(BF16)
    z_ref[0] = jnp.dot(hb, wz_ref[...], preferred_element_type=F32).astype(BF16)
    xbc_ref[0] = jnp.dot(hb, wx_ref[...], preferred_element_type=F32)
    sm = _dot_split(h, hb, ws_ref)
    sm_ref[0] = sm

    lf = _log_sigmoid(sm + bf_ref[...])
    lf_ref[0] = lf[:, :ATT_HEADS]
    tri = (lax.broadcasted_iota(I32, (tm, tm), 1) <= lax.broadcasted_iota(I32, (tm, tm), 0)).astype(BF16)
    sums = jnp.dot(tri, jnp.concatenate(_split3_bf16(lf), axis=1), preferred_element_type=F32)
    fcum = sums[:, :LANES] + sums[:, LANES:2 * LANES] + sums[:, 2 * LANES:] + carry_ref[0:1, :]
    carry_ref[0:1, :] = fcum[tm - 1:tm, :]
    hi, mid, lo = (part.astype(F32) for part in _split3_bf16(fcum * LOG2E))
    lane = lax.broadcasted_iota(I32, (tm, LANES), 1)
    packed = jnp.where(lane < ATT_HEADS, hi,
                       jnp.where(lane < 2 * ATT_HEADS, pltpu.roll(mid, ATT_HEADS, 1),
                                 jnp.where(lane < 3 * ATT_HEADS, pltpu.roll(lo, 2 * ATT_HEADS, 1), 0.0)))
    spread = jnp.dot(packed.astype(BF16), sel_ref[...], preferred_element_type=F32)
    low = lane < HEAD_DIM
    for pair in range(ATT_HEADS // 2):
        ps = slice(pair * LANES, (pair + 1) * LANES)
        for hh in range(2):
            hd = 2 * pair + hh
            keep = low if hh == 0 else jnp.logical_not(low)
            sl = slice(hd * LANES, (hd + 1) * LANES)
            part = spread[:, sl]
            aug_q = (part * rows_ref[0:1, sl] + rows_ref[2:3, sl]).astype(BF16)
            aug_k = (part * rows_ref[1:2, sl] + rows_ref[3:4, sl]).astype(BF16)
            ones_lane = jnp.broadcast_to(rows_ref[4:5, sl], (tm, LANES)).astype(BF16)
            qp_ref[0, :, sl] = jnp.where(keep, qb[:, ps], aug_q)
            kp_ref[0, :, sl] = jnp.where(keep, kb[:, ps], aug_k)
            vp_ref[0, :, sl] = jnp.where(keep, vb[:, ps], ones_lane)


def _in_proj(x, sh, sc, norm_w, wts, b_f, tm):
    B, L, _ = x.shape
    per_row = sh.shape[1] != 1
    mod_spec = (pl.BlockSpec((1, tm, D_MODEL), lambda b, i: (b, i, 0)) if per_row
                else pl.BlockSpec((1, 1, D_MODEL), lambda b, i: (b, 0, 0)))
    wq, wk, wv, wz, wx, ws = wts
    sel, rows = _bias_layout()
    sel, rows = jnp.asarray(sel, BF16), jnp.asarray(rows)
    bf2 = jnp.pad(b_f.reshape(1, ATT_HEADS), ((0, 0), (0, SMALL_W - ATT_HEADS)))
    wide = ATT_HEADS * LANES

    def wspec(w):
        return pl.BlockSpec(w.shape, lambda b, i: (0, 0), pipeline_mode=pl.Buffered(1))

    def ospec(width):
        return pl.BlockSpec((1, tm, width), lambda b, i: (b, i, 0))

    def oshape(width, dt):
        return jax.ShapeDtypeStruct((B, L, width), dt)

    return pl.pallas_call(
        _inproj_kernel,
        grid=(B, L // tm),
        in_specs=[pl.BlockSpec((1, tm, D_MODEL), lambda b, i: (b, i, 0)), mod_spec, mod_spec,
                  pl.BlockSpec((1, D_MODEL), lambda b, i: (0, 0)),
                  wspec(wq), wspec(wk), wspec(wv), wspec(wz), wspec(wx), wspec(ws),
                  wspec(bf2), wspec(sel), wspec(rows)],
        out_specs=[ospec(wide), ospec(wide), ospec(wide), ospec(ATT_WIDTH), ospec(ATT_WIDTH),
                   ospec(SSM_WIDTH), ospec(CONV_DIM), ospec(SMALL_W), ospec(ATT_HEADS)],
        out_shape=[oshape(wide, BF16), oshape(wide, BF16), oshape(wide, BF16),
                   oshape(ATT_WIDTH, F32), oshape(ATT_WIDTH, F32),
                   oshape(SSM_WIDTH, BF16), oshape(CONV_DIM, F32), oshape(SMALL_W, F32), oshape(ATT_HEADS, F32)],
        scratch_shapes=[pltpu.VMEM((SUBLANES, LANES), F32)],
        compiler_params=pltpu.CompilerParams(dimension_semantics=("arbitrary", "arbitrary"),
                                             vmem_limit_bytes=INPROJ_VMEM_LIMIT),
        name="in_proj",
    )(x, sh, sc, norm_w.reshape(1, D_MODEL), wq, wk, wv, wz, wx, ws, bf2, sel, rows)


FLASH_HEADS = 8


def _flash_kernel(qi_ref, ki_ref, qp_ref, kp_ref, vp_ref, o_ref, m_ref, acc_ref, *, tq, tk):
    t = pl.program_id(2)
    qi = qi_ref[t]
    ki = ki_ref[t]
    last = ((qi + 1) * tq - 1) // tk

    @pl.when(ki == 0)
    def _():
        m_ref[...] = jnp.full_like(m_ref, NEG_BIG)
        acc_ref[...] = jnp.zeros_like(acc_ref)

    def step(masked, row0=0):
        rows = slice(row0, tq)
        nr = tq - row0
        if masked:
            qpos = qi * tq + row0 + lax.broadcasted_iota(I32, (nr, tk), 0)
            kpos = ki * tk + lax.broadcasted_iota(I32, (nr, tk), 1)
            visible = kpos <= qpos
        for hh in range(FLASH_HEADS):
            q = qp_ref[0, rows, hh * LANES:(hh + 1) * LANES]
            k = kp_ref[0, :, hh * LANES:(hh + 1) * LANES]
            s = lax.dot_general(q, k, (((1,), (1,)), ((), ())), preferred_element_type=F32)
            if masked:
                s = jnp.where(visible, s, NEG_BIG)
            m_prev = m_ref[hh, rows, :]
            m_new = jnp.maximum(m_prev, jnp.max(s, axis=1, keepdims=True))
            p = jnp.exp2(s - jnp.concatenate([m_new] * (tk // LANES), axis=1))
            acc_ref[hh, rows, :] = (jnp.exp2(m_prev - m_new) * acc_ref[hh, rows, :]
                                    + jnp.dot(p.astype(BF16), vp_ref[0, :, hh * LANES:(hh + 1) * LANES],
                                              preferred_element_type=F32))
            m_ref[hh, rows, :] = m_new

    crosses = (ki + 1) * tk - 1 > qi * tq
    skip_rows = tk if tq == 2 * tk else 0

    @pl.when(crosses & (ki == last))
    def _():
        step(True, skip_rows)

    @pl.when(crosses & (ki != last))
    def _():
        step(True)

    @pl.when(jnp.logical_not(crosses))
    def _():
        step(False)

    @pl.when(ki == last)
    def _():
        lane = lax.broadcasted_iota(I32, (tq, LANES), 1)
        for pr in range(FLASH_HEADS // 2):
            a0 = acc_ref[2 * pr]
            a1 = acc_ref[2 * pr + 1]
            o0 = a0 / a0[:, HEAD_DIM:HEAD_DIM + 1]
            o1 = a1 / a1[:, 0:1]
            o_ref[0, :, pr * LANES:(pr + 1) * LANES] = jnp.where(lane < HEAD_DIM, o0, o1).astype(o_ref.dtype)


def _flash_attention(qp, kp, vp, tq, tk):
    B, L, _ = qp.shape
    pairs = ATT_HEADS // FLASH_HEADS
    hw = FLASH_HEADS * LANES
    qs, ks = [], []
    for qi in range(L // tq):
        for ki in range(((qi + 1) * tq - 1) // tk + 1):
            qs.append(qi)
            ks.append(ki)
    qi_tab = jnp.asarray(np.array(qs, np.int32))
    ki_tab = jnp.asarray(np.array(ks, np.int32))
    grid_spec = pltpu.PrefetchScalarGridSpec(
        num_scalar_prefetch=2,
        grid=(B, pairs, len(qs)),
        in_specs=[pl.BlockSpec((1, tq, hw), lambda b, p, t, qt, kt: (b, qt[t], p)),
                  pl.BlockSpec((1, tk, hw), lambda b, p, t, qt, kt: (b, kt[t], p)),
                  pl.BlockSpec((1, tk, hw), lambda b, p, t, qt, kt: (b, kt[t], p))],
        out_specs=pl.BlockSpec((1, tq, hw // 2), lambda b, p, t, qt, kt: (b, qt[t], p)),
        scratch_shapes=[pltpu.VMEM((FLASH_HEADS, tq, LANES), F32), pltpu.VMEM((FLASH_HEADS, tq, LANES), F32)],
    )
    return pl.pallas_call(
        functools.partial(_flash_kernel, tq=tq, tk=tk),
        grid_spec=grid_spec,
        out_shape=jax.ShapeDtypeStruct((B, L, ATT_WIDTH), BF16),
        compiler_params=_cparams("arbitrary", "arbitrary", "arbitrary"),
        name="fox_flash",
    )(qi_tab, ki_tab, qp, kp, vp)


def _head_expander():
    e = np.zeros((SMALL_W, SSM_WIDTH), np.float32)
    for h in range(SSM_HEADS):
        e[DT_COL + h, h * SSM_HEAD_DIM:(h + 1) * SSM_HEAD_DIM] = 1.0
    return e


def _expand_heads(vals, e_bf16):
    hi = vals.astype(BF16)
    lo = (vals - hi.astype(F32)).astype(BF16)
    return (jnp.dot(hi, e_bf16, preferred_element_type=F32)
            + jnp.dot(lo, e_bf16, preferred_element_type=F32))


def _conv_silu_rows(rows, cw_ref, cb_ref):
    acc = cb_ref[...] + cw_ref[CONV_WIDTH - 1:CONV_WIDTH, :] * rows[0]
    for j in range(1, CONV_WIDTH):
        acc = acc + cw_ref[CONV_WIDTH - 1 - j:CONV_WIDTH - j, :] * rows[j]
    return _silu(acc)


def _ssd_kernel(xbc_ref, sm_ref, z_ref, cw_ref, cb_ref, dtb_ref, alog_ref, e_ref, dx_ref, nw_ref,
                y_ref, st_ref, buf_ref, ht_ref):
    c = pl.program_id(1)
    nc = pl.num_programs(1)
    Q = SSD_CHUNK
    halo = SUBLANES

    @pl.when(c == 0)
    def _():
        buf_ref[0:halo, :] = jnp.zeros((halo, CONV_DIM), F32)
        ht_ref[...] = jnp.zeros_like(ht_ref)

    @pl.when(c > 0)
    def _():
        buf_ref[0:halo, :] = buf_ref[Q:Q + halo, :]

    buf_ref[halo:halo + Q, :] = xbc_ref[0]
    xc = _conv_silu_rows([buf_ref[halo - j:halo - j + Q, :] for j in range(CONV_WIDTH)], cw_ref, cb_ref)
    xs = xc[:, :SSM_WIDTH]
    e = e_ref[...]

    dt = _softplus(sm_ref[0] + dtb_ref[...])
    a = dt * (-jnp.exp(alog_ref[...]))
    row = lax.broadcasted_iota(I32, (Q, Q), 0)
    col = lax.broadcasted_iota(I32, (Q, Q), 1)
    causal = col <= row
    acum = jnp.dot(causal.astype(F32), a, precision=HIGHEST, preferred_element_type=F32)
    acum_t = acum.T
    dt_x = _expand_heads(dt, e)
    acum_x = _expand_heads(acum, e)
    last_x = acum_x[Q - 1:Q, :]
    xdt = xs * dt_x
    xdt_b = xdt.astype(BF16)
    x_end = (xdt * jnp.exp(last_x - acum_x)).astype(BF16)
    grow = jnp.exp(acum_x)
    cdecay = jnp.exp(last_x)

    lane = lax.broadcasted_iota(I32, (Q, LANES), 1)
    low = lane < SSM_HEAD_DIM
    hpg = SSM_HEADS // SSM_GROUPS
    gw = hpg * SSM_HEAD_DIM
    y_parts = []
    for g in range(SSM_GROUPS):
        bg = xc[:, SSM_WIDTH + g * D_STATE:SSM_WIDTH + (g + 1) * D_STATE].astype(BF16)
        cg = xc[:, SSM_WIDTH + (SSM_GROUPS + g) * D_STATE:SSM_WIDTH + (SSM_GROUPS + g + 1) * D_STATE].astype(BF16)
        scores = lax.dot_general(cg, bg, (((1,), (1,)), ((), ())), preferred_element_type=F32)
        gs = slice(g * gw, (g + 1) * gw)
        h_prev = ht_ref[:, gs]
        y_off = jnp.dot(cg, h_prev.astype(BF16), preferred_element_type=F32) * grow[:, gs]
        ht_ref[:, gs] = h_prev * cdecay[:, gs] + lax.dot_general(
            bg, x_end[:, gs], (((0,), (0,)), ((), ())), preferred_element_type=F32)
        for pr in range(hpg // 2):
            pair_lo = g * gw + pr * LANES
            xpair = xdt_b[:, pair_lo:pair_lo + LANES]
            halves = []
            for hh in range(2):
                h = g * hpg + 2 * pr + hh
                decay = jnp.where(causal, jnp.exp(acum[:, DT_COL + h:DT_COL + h + 1]
                                                  - acum_t[DT_COL + h:DT_COL + h + 1, :]), 0.0)
                halves.append(jnp.dot((scores * decay).astype(BF16), xpair, preferred_element_type=F32))
            y_parts.append(jnp.where(low, halves[0], halves[1]) + y_off[:, pr * LANES:(pr + 1) * LANES])
    y = jnp.concatenate(y_parts, axis=1) + dx_ref[...] * xs
    gated = y * _silu(z_ref[0].astype(F32))
    y_ref[0] = _rmsnorm_rows(gated, nw_ref[...]).astype(y_ref.dtype)

    @pl.when(c == nc - 1)
    def _():
        st_ref[0] = ht_ref[...]


def _ssm_params(dt_bias, A_log, D_skip):
    pad = (DT_COL, SMALL_W - DT_COL - SSM_HEADS)
    dtb = jnp.pad(dt_bias.astype(F32), pad).reshape(1, SMALL_W)
    alog = jnp.pad(A_log.astype(F32), pad).reshape(1, SMALL_W)
    dx = jnp.repeat(D_skip.astype(F32), SSM_HEAD_DIM).reshape(1, SSM_WIDTH)
    return dtb, alog, dx


def _ssd_prompt(xbc, small, z, conv_w, conv_b, dtb, alog, dx, e, ssm_norm_w):
    B, L, _ = xbc.shape
    Q = SSD_CHUNK
    full = lambda a: pl.BlockSpec(a.shape, lambda b, c: (0,) * a.ndim)
    row_spec = lambda w: pl.BlockSpec((1, Q, w), lambda b, c: (b, c, 0))
    cb = conv_b.reshape(1, CONV_DIM)
    nw = ssm_norm_w.reshape(1, SSM_WIDTH)
    return pl.pallas_call(
        _ssd_kernel,
        grid=(B, L // Q),
        in_specs=[row_spec(CONV_DIM), row_spec(SMALL_W), row_spec(SSM_WIDTH),
                  full(conv_w), full(cb), full(dtb), full(alog), full(e), full(dx), full(nw)],
        out_specs=[row_spec(SSM_WIDTH), pl.BlockSpec((1, D_STATE, SSM_WIDTH), lambda b, c: (b, 0, 0))],
        out_shape=[jax.ShapeDtypeStruct((B, L, SSM_WIDTH), BF16),
                   jax.ShapeDtypeStruct((B, D_STATE, SSM_WIDTH), F32)],
        scratch_shapes=[pltpu.VMEM((Q + SUBLANES, CONV_DIM), F32), pltpu.VMEM((D_STATE, SSM_WIDTH), F32)],
        compiler_params=_cparams("arbitrary", "arbitrary"),
        name="ssd_prompt",
    )(xbc, small, z, conv_w, cb, dtb, alog, e, dx, nw)


DECODE_PAGES = 16


def _decode_attn_kernel(pt_ref, qt_ref, knt_ref, vnt_ref, sm_ref, bf_ref, *refs, pps):
    k_refs = refs[0:pps]
    v_refs = refs[pps:2 * pps]
    lf_refs = refs[2 * pps:3 * pps]
    o_ref, lfo_ref, qrep_ref, m_ref, l_ref, acc_ref, carry_ref, bias_ref = refs[3 * pps:]
    blk = pl.program_id(1)
    nblk = pl.num_programs(1)
    H = ATT_HEADS
    page = k_refs[0].shape[3]
    lane_row = lax.broadcasted_iota(I32, (1, page), 1)

    @pl.when(blk == 0)
    def _():
        lf_new = _log_sigmoid(sm_ref[0] + bf_ref[...])
        lfo_ref[0] = lf_new[:, :H]
        diag = (lax.broadcasted_iota(I32, (H, SMALL_W), 0) == lax.broadcasted_iota(I32, (H, SMALL_W), 1))
        lf_col = jnp.sum(jnp.where(diag, jnp.broadcast_to(lf_new, (H, SMALL_W)), 0.0), axis=1, keepdims=True)
        carry_ref[...] = jnp.broadcast_to(lf_col, (H, page))
        qt = qt_ref[0]
        knt = knt_ref[0].astype(BF16).astype(F32)
        vnt = vnt_ref[0].astype(BF16).astype(F32)
        s_row = jnp.sum(qt * knt, axis=0, keepdims=True)
        lane = lax.broadcasted_iota(I32, (HEAD_DIM, page), 1)
        for h in range(H):
            qrep_ref[h] = jnp.broadcast_to(qt[:, h:h + 1], (HEAD_DIM, page))
            m_ref[h:h + 1, :] = jnp.where(lane_row == 0, jnp.broadcast_to(s_row[:, h:h + 1], (1, page)), NEG_BIG)
            acc_ref[h] = jnp.where(lane == 0, jnp.broadcast_to(vnt[:, h:h + 1], (HEAD_DIM, page)), 0.0)
        l_ref[...] = jnp.broadcast_to(jnp.where(lane_row == 0, 1.0, 0.0), (H, page))

    later = (lax.broadcasted_iota(I32, (page, page), 0) > lax.broadcasted_iota(I32, (page, page), 1)).astype(F32)
    carry = carry_ref[...]
    lf_all = jnp.concatenate([lf_refs[j][0] for j in range(pps)], axis=0)
    suffix = jnp.dot(lf_all, later, precision=HIGHEST, preferred_element_type=F32)
    for j in range(pps):
        bias_ref[j] = (suffix[j * H:(j + 1) * H, :] + carry) * LOG2E
        carry = carry + jnp.sum(lf_refs[j][0], axis=1, keepdims=True)
    carry_ref[...] = carry

    def head_body(h, _):
        q3 = qrep_ref[h]
        row = pl.ds(h, 1)
        m = m_ref[row, :]
        l = l_ref[row, :]
        acc = acc_ref[h]
        for j in range(pps):
            s = jnp.sum(q3 * k_refs[j][0, h], axis=0, keepdims=True) + bias_ref[j, row, :]
            m_new = jnp.maximum(m, s)
            alpha = jnp.exp2(m - m_new)
            p = jnp.exp2(s - m_new)
            l = alpha * l + p
            acc = alpha * acc + p * v_refs[j][0, h]
            m = m_new
        m_ref[row, :] = m
        l_ref[row, :] = l
        acc_ref[h] = acc
        return 0

    lax.fori_loop(0, H, head_body, 0)

    @pl.when(blk == nblk - 1)
    def _():
        m_all = m_ref[...]
        w = jnp.exp2(m_all - jnp.max(m_all, axis=1, keepdims=True))
        den = jnp.sum(l_ref[...] * w, axis=1, keepdims=True)
        heads = []
        for h in range(H):
            w_rows = jnp.broadcast_to(w[h:h + 1, :], (SUBLANES, page))
            num = lax.dot_general(w_rows, acc_ref[h], (((1,), (1,)), ((), ())), precision=HIGHEST,
                                  preferred_element_type=F32)
            heads.append(num[0:1, :] / den[h:h + 1, :])
        o_ref[0] = jnp.concatenate(heads, axis=1)


def _decode_attention(page_table, q_t, kn_t, vn_t, small, b_f, cache_k_t, cache_v_t, cache_lf_t):
    Bd = q_t.shape[0]
    n_pages = page_table.shape[1]
    page = cache_k_t.shape[3]
    pps = math.gcd(DECODE_PAGES, n_pages)
    pt_flat = page_table.reshape(-1)
    bf2 = jnp.pad(b_f.reshape(1, ATT_HEADS), ((0, 0), (0, SMALL_W - ATT_HEADS)))

    def page_spec(shape, j):
        def imap(b, blk, pt):
            return (pt[b * n_pages + (n_pages - 1 - (blk * pps + j))],) + (0,) * (len(shape) - 1)
        return pl.BlockSpec(shape, imap)

    col_spec = pl.BlockSpec((1, HEAD_DIM, ATT_HEADS), lambda b, blk, pt: (b, 0, 0))
    grid_spec = pltpu.PrefetchScalarGridSpec(
        num_scalar_prefetch=1,
        grid=(Bd, n_pages // pps),
        in_specs=([col_spec, col_spec, col_spec,
                   pl.BlockSpec((1, 1, SMALL_W), lambda b, blk, pt: (b, 0, 0)),
                   pl.BlockSpec((1, SMALL_W), lambda b, blk, pt: (0, 0))]
                  + [page_spec((1, ATT_HEADS, HEAD_DIM, page), j) for j in range(pps)]
                  + [page_spec((1, ATT_HEADS, HEAD_DIM, page), j) for j in range(pps)]
                  + [page_spec((1, ATT_HEADS, page), j) for j in range(pps)]),
        out_specs=[pl.BlockSpec((1, 1, ATT_WIDTH), lambda b, blk, pt: (b, 0, 0)),
                   pl.BlockSpec((1, 1, ATT_HEADS), lambda b, blk, pt: (b, 0, 0))],
        scratch_shapes=[pltpu.VMEM((ATT_HEADS, HEAD_DIM, page), F32), pltpu.VMEM((ATT_HEADS, page), F32),
                        pltpu.VMEM((ATT_HEADS, page), F32), pltpu.VMEM((ATT_HEADS, HEAD_DIM, page), F32),
                        pltpu.VMEM((ATT_HEADS, page), F32), pltpu.VMEM((pps, ATT_HEADS, page), F32)],
    )
    return pl.pallas_call(
        functools.partial(_decode_attn_kernel, pps=pps),
        grid_spec=grid_spec,
        out_shape=[jax.ShapeDtypeStruct((Bd, 1, ATT_WIDTH), F32),
                   jax.ShapeDtypeStruct((Bd, 1, ATT_HEADS), F32)],
        compiler_params=_cparams("arbitrary", "arbitrary"),
        name="fox_decode",
    )(pt_flat, q_t, kn_t, vn_t, small, bf2, *([cache_k_t] * pps), *([cache_v_t] * pps), *([cache_lf_t] * pps))


SSM_STEP_BATCH = 4


def _ssm_step_kernel(xbc_ref, sc_ref, sm_ref, z_ref, h0_ref, cw_ref, cb_ref, dtb_ref, alog_ref, e_ref,
                     dx_ref, nw_ref, y_ref, st_ref):
    for i in range(xbc_ref.shape[0]):
        _ssm_step_one(i, xbc_ref, sc_ref, sm_ref, z_ref, h0_ref, cw_ref, cb_ref, dtb_ref, alog_ref, e_ref,
                      dx_ref, nw_ref, y_ref, st_ref)


def _ssm_step_one(i, xbc_ref, sc_ref, sm_ref, z_ref, h0_ref, cw_ref, cb_ref, dtb_ref, alog_ref, e_ref,
                  dx_ref, nw_ref, y_ref, st_ref):
    H = SSM_HEADS
    rows = [xbc_ref[i]] + [sc_ref[i, CONV_WIDTH - 1 - j:CONV_WIDTH - j, :] for j in range(1, CONV_WIDTH)]
    xc = _conv_silu_rows(rows, cw_ref, cb_ref)
    xs = xc[:, :SSM_WIDTH]
    e = e_ref[...]
    dt = _softplus(sm_ref[i] + dtb_ref[...])
    da = jnp.exp(dt * (-jnp.exp(alog_ref[...])))
    both = _expand_heads(jnp.concatenate([jnp.broadcast_to(dt, (SUBLANES, SMALL_W)),
                                          jnp.broadcast_to(da, (SUBLANES, SMALL_W))], axis=0), e)
    dt_x = both[0:1, :]
    da_x = both[SUBLANES:SUBLANES + 1, :]
    xdt = xs * dt_x

    sub = lax.broadcasted_iota(I32, (H, SSM_WIDTH), 0)
    own = sub == lax.broadcasted_iota(I32, (H, SSM_WIDTH), 1) // SSM_HEAD_DIM

    def masked_parts(v):
        m = jnp.where(own, jnp.broadcast_to(v, (H, SSM_WIDTH)), 0.0)
        hi = m.astype(BF16)
        return hi, (m - hi.astype(F32)).astype(BF16)

    da_hi, da_lo = masked_parts(da_x)
    x_hi, x_lo = masked_parts(xdt)
    lhs = jnp.concatenate([da_hi, da_lo, x_hi, x_lo], axis=0)
    hpg = H // SSM_GROUPS
    grp = lax.broadcasted_iota(I32, (H, D_STATE), 0) // hpg
    b_rows = jnp.zeros((H, D_STATE), F32)
    c_rows = jnp.zeros((H, D_STATE), F32)
    for g in range(SSM_GROUPS):
        bg = xc[:, SSM_WIDTH + g * D_STATE:SSM_WIDTH + (g + 1) * D_STATE]
        cg = xc[:, SSM_WIDTH + (SSM_GROUPS + g) * D_STATE:SSM_WIDTH + (SSM_GROUPS + g + 1) * D_STATE]
        b_rows = jnp.where(grp == g, jnp.broadcast_to(bg, (H, D_STATE)), b_rows)
        c_rows = jnp.where(grp == g, jnp.broadcast_to(cg, (H, D_STATE)), c_rows)
    ones = jnp.ones((2 * H, D_STATE), BF16)
    zeros = jnp.zeros((2 * H, D_STATE), BF16)
    b_bf = b_rows.astype(BF16)
    rhs = jnp.concatenate([jnp.concatenate([ones, zeros], axis=1),
                           jnp.concatenate([zeros, jnp.concatenate([b_bf, b_bf], axis=0)], axis=1)], axis=0)
    mix = lax.dot_general(lhs, rhs, (((0,), (0,)), ((), ())), preferred_element_type=F32)
    h0 = h0_ref[i].reshape(SSM_WIDTH, D_STATE)
    h_new = mix[:, :D_STATE] * h0 + mix[:, D_STATE:]
    st_ref[i] = h_new.reshape(H, SSM_HEAD_DIM, D_STATE)
    y_t = lax.dot_general(c_rows.astype(BF16), h_new.astype(BF16), (((1,), (1,)), ((), ())),
                          preferred_element_type=F32)
    y = jnp.sum(jnp.where(own, y_t, 0.0), axis=0, keepdims=True) + dx_ref[...] * xs
    gated = y * _silu(z_ref[i].astype(F32))
    y_ref[i] = _rmsnorm_rows(gated, nw_ref[...]).astype(y_ref.dtype)


def _ssm_step(xbc, state_conv, small, z, state_ssm, conv_w, conv_b, dtb, alog, dx, e, ssm_norm_w):
    Bd = xbc.shape[0]
    bb = math.gcd(SSM_STEP_BATCH, Bd)
    full = lambda a: pl.BlockSpec(a.shape, lambda b: (0,) * a.ndim)
    row = lambda w: pl.BlockSpec((bb, 1, w), lambda b: (b, 0, 0))
    st_spec = pl.BlockSpec((bb, SSM_HEADS, SSM_HEAD_DIM, D_STATE), lambda b: (b, 0, 0, 0))
    cb = conv_b.reshape(1, CONV_DIM)
    nw = ssm_norm_w.reshape(1, SSM_WIDTH)
    return pl.pallas_call(
        _ssm_step_kernel,
        grid=(Bd // bb,),
        in_specs=[row(CONV_DIM), pl.BlockSpec((bb, CONV_WIDTH - 1, CONV_DIM), lambda b: (b, 0, 0)),
                  row(SMALL_W), row(SSM_WIDTH), st_spec,
                  full(conv_w), full(cb), full(dtb), full(alog), full(e), full(dx), full(nw)],
        out_specs=[row(SSM_WIDTH), st_spec],
        out_shape=[jax.ShapeDtypeStruct((Bd, 1, SSM_WIDTH), BF16),
                   jax.ShapeDtypeStruct(state_ssm.shape, F32)],
        compiler_params=_cparams("arbitrary"),
        name="ssm_step",
    )(xbc, state_conv, small, z, state_ssm, conv_w, cb, dtb, alog, e, dx, nw)


def _outproj_kernel(att_ref, ssm_ref, x_ref, g1_ref, sh2_ref, sc2_ref, n2_ref, wa_ref, wsm_ref, rw_ref, rb_ref,
                    cin_ref, h2_prev_ref, x1_ref, h2_ref, rt_ref, tg_ref, cout_ref, cnt_ref):
    del h2_prev_ref

    @pl.when((pl.program_id(0) == 0) & (pl.program_id(1) == 0))
    def _():
        cnt_ref[...] = cin_ref[...]

    y = (jnp.dot(att_ref[0], wa_ref[...], preferred_element_type=F32)
         + jnp.dot(ssm_ref[0], wsm_ref[...], preferred_element_type=F32))
    x1 = x_ref[0] + g1_ref[0] * y
    x1_ref[0] = x1
    h2 = _rmsnorm_rows(x1, n2_ref[...]) * (1.0 + sc2_ref[0]) + sh2_ref[0]
    h2_ref[...] = _pack_bf16_pairs(h2)
    logits = _dot_split(h2, h2.astype(BF16), rw_ref) + rb_ref[...]
    tm = logits.shape[0]
    lane = lax.broadcasted_iota(I32, (tm, LANES), 1).astype(F32)
    cur = logits
    idxs = []
    val_tile = jnp.full((tm, LANES), NEG_BIG, F32)
    chosen = jnp.zeros((tm, LANES), F32)
    for k in range(TOP_K):
        m = jnp.max(cur, axis=1, keepdims=True)
        idx = jnp.min(jnp.where(cur == m, lane, float(LANES)), axis=1, keepdims=True)
        idxs.append(idx)
        val_tile = jnp.where(lane == float(k), m, val_tile)
        hit = lane == idx
        chosen = jnp.where(hit, 1.0, chosen)
        cur = jnp.where(hit, 2.0 * NEG_BIG, cur)
    top = jnp.max(val_tile, axis=1, keepdims=True)
    ex = jnp.exp2((val_tile - top) * LOG2E)
    tg_ref[0] = ex / jnp.sum(ex, axis=1, keepdims=True)

    before = (lax.broadcasted_iota(I32, (tm, tm), 1) < lax.broadcasted_iota(I32, (tm, tm), 0)).astype(BF16)
    rank = jnp.dot(before, chosen.astype(BF16), preferred_element_type=F32) + cnt_ref[0:1, :]
    cnt_ref[0:1, :] = cnt_ref[0:1, :] + jnp.sum(chosen, axis=0, keepdims=True)
    cout_ref[...] = cnt_ref[...]
    route = jnp.zeros((tm, LANES), F32)
    for k in range(TOP_K):
        rank_k = jnp.sum(jnp.where(lane == idxs[k], rank, 0.0), axis=1, keepdims=True)
        route = jnp.where(lane == float(k), idxs[k], route)
        route = jnp.where(lane == float(TOP_K + k), rank_k, route)
    rt_ref[0] = route.astype(I32)


def _out_proj(att, ssm, x, g1, sh2, sc2, norm2_w, wa, wsm, rw, rb, counts_in, h2_all, row0, tm):
    B, L, _ = x.shape
    steps = L // tm
    assert row0 % tm == 0
    per_row = g1.shape[1] != 1
    mod_spec = (pl.BlockSpec((1, tm, D_MODEL), lambda b, i: (b, i, 0)) if per_row
                else pl.BlockSpec((1, 1, D_MODEL), lambda b, i: (b, 0, 0)))
    full = lambda a: pl.BlockSpec(a.shape, lambda b, i: (0,) * a.ndim)
    row = lambda w: pl.BlockSpec((1, tm, w), lambda b, i: (b, i, 0))
    n2 = norm2_w.reshape(1, D_MODEL)
    cnt_spec = pl.BlockSpec((SUBLANES, LANES), lambda b, i: (0, 0))
    return pl.pallas_call(
        _outproj_kernel,
        grid=(B, L // tm),
        in_specs=[row(ATT_WIDTH), row(SSM_WIDTH), row(D_MODEL), mod_spec, mod_spec, mod_spec,
                  full(n2), full(wa), full(wsm), full(rw), full(rb), cnt_spec, pl.BlockSpec(memory_space=pl.ANY)],
        out_specs=[row(D_MODEL), pl.BlockSpec((tm, D_MODEL // 2), lambda b, i: (row0 // tm + b * steps + i, 0)),
                   row(LANES), row(LANES), cnt_spec],
        out_shape=[jax.ShapeDtypeStruct((B, L, D_MODEL), F32), jax.ShapeDtypeStruct(h2_all.shape, h2_all.dtype),
                   jax.ShapeDtypeStruct((B, L, LANES), I32), jax.ShapeDtypeStruct((B, L, LANES), F32),
                   jax.ShapeDtypeStruct((SUBLANES, LANES), F32)],
        input_output_aliases={12: 1},
        scratch_shapes=[pltpu.VMEM((SUBLANES, LANES), F32)],
        compiler_params=_cparams("arbitrary", "arbitrary"),
        name="out_proj_route",
    )(att, ssm, x, g1, sh2, sc2, n2, wa, wsm, rw, rb, counts_in, h2_all)


def _moe_block_tables(counts, n_blocks):
    tb = MOE_ROWS
    nb = (counts + tb - 1) // tb
    cum = jnp.cumsum(nb)
    blk_start = cum - nb
    b = jnp.arange(n_blocks, dtype=I32)
    block_expert = jnp.minimum(jnp.sum((cum[None, :] <= b[:, None]).astype(I32), axis=1), N_EXPERTS - 1)
    rows_left = counts[block_expert] - (b - blk_start[block_expert]) * tb
    block_rows = jnp.where(b < cum[-1], jnp.clip(rows_left, 0, tb), 0).astype(I32)
    prev = jnp.concatenate([jnp.full((1,), -1, I32), block_expert[:-1]])
    block_first = ((block_expert != prev) & (block_rows > 0)).astype(I32)
    return (blk_start * tb).astype(I32), block_expert.astype(I32), block_first, block_rows


SC_WORKERS = 32
SC_SCATTER_WINDOW = 48
SC_WINDOW = 64


def _sc_scatter_rows(src, slots, n_rows):
    T, d = src.shape
    win = SC_SCATTER_WINDOW
    steps = T // win * TOP_K
    assert T % win == 0 and steps % SC_WORKERS == 0
    idx = slots.reshape(T // win, win, TOP_K).transpose(0, 2, 1).reshape(steps, win)
    idx_rows = jnp.pad(idx, ((0, 0), (0, LANES - win)))
    mesh = plsc.VectorSubcoreMesh(core_axis_name="c", subcore_axis_name="s")

    @functools.partial(pl.kernel, out_type=jax.ShapeDtypeStruct((n_rows, d), src.dtype), mesh=mesh, name="moe_scatter")
    def scatter(x_hbm, i_hbm, o_hbm):
        def body(x_vmem, i_vmem):
            pltpu.sync_copy(x_vmem, o_hbm.at[i_vmem.at[0, pl.ds(0, win)]])

        pltpu.emit_pipeline(body, grid=(steps,),
                            in_specs=[pl.BlockSpec((win, d), lambda i: (i // TOP_K, 0)),
                                      pl.BlockSpec((1, LANES), lambda i: (i, 0))],
                            out_specs=[],
                            core_axis_name=("c", "s"), dimension_semantics=(pltpu.PARALLEL,))(x_hbm, i_hbm)

    return scatter(src, idx_rows)


def _moe_kernel(be_ref, first_ref, rows_ref, x_ref, wgu_ref, bgu_ref, wd_ref, bd_ref, o_ref, wgu_s, wd_s):
    i = pl.program_id(0)

    @pl.when(first_ref[i] == 1)
    def _():
        wgu_s[...] = wgu_ref[0].astype(BF16)
        wd_s[...] = wd_ref[0].astype(BF16)

    @pl.when(rows_ref[i] > 0)
    def _():
        live = lax.broadcasted_iota(I32, x_ref.shape, 0) < rows_ref[i]
        x = _unpack_bf16_pairs(jnp.where(live, x_ref[...], jnp.uint32(0))).astype(BF16)
        gu = jnp.dot(x, wgu_s[...], preferred_element_type=F32) + bgu_ref[0]
        g = jnp.minimum(gu[:, :D_FF], SWIGLU_LIMIT)
        u = jnp.clip(gu[:, D_FF:], -SWIGLU_LIMIT, SWIGLU_LIMIT)
        act = (u + 1.0) * (g * jax.nn.sigmoid(SWIGLU_ALPHA * g))
        o_ref[...] = _pack_bf16_pairs(jnp.dot(act.astype(BF16), wd_s[...], preferred_element_type=F32) + bd_ref[0])

    @pl.when(rows_ref[i] == 0)
    def _():
        o_ref[...] = jnp.zeros_like(o_ref)


def _moe_blocks(block_expert, block_first, block_rows, x_sorted, w_gate_up, b_gate_up, w_down, b_down):
    n_rows = x_sorted.shape[0]
    tb = MOE_ROWS
    grid_spec = pltpu.PrefetchScalarGridSpec(
        num_scalar_prefetch=3,
        grid=(n_rows // tb,),
        in_specs=[pl.BlockSpec((tb, D_MODEL // 2), lambda i, be, bf, br: (i, 0)),
                  pl.BlockSpec((1, D_MODEL, 2 * D_FF), lambda i, be, bf, br: (be[i], 0, 0)),
                  pl.BlockSpec((1, 1, 2 * D_FF), lambda i, be, bf, br: (be[i], 0, 0)),
                  pl.BlockSpec((1, D_FF, D_MODEL), lambda i, be, bf, br: (be[i], 0, 0)),
                  pl.BlockSpec((1, 1, D_MODEL), lambda i, be, bf, br: (be[i], 0, 0))],
        out_specs=pl.BlockSpec((tb, D_MODEL // 2), lambda i, be, bf, br: (i, 0)),
        scratch_shapes=[pltpu.VMEM((D_MODEL, 2 * D_FF), BF16), pltpu.VMEM((D_FF, D_MODEL), BF16)],
    )
    return pl.pallas_call(
        _moe_kernel,
        grid_spec=grid_spec,
        out_shape=jax.ShapeDtypeStruct((n_rows, D_MODEL // 2), jnp.uint32),
        compiler_params=_cparams("arbitrary"),
        name="moe_experts",
    )(block_expert, block_first, block_rows, x_sorted, w_gate_up,
      b_gate_up.reshape(N_EXPERTS, 1, 2 * D_FF), w_down, b_down.reshape(N_EXPERTS, 1, D_MODEL))


def _sc_gather_rows(table, idx):
    n = idx.shape[0]
    d = table.shape[1]
    win = SC_WINDOW
    idx_rows = jnp.pad(idx.reshape(n // win, win), ((0, 0), (0, LANES - win)))
    mesh = plsc.VectorSubcoreMesh(core_axis_name="c", subcore_axis_name="s")

    @functools.partial(pl.kernel, out_type=jax.ShapeDtypeStruct((n, d), table.dtype), mesh=mesh, name="moe_gather")
    def gather(t_hbm, i_hbm, o_hbm):
        def body(i_vmem, o_vmem):
            pltpu.sync_copy(t_hbm.at[i_vmem.at[0, pl.ds(0, win)]], o_vmem)

        pltpu.emit_pipeline(body, grid=(n // win,),
                            in_specs=[pl.BlockSpec((1, LANES), lambda i: (i, 0))],
                            out_specs=[pl.BlockSpec((win, d), lambda i: (i, 0))],
                            core_axis_name=("c", "s"), dimension_semantics=(pltpu.PARALLEL,))(i_hbm, o_hbm)

    return gather(table, idx_rows)


def _final_kernel(x1_ref, tg_ref, g2_ref, nw_ref, y0_ref, y1_ref, y2_ref, y3_ref, o_ref):
    gates = tg_ref[0]
    moe = gates[:, 0:1] * _unpack_bf16_pairs(y0_ref[...])
    for k, y_ref in enumerate((y1_ref, y2_ref, y3_ref), start=1):
        moe = moe + gates[:, k:k + 1] * _unpack_bf16_pairs(y_ref[...])
    o_ref[0] = _rmsnorm_rows(x1_ref[0] + g2_ref[0] * moe, nw_ref[...])


def _moe_final(x1, gates, g2, final_norm_w, y_rows, row0, tt):
    B, L, _ = x1.shape
    steps = L // tt
    blocks_per_k = B * steps
    base = row0 // tt
    assert row0 % tt == 0 and TOP_K == 4
    per_row = g2.shape[1] != 1
    mod_spec = (pl.BlockSpec((1, tt, D_MODEL), lambda b, i: (b, i, 0)) if per_row
                else pl.BlockSpec((1, 1, D_MODEL), lambda b, i: (b, 0, 0)))
    y_specs = [pl.BlockSpec((tt, D_MODEL // 2), functools.partial(lambda b, i, k: (base + k * blocks_per_k + b * steps + i, 0), k=k))
               for k in range(TOP_K)]
    nw = final_norm_w.reshape(1, D_MODEL)
    return pl.pallas_call(
        _final_kernel,
        grid=(B, steps),
        in_specs=[pl.BlockSpec((1, tt, D_MODEL), lambda b, i: (b, i, 0)),
                  pl.BlockSpec((1, tt, LANES), lambda b, i: (b, i, 0)),
                  mod_spec, pl.BlockSpec((1, D_MODEL), lambda b, i: (0, 0))] + y_specs,
        out_specs=pl.BlockSpec((1, tt, D_MODEL), lambda b, i: (b, i, 0)),
        out_shape=jax.ShapeDtypeStruct((B, L, D_MODEL), F32),
        compiler_params=_cparams("arbitrary", "arbitrary"),
        name="moe_final",
    )(x1, gates, g2, nw, y_rows, y_rows, y_rows, y_rows)


def kernel(x_prompt, x_sample, c_prompt, c_sample, cache_k, cache_v, cache_lf, state_conv, state_ssm, page_table,
           ada_w, ada_b, norm1_w, w_in, b_f, conv_w, conv_b, dt_bias, A_log, D_skip, ssm_norm_w, w_out,
           norm2_w, router_w, router_b, w_gate_up, b_gate_up, w_down, b_down, final_norm_w):
    assert ada_w.shape[0] == 1, "single-layer trunk"
    B, L, D = x_prompt.shape
    Bd = x_sample.shape[0]
    assert x_sample.shape[1] == 1 and L % SSD_CHUNK == 0

    n_c = B + Bd
    rows = -(-n_c // SUBLANES) * SUBLANES
    c_all = jnp.concatenate([c_prompt, c_sample, jnp.zeros((rows - n_c, D), F32)], axis=0)
    mod = _modulation(c_all, ada_w[0], ada_b[0])
    mod_p = [m.reshape(B, 1, D) for m in jnp.split(mod[:B], 6, axis=-1)]
    mod_s = [m.reshape(1, Bd, D) for m in jnp.split(mod[B:n_c], 6, axis=-1)]

    w = w_in[0]
    o_f = 3 * ATT_WIDTH
    o_z = o_f + ATT_HEADS
    o_x = o_z + SSM_WIDTH
    o_dt = o_x + CONV_DIM
    w_small = jnp.concatenate([w[:, o_f:o_z], w[:, o_dt:o_dt + SSM_HEADS],
                               jnp.zeros((D, SMALL_W - ATT_HEADS - SSM_HEADS), F32)], axis=1)
    wts = (w[:, :ATT_WIDTH].astype(BF16), w[:, ATT_WIDTH:2 * ATT_WIDTH].astype(BF16),
           w[:, 2 * ATT_WIDTH:o_f].astype(BF16), w[:, o_z:o_x].astype(BF16), w[:, o_x:o_dt].astype(BF16), _split_weight(w_small))
    wa = w_out[0][:ATT_WIDTH].astype(BF16)
    wsm = w_out[0][ATT_WIDTH:].astype(BF16)
    rw = _split_weight(jnp.pad(router_w[0], ((0, 0), (0, LANES - N_EXPERTS))))
    rb = jnp.pad(router_b[0].reshape(1, N_EXPERTS), ((0, 0), (0, LANES - N_EXPERTS)), constant_values=NEG_BIG)
    dtb, alog, dx = _ssm_params(dt_bias[0], A_log[0], D_skip[0])
    e = jnp.asarray(_head_expander(), BF16)

    tm_p = min(512, L)
    qp, kp, vp, k_p, v_p, z_p, xbc_p, small_p, lf_p = _in_proj(x_prompt, mod_p[0], mod_p[1], norm1_w[0], wts,
                                                              b_f[0], tm_p)
    att_p = _flash_attention(qp, kp, vp, min(1024, L), min(512, L))
    ssm_p, st_p = _ssd_prompt(xbc_p, small_p, z_p, conv_w[0], conv_b[0], dtb, alog, dx, e, ssm_norm_w[0])
    n_tok = B * L + Bd
    zero_counts = jnp.zeros((SUBLANES, LANES), F32)
    h2_all = jnp.zeros((n_tok, D // 2), jnp.uint32)
    x1_p, h2_all, rt_p, tg_p, counts_p = _out_proj(att_p, ssm_p, x_prompt, mod_p[2], mod_p[3], mod_p[4], norm2_w[0],
                                                   wa, wsm, rw, rb, zero_counts, h2_all, 0, min(512, L))

    xs_rows = x_sample.reshape(1, Bd, D)
    qp_s, _, _, k_s, v_s, z_s, xbc_s, small_s, _ = _in_proj(xs_rows, mod_s[0], mod_s[1], norm1_w[0], wts, b_f[0], Bd)
    per_row = lambda a: a.reshape(Bd, 1, a.shape[-1])
    head_cols = lambda a: a.reshape(Bd, ATT_HEADS, HEAD_DIM).transpose(0, 2, 1).astype(F32)
    q_halves = qp_s.astype(F32).reshape(Bd, ATT_HEADS, 2, HEAD_DIM)
    odd_head = (jnp.arange(ATT_HEADS) % 2 == 1)[None, :, None]
    q_s = jnp.where(odd_head, q_halves[:, :, 1, :], q_halves[:, :, 0, :])
    att_s4, lf_s = _decode_attention(
        page_table, head_cols(q_s), head_cols(k_s), head_cols(v_s), per_row(small_s), b_f[0],
        cache_k[0].transpose(0, 2, 3, 1), cache_v[0].transpose(0, 2, 3, 1), cache_lf[0].transpose(0, 2, 1))
    att_s = att_s4.reshape(1, Bd, ATT_WIDTH).astype(BF16)
    ssm_s, st_s = _ssm_step(per_row(xbc_s), state_conv[0], per_row(small_s), per_row(z_s), state_ssm[0],
                            conv_w[0], conv_b[0], dtb, alog, dx, e, ssm_norm_w[0])
    x1_s, h2_all, rt_s, tg_s, counts = _out_proj(att_s, ssm_s.reshape(1, Bd, SSM_WIDTH), xs_rows, mod_s[2], mod_s[3],
                                                 mod_s[4], norm2_w[0], wa, wsm, rw, rb, counts_p, h2_all, B * L, Bd)

    n_blocks = (n_tok * TOP_K + N_EXPERTS * (MOE_ROWS - 1)) // MOE_ROWS
    pad_start, b_exp, b_first, b_rows = _moe_block_tables(counts[0, :N_EXPERTS].astype(I32), n_blocks)

    def token_slots(rt):
        e_idx = rt[..., :TOP_K].reshape(-1, TOP_K)
        first = jnp.sum(jnp.where(e_idx[..., None] == jnp.arange(N_EXPERTS, dtype=I32), pad_start, 0), axis=-1)
        return first + rt[..., TOP_K:2 * TOP_K].reshape(-1, TOP_K)

    slots_p, slots_s = token_slots(rt_p), token_slots(rt_s)
    x_sorted = _sc_scatter_rows(h2_all, jnp.concatenate([slots_p, slots_s], axis=0), n_blocks * MOE_ROWS)
    y_sorted = _moe_blocks(b_exp, b_first, b_rows, x_sorted, w_gate_up[0], b_gate_up[0], w_down[0], b_down[0])

    n_assign = n_tok * TOP_K
    chunk = SC_WINDOW * SC_WORKERS
    n_idx = -(-n_assign // chunk) * chunk
    slots = jnp.concatenate([slots_p.T.reshape(-1), slots_s.T.reshape(-1), jnp.zeros((n_idx - n_assign,), I32)])
    y_rows = _sc_gather_rows(y_sorted, slots)
    y_prompt = _moe_final(x1_p, tg_p, mod_p[5], final_norm_w, y_rows, 0, min(512, L))
    y_sample = _moe_final(x1_s, tg_s, mod_s[5], final_norm_w, y_rows, B * L * TOP_K, Bd).reshape(Bd, 1, D)

    conv_s = jnp.concatenate([state_conv[0][:, 1:], xbc_s.reshape(Bd, 1, CONV_DIM)], axis=1)
    ssm_state_p = st_p.reshape(B, D_STATE, SSM_HEADS, SSM_HEAD_DIM).transpose(0, 2, 3, 1)
    return (y_prompt, y_sample,
            k_p.reshape(1, B, L, ATT_HEADS, HEAD_DIM), v_p.reshape(1, B, L, ATT_HEADS, HEAD_DIM),
            lf_p.reshape(1, B, L, ATT_HEADS), xbc_p[:, L - (CONV_WIDTH - 1):].reshape(1, B, CONV_WIDTH - 1, CONV_DIM),
            ssm_state_p.reshape(1, B, SSM_HEADS, SSM_HEAD_DIM, D_STATE),
            k_s.reshape(1, Bd, 1, ATT_HEADS, HEAD_DIM), v_s.reshape(1, Bd, 1, ATT_HEADS, HEAD_DIM),
            lf_s.reshape(1, Bd, 1, ATT_HEADS), conv_s.reshape(1, Bd, CONV_WIDTH - 1, CONV_DIM),
            st_s.reshape(1, Bd, SSM_HEADS, SSM_HEAD_DIM, D_STATE))
```

```python
import functools
import math

import numpy as np
import jax
import jax.numpy as jnp
from jax import lax
from jax.experimental import pallas as pl
from jax.experimental.pallas import tpu as pltpu
from jax.experimental.pallas import tpu_sc as plsc

F32 = jnp.float32
BF16 = jnp.bfloat16
I32 = jnp.int32
HIGHEST = lax.Precision.HIGHEST

D_MODEL = 1024
ATT_HEADS = 16
HEAD_DIM = 64
ATT_WIDTH = ATT_HEADS * HEAD_DIM
SSM_HEADS = 16
SSM_HEAD_DIM = 64
SSM_WIDTH = SSM_HEADS * SSM_HEAD_DIM
SSM_GROUPS = 2
D_STATE = 128
CONV_WIDTH = 4
CONV_DIM = SSM_WIDTH + 2 * SSM_GROUPS * D_STATE
SSD_CHUNK = 128
N_EXPERTS = 32
TOP_K = 4
D_FF = D_MODEL
SWIGLU_LIMIT = 7.0
SWIGLU_ALPHA = 1.702
NORM_EPS = 1e-5

LANES = 128
SUBLANES = 8
SMALL_W = LANES
DT_COL = ATT_HEADS
NEG_BIG = -1e30
LOG2E = math.log2(math.e)
VMEM_LIMIT = 48 * 1024 * 1024
INPROJ_VMEM_LIMIT = 58 * 1024 * 1024
MOE_ROWS = 512


def _cparams(*sem):
    return pltpu.CompilerParams(dimension_semantics=sem, vmem_limit_bytes=VMEM_LIMIT)


def _silu(x):
    return x * jax.nn.sigmoid(x)


def _softplus(x):
    return jnp.maximum(x, 0.0) + jnp.log(1.0 + jnp.exp(-jnp.abs(x)))


def _log_sigmoid(x):
    return -_softplus(-x)


def _rmsnorm_rows(x, w):
    var = jnp.mean(x * x, axis=-1, keepdims=True)
    return x * lax.rsqrt(var + NORM_EPS) * w


def _split3_bf16(x):
    hi = x.astype(BF16)
    r = x - hi.astype(F32)
    mid = r.astype(BF16)
    lo = (r - mid.astype(F32)).astype(BF16)
    return hi, mid, lo


def _pack_bf16_pairs(x):
    half = x.shape[1] // 2
    lo = pltpu.bitcast(x[:, :half].astype(BF16).astype(F32), jnp.uint32)
    hi = pltpu.bitcast(x[:, half:].astype(BF16).astype(F32), jnp.uint32)
    return hi | lax.shift_right_logical(lo, jnp.uint32(16))


def _unpack_bf16_pairs(w):
    lo = pltpu.bitcast(lax.shift_left(w, jnp.uint32(16)), F32)
    hi = pltpu.bitcast(w & jnp.uint32(0xFFFF0000), F32)
    return jnp.concatenate([lo, hi], axis=1)


def _split_weight(w):
    hi = w.astype(BF16)
    lo = (w - hi.astype(F32)).astype(BF16)
    return jnp.concatenate([hi, lo], axis=1)


def _dot_split(x, x_hi, w_ref):
    x_lo = (x - x_hi.astype(F32)).astype(BF16)
    both = jnp.dot(x_hi, w_ref[...], preferred_element_type=F32)
    return (both[:, :LANES] + both[:, LANES:]
            + jnp.dot(x_lo, w_ref[:, :LANES], preferred_element_type=F32))


def _mod_kernel(c_ref, w_ref, b_ref, o_ref):
    s = _silu(c_ref[...]).astype(BF16)
    o_ref[...] = jnp.dot(s, w_ref[...].astype(BF16), preferred_element_type=F32) + b_ref[...]


def _modulation(c_all, ada_w, ada_b):
    rows = c_all.shape[0]
    n_out = ada_w.shape[1]
    tn = D_MODEL
    return pl.pallas_call(
        _mod_kernel,
        grid=(n_out // tn,),
        in_specs=[pl.BlockSpec((rows, D_MODEL), lambda j: (0, 0)),
                  pl.BlockSpec((D_MODEL, tn), lambda j: (0, j)),
                  pl.BlockSpec((1, tn), lambda j: (0, j))],
        out_specs=pl.BlockSpec((rows, tn), lambda j: (0, j)),
        out_shape=jax.ShapeDtypeStruct((rows, n_out), F32),
        compiler_params=_cparams("arbitrary"),
        name="adaln_mod",
    )(c_all, ada_w, ada_b.reshape(1, n_out))


def _free_half(h):
    return h * LANES + (HEAD_DIM if h % 2 == 0 else 0)


def _bias_layout():
    wide = ATT_HEADS * LANES
    sel = np.zeros((SMALL_W, wide), np.float32)
    rows = np.zeros((5, wide), np.float32)
    for h in range(ATT_HEADS):
        base = _free_half(h)
        rows[4, base] = 1.0
        for part in range(3):
            sel[part * ATT_HEADS + h, base + part] = 1.0
            sel[part * ATT_HEADS + h, base + 3 + part] = -1.0
            rows[0, base + part] = 1.0
            rows[3, base + part] = 1.0
            rows[1, base + 3 + part] = 1.0
            rows[2, base + 3 + part] = 1.0
    return sel, rows


def _inproj_kernel(x_ref, sh_ref, sc_ref, nw_ref, wq_ref, wk_ref, wv_ref, wz_ref, wx_ref, ws_ref, bf_ref,
                   sel_ref, rows_ref, qp_ref, kp_ref, vp_ref, k_ref, v_ref, z_ref, xbc_ref, sm_ref, lf_ref,
                   carry_ref):
    tm = x_ref.shape[1]

    @pl.when(pl.program_id(1) == 0)
    def _():
        carry_ref[...] = jnp.zeros_like(carry_ref)

    h = _rmsnorm_rows(x_ref[0], nw_ref[...]) * (1.0 + sc_ref[0]) + sh_ref[0]
    hb = h.astype(BF16)
    qb = (jnp.dot(hb, wq_ref[...], preferred_element_type=F32) * (HEAD_DIM ** -0.5 * LOG2E)).astype(BF16)
    k = jnp.dot(hb, wk_ref[...], preferred_element_type=F32)
    k_ref[0] = k
    kb = k.astype(BF16)
    v = jnp.dot(hb, wv_ref[...], preferred_element_type=F32)
    v_ref[0] = v
    vb = v.astype(BF16)
    z_ref[0] = jnp.dot(hb, wz_ref[...], preferred_element_type=F32).astype(BF16)
    xbc_ref[0] = jnp.dot(hb, wx_ref[...], preferred_element_type=F32)
    sm = _dot_split(h, hb, ws_ref)
    sm_ref[0] = sm

    lf = _log_sigmoid(sm + bf_ref[...])
    lf_ref[0] = lf[:, :ATT_HEADS]
    tri = (lax.broadcasted_iota(I32, (tm, tm), 1) <= lax.broadcasted_iota(I32, (tm, tm), 0)).astype(BF16)
    sums = jnp.dot(tri, jnp.concatenate(_split3_bf16(lf), axis=1), preferred_element_type=F32)
    fcum = sums[:, :LANES] + sums[:, LANES:2 * LANES] + sums[:, 2 * LANES:] + carry_ref[0:1, :]
    carry_ref[0:1, :] = fcum[tm - 1:tm, :]
    hi, mid, lo = (part.astype(F32) for part in _split3_bf16(fcum * LOG2E))
    lane = lax.broadcasted_iota(I32, (tm, LANES), 1)
    packed = jnp.where(lane < ATT_HEADS, hi,
                       jnp.where(lane < 2 * ATT_HEADS, pltpu.roll(mid, ATT_HEADS, 1),
                                 jnp.where(lane < 3 * ATT_HEADS, pltpu.roll(lo, 2 * ATT_HEADS, 1), 0.0)))
    spread = jnp.dot(packed.astype(BF16), sel_ref[...], preferred_element_type=F32)
    low = lane < HEAD_DIM
    for pair in range(ATT_HEADS // 2):
        ps = slice(pair * LANES, (pair + 1) * LANES)
        for hh in range(2):
            hd = 2 * pair + hh
            keep = low if hh == 0 else jnp.logical_not(low)
            sl = slice(hd * LANES, (hd + 1) * LANES)
            part = spread[:, sl]
            aug_q = (part * rows_ref[0:1, sl] + rows_ref[2:3, sl]).astype(BF16)
            aug_k = (part * rows_ref[1:2, sl] + rows_ref[3:4, sl]).astype(BF16)
            ones_lane = jnp.broadcast_to(rows_ref[4:5, sl], (tm, LANES)).astype(BF16)
            qp_ref[0, :, sl] = jnp.where(keep, qb[:, ps], aug_q)
            kp_ref[0, :, sl] = jnp.where(keep, kb[:, ps], aug_k)
            vp_ref[0, :, sl] = jnp.where(keep, vb[:, ps], ones_lane)


def _in_proj(x, sh, sc, norm_w, wts, b_f, tm):
    B, L, _ = x.shape
    per_row = sh.shape[1] != 1
    mod_spec = (pl.BlockSpec((1, tm, D_MODEL), lambda b, i: (b, i, 0)) if per_row
                else pl.BlockSpec((1, 1, D_MODEL), lambda b, i: (b, 0, 0)))
    wq, wk, wv, wz, wx, ws = wts
    sel, rows = _bias_layout()
    sel, rows = jnp.asarray(sel, BF16), jnp.asarray(rows)
    bf2 = jnp.pad(b_f.reshape(1, ATT_HEADS), ((0, 0), (0, SMALL_W - ATT_HEADS)))
    wide = ATT_HEADS * LANES

    def wspec(w):
        return pl.BlockSpec(w.shape, lambda b, i: (0, 0), pipeline_mode=pl.Buffered(1))

    def ospec(width):
        return pl.BlockSpec((1, tm, width), lambda b, i: (b, i, 0))

    def oshape(width, dt):
        return jax.ShapeDtypeStruct((B, L, width), dt)

    return pl.pallas_call(
        _inproj_kernel,
        grid=(B, L // tm),
        in_specs=[pl.BlockSpec((1, tm, D_MODEL), lambda b, i: (b, i, 0)), mod_spec, mod_spec,
                  pl.BlockSpec((1, D_MODEL), lambda b, i: (0, 0)),
                  wspec(wq), wspec(wk), wspec(wv), wspec(wz), wspec(wx), wspec(ws),
                  wspec(bf2), wspec(sel), wspec(rows)],
        out_specs=[ospec(wide), ospec(wide), ospec(wide), ospec(ATT_WIDTH), ospec(ATT_WIDTH),
                   ospec(SSM_WIDTH), ospec(CONV_DIM), ospec(SMALL_W), ospec(ATT_HEADS)],
        out_shape=[oshape(wide, BF16), oshape(wide, BF16), oshape(wide, BF16),
                   oshape(ATT_WIDTH, F32), oshape(ATT_WIDTH, F32),
                   oshape(SSM_WIDTH, BF16), oshape(CONV_DIM, F32), oshape(SMALL_W, F32), oshape(ATT_HEADS, F32)],
        scratch_shapes=[pltpu.VMEM((SUBLANES, LANES), F32)],
        compiler_params=pltpu.CompilerParams(dimension_semantics=("arbitrary", "arbitrary"),
                                             vmem_limit_bytes=INPROJ_VMEM_LIMIT),
        name="in_proj",
    )(x, sh, sc, norm_w.reshape(1, D_MODEL), wq, wk, wv, wz, wx, ws, bf2, sel, rows)


FLASH_HEADS = 8


def _flash_kernel(qi_ref, ki_ref, qp_ref, kp_ref, vp_ref, o_ref, m_ref, acc_ref, *, tq, tk):
    t = pl.program_id(2)
    qi = qi_ref[t]
    ki = ki_ref[t]
    last = ((qi + 1) * tq - 1) // tk

    @pl.when(ki == 0)
    def _():
        m_ref[...] = jnp.full_like(m_ref, NEG_BIG)
        acc_ref[...] = jnp.zeros_like(acc_ref)

    def step(masked, row0=0):
        rows = slice(row0, tq)
        nr = tq - row0
        if masked:
            qpos = qi * tq + row0 + lax.broadcasted_iota(I32, (nr, tk), 0)
            kpos = ki * tk + lax.broadcasted_iota(I32, (nr, tk), 1)
            visible = kpos <= qpos
        for hh in range(FLASH_HEADS):
            q = qp_ref[0, rows, hh * LANES:(hh + 1) * LANES]
            k = kp_ref[0, :, hh * LANES:(hh + 1) * LANES]
            s = lax.dot_general(q, k, (((1,), (1,)), ((), ())), preferred_element_type=F32)
            if masked:
                s = jnp.where(visible, s, NEG_BIG)
            m_prev = m_ref[hh, rows, :]
            m_new = jnp.maximum(m_prev, jnp.max(s, axis=1, keepdims=True))
            p = jnp.exp2(s - jnp.concatenate([m_new] * (tk // LANES), axis=1))
            acc_ref[hh, rows, :] = (jnp.exp2(m_prev - m_new) * acc_ref[hh, rows, :]
                                    + jnp.dot(p.astype(BF16), vp_ref[0, :, hh * LANES:(hh + 1) * LANES],
                                              preferred_element_type=F32))
            m_ref[hh, rows, :] = m_new

    crosses = (ki + 1) * tk - 1 > qi * tq
    skip_rows = tk if tq == 2 * tk else 0

    @pl.when(crosses & (ki == last))
    def _():
        step(True, skip_rows)

    @pl.when(crosses & (ki != last))
    def _():
        step(True)

    @pl.when(jnp.logical_not(crosses))
    def _():
        step(False)

    @pl.when(ki == last)
    def _():
        lane = lax.broadcasted_iota(I32, (tq, LANES), 1)
        for pr in range(FLASH_HEADS // 2):
            a0 = acc_ref[2 * pr]
            a1 = acc_ref[2 * pr + 1]
            o0 = a0 / a0[:, HEAD_DIM:HEAD_DIM + 1]
            o1 = a1 / a1[:, 0:1]
            o_ref[0, :, pr * LANES:(pr + 1) * LANES] = jnp.where(lane < HEAD_DIM, o0, o1).astype(o_ref.dtype)


def _flash_attention(qp, kp, vp, tq, tk):
    B, L, _ = qp.shape
    pairs = ATT_HEADS // FLASH_HEADS
    hw = FLASH_HEADS * LANES
    qs, ks = [], []
    for qi in range(L // tq):
        for ki in range(((qi + 1) * tq - 1) // tk + 1):
            qs.append(qi)
            ks.append(ki)
    qi_tab = jnp.asarray(np.array(qs, np.int32))
    ki_tab = jnp.asarray(np.array(ks, np.int32))
    grid_spec = pltpu.PrefetchScalarGridSpec(
        num_scalar_prefetch=2,
        grid=(B, pairs, len(qs)),
        in_specs=[pl.BlockSpec((1, tq, hw), lambda b, p, t, qt, kt: (b, qt[t], p)),
                  pl.BlockSpec((1, tk, hw), lambda b, p, t, qt, kt: (b, kt[t], p)),
                  pl.BlockSpec((1, tk, hw), lambda b, p, t, qt, kt: (b, kt[t], p))],
        out_specs=pl.BlockSpec((1, tq, hw // 2), lambda b, p, t, qt, kt: (b, qt[t], p)),
        scratch_shapes=[pltpu.VMEM((FLASH_HEADS, tq, LANES), F32), pltpu.VMEM((FLASH_HEADS, tq, LANES), F32)],
    )
    return pl.pallas_call(
        functools.partial(_flash_kernel, tq=tq, tk=tk),
        grid_spec=grid_spec,
        out_shape=jax.ShapeDtypeStruct((B, L, ATT_WIDTH), BF16),
        compiler_params=_cparams("arbitrary", "arbitrary", "arbitrary"),
        name="fox_flash",
    )(qi_tab, ki_tab, qp, kp, vp)


def _head_expander():
    e = np.zeros((SMALL_W, SSM_WIDTH), np.float32)
    for h in range(SSM_HEADS):
        e[DT_COL + h, h * SSM_HEAD_DIM:(h + 1) * SSM_HEAD_DIM] = 1.0
    return e


def _expand_heads(vals, e_bf16):
    hi = vals.astype(BF16)
    lo = (vals - hi.astype(F32)).astype(BF16)
    return (jnp.dot(hi, e_bf16, preferred_element_type=F32)
            + jnp.dot(lo, e_bf16, preferred_element_type=F32))


def _conv_silu_rows(rows, cw_ref, cb_ref):
    acc = cb_ref[...] + cw_ref[CONV_WIDTH - 1:CONV_WIDTH, :] * rows[0]
    for j in range(1, CONV_WIDTH):
        acc = acc + cw_ref[CONV_WIDTH - 1 - j:CONV_WIDTH - j, :] * rows[j]
    return _silu(acc)


def _ssd_kernel(xbc_ref, sm_ref, z_ref, cw_ref, cb_ref, dtb_ref, alog_ref, e_ref, dx_ref, nw_ref,
                y_ref, st_ref, buf_ref, ht_ref):
    c = pl.program_id(1)
    nc = pl.num_programs(1)
    Q = SSD_CHUNK
    halo = SUBLANES

    @pl.when(c == 0)
    def _():
        buf_ref[0:halo, :] = jnp.zeros((halo, CONV_DIM), F32)
        ht_ref[...] = jnp.zeros_like(ht_ref)

    @pl.when(c > 0)
    def _():
        buf_ref[0:halo, :] = buf_ref[Q:Q + halo, :]

    buf_ref[halo:halo + Q, :] = xbc_ref[0]
    xc = _conv_silu_rows([buf_ref[halo - j:halo - j + Q, :] for j in range(CONV_WIDTH)], cw_ref, cb_ref)
    xs = xc[:, :SSM_WIDTH]
    e = e_ref[...]

    dt = _softplus(sm_ref[0] + dtb_ref[...])
    a = dt * (-jnp.exp(alog_ref[...]))
    row = lax.broadcasted_iota(I32, (Q, Q), 0)
    col = lax.broadcasted_iota(I32, (Q, Q), 1)
    causal = col <= row
    acum = jnp.dot(causal.astype(F32), a, precision=HIGHEST, preferred_element_type=F32)
    acum_t = acum.T
    dt_x = _expand_heads(dt, e)
    acum_x = _expand_heads(acum, e)
    last_x = acum_x[Q - 1:Q, :]
    xdt = xs * dt_x
    xdt_b = xdt.astype(BF16)
    x_end = (xdt * jnp.exp(last_x - acum_x)).astype(BF16)
    grow = jnp.exp(acum_x)
    cdecay = jnp.exp(last_x)

    lane = lax.broadcasted_iota(I32, (Q, LANES), 1)
    low = lane < SSM_HEAD_DIM
    hpg = SSM_HEADS // SSM_GROUPS
    gw = hpg * SSM_HEAD_DIM
    y_parts = []
    for g in range(SSM_GROUPS):
        bg = xc[:, SSM_WIDTH + g * D_STATE:SSM_WIDTH + (g + 1) * D_STATE].astype(BF16)
        cg = xc[:, SSM_WIDTH + (SSM_GROUPS + g) * D_STATE:SSM_WIDTH + (SSM_GROUPS + g + 1) * D_STATE].astype(BF16)
        scores = lax.dot_general(cg, bg, (((1,), (1,)), ((), ())), preferred_element_type=F32)
        gs = slice(g * gw, (g + 1) * gw)
        h_prev = ht_ref[:, gs]
        y_off = jnp.dot(cg, h_prev.astype(BF16), preferred_element_type=F32) * grow[:, gs]
        ht_ref[:, gs] = h_prev * cdecay[:, gs] + lax.dot_general(
            bg, x_end[:, gs], (((0,), (0,)), ((), ())), preferred_element_type=F32)
        for pr in range(hpg // 2):
            pair_lo = g * gw + pr * LANES
            xpair = xdt_b[:, pair_lo:pair_lo + LANES]
            halves = []
            for hh in range(2):
                h = g * hpg + 2 * pr + hh
                decay = jnp.where(causal, jnp.exp(acum[:, DT_COL + h:DT_COL + h + 1]
                                                  - acum_t[DT_COL + h:DT_COL + h + 1, :]), 0.0)
                halves.append(jnp.dot((scores * decay).astype(BF16), xpair, preferred_element_type=F32))
            y_parts.append(jnp.where(low, halves[0], halves[1]) + y_off[:, pr * LANES:(pr + 1) * LANES])
    y = jnp.concatenate(y_parts, axis=1) + dx_ref[...] * xs
    gated = y * _silu(z_ref[0].astype(F32))
    y_ref[0] = _rmsnorm_rows(gated, nw_ref[...]).astype(y_ref.dtype)

    @pl.when(c == nc - 1)
    def _():
        st_ref[0] = ht_ref[...]


def _ssm_params(dt_bias, A_log, D_skip):
    pad = (DT_COL, SMALL_W - DT_COL - SSM_HEADS)
    dtb = jnp.pad(dt_bias.astype(F32), pad).reshape(1, SMALL_W)
    alog = jnp.pad(A_log.astype(F32), pad).reshape(1, SMALL_W)
    dx = jnp.repeat(D_skip.astype(F32), SSM_HEAD_DIM).reshape(1, SSM_WIDTH)
    return dtb, alog, dx


def _ssd_prompt(xbc, small, z, conv_w, conv_b, dtb, alog, dx, e, ssm_norm_w):
    B, L, _ = xbc.shape
    Q = SSD_CHUNK
    full = lambda a: pl.BlockSpec(a.shape, lambda b, c: (0,) * a.ndim)
    row_spec = lambda w: pl.BlockSpec((1, Q, w), lambda b, c: (b, c, 0))
    cb = conv_b.reshape(1, CONV_DIM)
    nw = ssm_norm_w.reshape(1, SSM_WIDTH)
    return pl.pallas_call(
        _ssd_kernel,
        grid=(B, L // Q),
        in_specs=[row_spec(CONV_DIM), row_spec(SMALL_W), row_spec(SSM_WIDTH),
                  full(conv_w), full(cb), full(dtb), full(alog), full(e), full(dx), full(nw)],
        out_specs=[row_spec(SSM_WIDTH), pl.BlockSpec((1, D_STATE, SSM_WIDTH), lambda b, c: (b, 0, 0))],
        out_shape=[jax.ShapeDtypeStruct((B, L, SSM_WIDTH), BF16),
                   jax.ShapeDtypeStruct((B, D_STATE, SSM_WIDTH), F32)],
        scratch_shapes=[pltpu.VMEM((Q + SUBLANES, CONV_DIM), F32), pltpu.VMEM((D_STATE, SSM_WIDTH), F32)],
        compiler_params=_cparams("arbitrary", "arbitrary"),
        name="ssd_prompt",
    )(xbc, small, z, conv_w, cb, dtb, alog, e, dx, nw)


DECODE_PAGES = 16


def _decode_attn_kernel(pt_ref, qt_ref, knt_ref, vnt_ref, sm_ref, bf_ref, *refs, pps):
    k_refs = refs[0:pps]
    v_refs = refs[pps:2 * pps]
    lf_refs = refs[2 * pps:3 * pps]
    o_ref, lfo_ref, qrep_ref, m_ref, l_ref, acc_ref, carry_ref, bias_ref = refs[3 * pps:]
    blk = pl.program_id(1)
    nblk = pl.num_programs(1)
    H = ATT_HEADS
    page = k_refs[0].shape[3]
    lane_row = lax.broadcasted_iota(I32, (1, page), 1)

    @pl.when(blk == 0)
    def _():
        lf_new = _log_sigmoid(sm_ref[0] + bf_ref[...])
        lfo_ref[0] = lf_new[:, :H]
        diag = (lax.broadcasted_iota(I32, (H, SMALL_W), 0) == lax.broadcasted_iota(I32, (H, SMALL_W), 1))
        lf_col = jnp.sum(jnp.where(diag, jnp.broadcast_to(lf_new, (H, SMALL_W)), 0.0), axis=1, keepdims=True)
        carry_ref[...] = jnp.broadcast_to(lf_col, (H, page))
        qt = qt_ref[0]
        knt = knt_ref[0].astype(BF16).astype(F32)
        vnt = vnt_ref[0].astype(BF16).astype(F32)
        s_row = jnp.sum(qt * knt, axis=0, keepdims=True)
        lane = lax.broadcasted_iota(I32, (HEAD_DIM, page), 1)
        for h in range(H):
            qrep_ref[h] = jnp.broadcast_to(qt[:, h:h + 1], (HEAD_DIM, page))
            m_ref[h:h + 1, :] = jnp.where(lane_row == 0, jnp.broadcast_to(s_row[:, h:h + 1], (1, page)), NEG_BIG)
            acc_ref[h] = jnp.where(lane == 0, jnp.broadcast_to(vnt[:, h:h + 1], (HEAD_DIM, page)), 0.0)
        l_ref[...] = jnp.broadcast_to(jnp.where(lane_row == 0, 1.0, 0.0), (H, page))

    later = (lax.broadcasted_iota(I32, (page, page), 0) > lax.broadcasted_iota(I32, (page, page), 1)).astype(F32)
    carry = carry_ref[...]
    lf_all = jnp.concatenate([lf_refs[j][0] for j in range(pps)], axis=0)
    suffix = jnp.dot(lf_all, later, precision=HIGHEST, preferred_element_type=F32)
    for j in range(pps):
        bias_ref[j] = (suffix[j * H:(j + 1) * H, :] + carry) * LOG2E
        carry = carry + jnp.sum(lf_refs[j][0], axis=1, keepdims=True)
    carry_ref[...] = carry

    def head_body(h, _):
        q3 = qrep_ref[h]
        row = pl.ds(h, 1)
        m = m_ref[row, :]
        l = l_ref[row, :]
        acc = acc_ref[h]
        for j in range(pps):
            s = jnp.sum(q3 * k_refs[j][0, h], axis=0, keepdims=True) + bias_ref[j, row, :]
            m_new = jnp.maximum(m, s)
            alpha = jnp.exp2(m - m_new)
            p = jnp.exp2(s - m_new)
            l = alpha * l + p
            acc = alpha * acc + p * v_refs[j][0, h]
            m = m_new
        m_ref[row, :] = m
        l_ref[row, :] = l
        acc_ref[h] = acc
        return 0

    lax.fori_loop(0, H, head_body, 0)

    @pl.when(blk == nblk - 1)
    def _():
        m_all = m_ref[...]
        w = jnp.exp2(m_all - jnp.max(m_all, axis=1, keepdims=True))
        den = jnp.sum(l_ref[...] * w, axis=1, keepdims=True)
        heads = []
        for h in range(H):
            w_rows = jnp.broadcast_to(w[h:h + 1, :], (SUBLANES, page))
            num = lax.dot_general(w_rows, acc_ref[h], (((1,), (1,)), ((), ())), precision=HIGHEST,
                                  preferred_element_type=F32)
            heads.append(num[0:1, :] / den[h:h + 1, :])
        o_ref[0] = jnp.concatenate(heads, axis=1)


def _decode_attention(page_table, q_t, kn_t, vn_t, small, b_f, cache_k_t, cache_v_t, cache_lf_t):
    Bd = q_t.shape[0]
    n_pages = page_table.shape[1]
    page = cache_k_t.shape[3]
    pps = math.gcd(DECODE_PAGES, n_pages)
    pt_flat = page_table.reshape(-1)
    bf2 = jnp.pad(b_f.reshape(1, ATT_HEADS), ((0, 0), (0, SMALL_W - ATT_HEADS)))

    def page_spec(shape, j):
        def imap(b, blk, pt):
            return (pt[b * n_pages + (n_pages - 1 - (blk * pps + j))],) + (0,) * (len(shape) - 1)
        return pl.BlockSpec(shape, imap)

    col_spec = pl.BlockSpec((1, HEAD_DIM, ATT_HEADS), lambda b, blk, pt: (b, 0, 0))
    grid_spec = pltpu.PrefetchScalarGridSpec(
        num_scalar_prefetch=1,
        grid=(Bd, n_pages // pps),
        in_specs=([col_spec, col_spec, col_spec,
                   pl.BlockSpec((1, 1, SMALL_W), lambda b, blk, pt: (b, 0, 0)),
                   pl.BlockSpec((1, SMALL_W), lambda b, blk, pt: (0, 0))]
                  + [page_spec((1, ATT_HEADS, HEAD_DIM, page), j) for j in range(pps)]
                  + [page_spec((1, ATT_HEADS, HEAD_DIM, page), j) for j in range(pps)]
                  + [page_spec((1, ATT_HEADS, page), j) for j in range(pps)]),
        out_specs=[pl.BlockSpec((1, 1, ATT_WIDTH), lambda b, blk, pt: (b, 0, 0)),
                   pl.BlockSpec((1, 1, ATT_HEADS), lambda b, blk, pt: (b, 0, 0))],
        scratch_shapes=[pltpu.VMEM((ATT_HEADS, HEAD_DIM, page), F32), pltpu.VMEM((ATT_HEADS, page), F32),
                        pltpu.VMEM((ATT_HEADS, page), F32), pltpu.VMEM((ATT_HEADS, HEAD_DIM, page), F32),
                        pltpu.VMEM((ATT_HEADS, page), F32), pltpu.VMEM((pps, ATT_HEADS, page), F32)],
    )
    return pl.pallas_call(
        functools.partial(_decode_attn_kernel, pps=pps),
        grid_spec=grid_spec,
        out_shape=[jax.ShapeDtypeStruct((Bd, 1, ATT_WIDTH), F32),
                   jax.ShapeDtypeStruct((Bd, 1, ATT_HEADS), F32)],
        compiler_params=_cparams("arbitrary", "arbitrary"),
        name="fox_decode",
    )(pt_flat, q_t, kn_t, vn_t, small, bf2, *([cache_k_t] * pps), *([cache_v_t] * pps), *([cache_lf_t] * pps))


SSM_STEP_BATCH = 4


def _ssm_step_kernel(xbc_ref, sc_ref, sm_ref, z_ref, h0_ref, cw_ref, cb_ref, dtb_ref, alog_ref, e_ref,
                     dx_ref, nw_ref, y_ref, st_ref):
    for i in range(xbc_ref.shape[0]):
        _ssm_step_one(i, xbc_ref, sc_ref, sm_ref, z_ref, h0_ref, cw_ref, cb_ref, dtb_ref, alog_ref, e_ref,
                      dx_ref, nw_ref, y_ref, st_ref)


def _ssm_step_one(i, xbc_ref, sc_ref, sm_ref, z_ref, h0_ref, cw_ref, cb_ref, dtb_ref, alog_ref, e_ref,
                  dx_ref, nw_ref, y_ref, st_ref):
    H = SSM_HEADS
    rows = [xbc_ref[i]] + [sc_ref[i, CONV_WIDTH - 1 - j:CONV_WIDTH - j, :] for j in range(1, CONV_WIDTH)]
    xc = _conv_silu_rows(rows, cw_ref, cb_ref)
    xs = xc[:, :SSM_WIDTH]
    e = e_ref[...]
    dt = _softplus(sm_ref[i] + dtb_ref[...])
    da = jnp.exp(dt * (-jnp.exp(alog_ref[...])))
    both = _expand_heads(jnp.concatenate([jnp.broadcast_to(dt, (SUBLANES, SMALL_W)),
                                          jnp.broadcast_to(da, (SUBLANES, SMALL_W))], axis=0), e)
    dt_x = both[0:1, :]
    da_x = both[SUBLANES:SUBLANES + 1, :]
    xdt = xs * dt_x

    sub = lax.broadcasted_iota(I32, (H, SSM_WIDTH), 0)
    own = sub == lax.broadcasted_iota(I32, (H, SSM_WIDTH), 1) // SSM_HEAD_DIM

    def masked_parts(v):
        m = jnp.where(own, jnp.broadcast_to(v, (H, SSM_WIDTH)), 0.0)
        hi = m.astype(BF16)
        return hi, (m - hi.astype(F32)).astype(BF16)

    da_hi, da_lo = masked_parts(da_x)
    x_hi, x_lo = masked_parts(xdt)
    lhs = jnp.concatenate([da_hi, da_lo, x_hi, x_lo], axis=0)
    hpg = H // SSM_GROUPS
    grp = lax.broadcasted_iota(I32, (H, D_STATE), 0) // hpg
    b_rows = jnp.zeros((H, D_STATE), F32)
    c_rows = jnp.zeros((H, D_STATE), F32)
    for g in range(SSM_GROUPS):
        bg = xc[:, SSM_WIDTH + g * D_STATE:SSM_WIDTH + (g + 1) * D_STATE]
        cg = xc[:, SSM_WIDTH + (SSM_GROUPS + g) * D_STATE:SSM_WIDTH + (SSM_GROUPS + g + 1) * D_STATE]
        b_rows = jnp.where(grp == g, jnp.broadcast_to(bg, (H, D_STATE)), b_rows)
        c_rows = jnp.where(grp == g, jnp.broadcast_to(cg, (H, D_STATE)), c_rows)
    ones = jnp.ones((2 * H, D_STATE), BF16)
    zeros = jnp.zeros((2 * H, D_STATE), BF16)
    b_bf = b_rows.astype(BF16)
    rhs = jnp.concatenate([jnp.concatenate([ones, zeros], axis=1),
                           jnp.concatenate([zeros, jnp.concatenate([b_bf, b_bf], axis=0)], axis=1)], axis=0)
    mix = lax.dot_general(lhs, rhs, (((0,), (0,)), ((), ())), preferred_element_type=F32)
    h0 = h0_ref[i].reshape(SSM_WIDTH, D_STATE)
    h_new = mix[:, :D_STATE] * h0 + mix[:, D_STATE:]
    st_ref[i] = h_new.reshape(H, SSM_HEAD_DIM, D_STATE)
    y_t = lax.dot_general(c_rows.astype(BF16), h_new.astype(BF16), (((1,), (1,)), ((), ())),
                          preferred_element_type=F32)
    y = jnp.sum(jnp.where(own, y_t, 0.0), axis=0, keepdims=True) + dx_ref[...] * xs
    gated = y * _silu(z_ref[i].astype(F32))
    y_ref[i] = _rmsnorm_rows(gated, nw_ref[...]).astype(y_ref.dtype)


def _ssm_step(xbc, state_conv, small, z, state_ssm, conv_w, conv_b, dtb, alog, dx, e, ssm_norm_w):
    Bd = xbc.shape[0]
    bb = math.gcd(SSM_STEP_BATCH, Bd)
    full = lambda a: pl.BlockSpec(a.shape, lambda b: (0,) * a.ndim)
    row = lambda w: pl.BlockSpec((bb, 1, w), lambda b: (b, 0, 0))
    st_spec = pl.BlockSpec((bb, SSM_HEADS, SSM_HEAD_DIM, D_STATE), lambda b: (b, 0, 0, 0))
    cb = conv_b.reshape(1, CONV_DIM)
    nw = ssm_norm_w.reshape(1, SSM_WIDTH)
    return pl.pallas_call(
        _ssm_step_kernel,
        grid=(Bd // bb,),
        in_specs=[row(CONV_DIM), pl.BlockSpec((bb, CONV_WIDTH - 1, CONV_DIM), lambda b: (b, 0, 0)),
                  row(SMALL_W), row(SSM_WIDTH), st_spec,
                  full(conv_w), full(cb), full(dtb), full(alog), full(e), full(dx), full(nw)],
        out_specs=[row(SSM_WIDTH), st_spec],
        out_shape=[jax.ShapeDtypeStruct((Bd, 1, SSM_WIDTH), BF16),
                   jax.ShapeDtypeStruct(state_ssm.shape, F32)],
        compiler_params=_cparams("arbitrary"),
        name="ssm_step",
    )(xbc, state_conv, small, z, state_ssm, conv_w, cb, dtb, alog, e, dx, nw)


def _outproj_kernel(att_ref, ssm_ref, x_ref, g1_ref, sh2_ref, sc2_ref, n2_ref, wa_ref, wsm_ref, rw_ref, rb_ref,
                    cin_ref, h2_prev_ref, x1_ref, h2_ref, rt_ref, tg_ref, cout_ref, cnt_ref):
    del h2_prev_ref

    @pl.when((pl.program_id(0) == 0) & (pl.program_id(1) == 0))
    def _():
        cnt_ref[...] = cin_ref[...]

    y = (jnp.dot(att_ref[0], wa_ref[...], preferred_element_type=F32)
         + jnp.dot(ssm_ref[0], wsm_ref[...], preferred_element_type=F32))
    x1 = x_ref[0] + g1_ref[0] * y
    x1_ref[0] = x1
    h2 = _rmsnorm_rows(x1, n2_ref[...]) * (1.0 + sc2_ref[0]) + sh2_ref[0]
    h2_ref[...] = _pack_bf16_pairs(h2)
    logits = _dot_split(h2, h2.astype(BF16), rw_ref) + rb_ref[...]
    tm = logits.shape[0]
    lane = lax.broadcasted_iota(I32, (tm, LANES), 1).astype(F32)
    cur = logits
    idxs = []
    val_tile = jnp.full((tm, LANES), NEG_BIG, F32)
    chosen = jnp.zeros((tm, LANES), F32)
    for k in range(TOP_K):
        m = jnp.max(cur, axis=1, keepdims=True)
        idx = jnp.min(jnp.where(cur == m, lane, float(LANES)), axis=1, keepdims=True)
        idxs.append(idx)
        val_tile = jnp.where(lane == float(k), m, val_tile)
        hit = lane == idx
        chosen = jnp.where(hit, 1.0, chosen)
        cur = jnp.where(hit, 2.0 * NEG_BIG, cur)
    top = jnp.max(val_tile, axis=1, keepdims=True)
    ex = jnp.exp2((val_tile - top) * LOG2E)
    tg_ref[0] = ex / jnp.sum(ex, axis=1, keepdims=True)

    before = (lax.broadcasted_iota(I32, (tm, tm), 1) < lax.broadcasted_iota(I32, (tm, tm), 0)).astype(BF16)
    rank = jnp.dot(before, chosen.astype(BF16), preferred_element_type=F32) + cnt_ref[0:1, :]
    cnt_ref[0:1, :] = cnt_ref[0:1, :] + jnp.sum(chosen, axis=0, keepdims=True)
    cout_ref[...] = cnt_ref[...]
    route = jnp.zeros((tm, LANES), F32)
    for k in range(TOP_K):
        rank_k = jnp.sum(jnp.where(lane == idxs[k], rank, 0.0), axis=1, keepdims=True)
        route = jnp.where(lane == float(k), idxs[k], route)
        route = jnp.where(lane == float(TOP_K + k), rank_k, route)
    rt_ref[0] = route.astype(I32)


def _out_proj(att, ssm, x, g1, sh2, sc2, norm2_w, wa, wsm, rw, rb, counts_in, h2_all, row0, tm):
    B, L, _ = x.shape
    steps = L // tm
    assert row0 % tm == 0
    per_row = g1.shape[1] != 1
    mod_spec = (pl.BlockSpec((1, tm, D_MODEL), lambda b, i: (b, i, 0)) if per_row
                else pl.BlockSpec((1, 1, D_MODEL), lambda b, i: (b, 0, 0)))
    full = lambda a: pl.BlockSpec(a.shape, lambda b, i: (0,) * a.ndim)
    row = lambda w: pl.BlockSpec((1, tm, w), lambda b, i: (b, i, 0))
    n2 = norm2_w.reshape(1, D_MODEL)
    cnt_spec = pl.BlockSpec((SUBLANES, LANES), lambda b, i: (0, 0))
    return pl.pallas_call(
        _outproj_kernel,
        grid=(B, L // tm),
        in_specs=[row(ATT_WIDTH), row(SSM_WIDTH), row(D_MODEL), mod_spec, mod_spec, mod_spec,
                  full(n2), full(wa), full(wsm), full(rw), full(rb), cnt_spec, pl.BlockSpec(memory_space=pl.ANY)],
        out_specs=[row(D_MODEL), pl.BlockSpec((tm, D_MODEL // 2), lambda b, i: (row0 // tm + b * steps + i, 0)),
                   row(LANES), row(LANES), cnt_spec],
        out_shape=[jax.ShapeDtypeStruct((B, L, D_MODEL), F32), jax.ShapeDtypeStruct(h2_all.shape, h2_all.dtype),
                   jax.ShapeDtypeStruct((B, L, LANES), I32), jax.ShapeDtypeStruct((B, L, LANES), F32),
                   jax.ShapeDtypeStruct((SUBLANES, LANES), F32)],
        input_output_aliases={12: 1},
        scratch_shapes=[pltpu.VMEM((SUBLANES, LANES), F32)],
        compiler_params=_cparams("arbitrary", "arbitrary"),
        name="out_proj_route",
    )(att, ssm, x, g1, sh2, sc2, n2, wa, wsm, rw, rb, counts_in, h2_all)


def _moe_block_tables(counts, n_blocks):
    tb = MOE_ROWS
    nb = (counts + tb - 1) // tb
    cum = jnp.cumsum(nb)
    blk_start = cum - nb
    b = jnp.arange(n_blocks, dtype=I32)
    block_expert = jnp.minimum(jnp.sum((cum[None, :] <= b[:, None]).astype(I32), axis=1), N_EXPERTS - 1)
    mine = block_expert[:, None] == jnp.arange(N_EXPERTS, dtype=I32)[None, :]
    pick = lambda per_expert: jnp.sum(jnp.where(mine, per_expert[None, :], 0), axis=1)
    rows_left = pick(counts) - (b - pick(blk_start)) * tb
    block_rows = jnp.where(b < cum[-1], jnp.clip(rows_left, 0, tb), 0).astype(I32)
    prev = jnp.concatenate([jnp.full((1,), -1, I32), block_expert[:-1]])
    block_first = ((block_expert != prev) & (block_rows > 0)).astype(I32)
    return (blk_start * tb).astype(I32), block_expert.astype(I32), block_first, block_rows


SC_WORKERS = 32
SC_SCATTER_WINDOW = 48
SC_WINDOW = 64


def _sc_scatter_rows(src, slots, n_rows):
    T, d = src.shape
    win = SC_SCATTER_WINDOW
    steps = T // win * TOP_K
    assert T % win == 0 and steps % SC_WORKERS == 0
    idx = slots.reshape(T // win, win, TOP_K).transpose(0, 2, 1).reshape(steps, win)
    idx_rows = jnp.pad(idx, ((0, 0), (0, LANES - win)))
    mesh = plsc.VectorSubcoreMesh(core_axis_name="c", subcore_axis_name="s")

    @functools.partial(pl.kernel, out_type=jax.ShapeDtypeStruct((n_rows, d), src.dtype), mesh=mesh, name="moe_scatter")
    def scatter(x_hbm, i_hbm, o_hbm):
        def body(x_vmem, i_vmem):
            pltpu.sync_copy(x_vmem, o_hbm.at[i_vmem.at[0, pl.ds(0, win)]])

        pltpu.emit_pipeline(body, grid=(steps,),
                            in_specs=[pl.BlockSpec((win, d), lambda i: (i // TOP_K, 0)),
                                      pl.BlockSpec((1, LANES), lambda i: (i, 0))],
                            out_specs=[],
                            core_axis_name=("c", "s"), dimension_semantics=(pltpu.PARALLEL,))(x_hbm, i_hbm)

    return scatter(src, idx_rows)


def _moe_kernel(be_ref, first_ref, rows_ref, x_ref, wgu_ref, bgu_ref, wd_ref, bd_ref, o_ref, wgu_s, wd_s):
    i = pl.program_id(0)

    @pl.when(first_ref[i] == 1)
    def _():
        wgu_s[...] = wgu_ref[0].astype(BF16)
        wd_s[...] = wd_ref[0].astype(BF16)

    @pl.when(rows_ref[i] > 0)
    def _():
        live = lax.broadcasted_iota(I32, x_ref.shape, 0) < rows_ref[i]
        x = _unpack_bf16_pairs(jnp.where(live, x_ref[...], jnp.uint32(0))).astype(BF16)
        gu = jnp.dot(x, wgu_s[...], preferred_element_type=F32) + bgu_ref[0]
        g = jnp.minimum(gu[:, :D_FF], SWIGLU_LIMIT)
        u = jnp.clip(gu[:, D_FF:], -SWIGLU_LIMIT, SWIGLU_LIMIT)
        act = (u + 1.0) * (g * jax.nn.sigmoid(SWIGLU_ALPHA * g))
        o_ref[...] = _pack_bf16_pairs(jnp.dot(act.astype(BF16), wd_s[...], preferred_element_type=F32) + bd_ref[0])

    @pl.when(rows_ref[i] == 0)
    def _():
        o_ref[...] = jnp.zeros_like(o_ref)


def _moe_blocks(block_expert, block_first, block_rows, x_sorted, w_gate_up, b_gate_up, w_down, b_down):
    n_rows = x_sorted.shape[0]
    tb = MOE_ROWS
    grid_spec = pltpu.PrefetchScalarGridSpec(
        num_scalar_prefetch=3,
        grid=(n_rows // tb,),
        in_specs=[pl.BlockSpec((tb, D_MODEL // 2), lambda i, be, bf, br: (i, 0)),
                  pl.BlockSpec((1, D_MODEL, 2 * D_FF), lambda i, be, bf, br: (be[i], 0, 0)),
                  pl.BlockSpec((1, 1, 2 * D_FF), lambda i, be, bf, br: (be[i], 0, 0)),
                  pl.BlockSpec((1, D_FF, D_MODEL), lambda i, be, bf, br: (be[i], 0, 0)),
                  pl.BlockSpec((1, 1, D_MODEL), lambda i, be, bf, br: (be[i], 0, 0))],
        out_specs=pl.BlockSpec((tb, D_MODEL // 2), lambda i, be, bf, br: (i, 0)),
        scratch_shapes=[pltpu.VMEM((D_MODEL, 2 * D_FF), BF16), pltpu.VMEM((D_FF, D_MODEL), BF16)],
    )
    return pl.pallas_call(
        _moe_kernel,
        grid_spec=grid_spec,
        out_shape=jax.ShapeDtypeStruct((n_rows, D_MODEL // 2), jnp.uint32),
        compiler_params=_cparams("arbitrary"),
        name="moe_experts",
    )(block_expert, block_first, block_rows, x_sorted, w_gate_up,
      b_gate_up.reshape(N_EXPERTS, 1, 2 * D_FF), w_down, b_down.reshape(N_EXPERTS, 1, D_MODEL))


def _sc_gather_rows(table, idx):
    n = idx.shape[0]
    d = table.shape[1]
    win = SC_WINDOW
    idx_rows = jnp.pad(idx.reshape(n // win, win), ((0, 0), (0, LANES - win)))
    mesh = plsc.VectorSubcoreMesh(core_axis_name="c", subcore_axis_name="s")

    @functools.partial(pl.kernel, out_type=jax.ShapeDtypeStruct((n, d), table.dtype), mesh=mesh, name="moe_gather")
    def gather(t_hbm, i_hbm, o_hbm):
        def body(i_vmem, o_vmem):
            pltpu.sync_copy(t_hbm.at[i_vmem.at[0, pl.ds(0, win)]], o_vmem)

        pltpu.emit_pipeline(body, grid=(n // win,),
                            in_specs=[pl.BlockSpec((1, LANES), lambda i: (i, 0))],
                            out_specs=[pl.BlockSpec((win, d), lambda i: (i, 0))],
                            core_axis_name=("c", "s"), dimension_semantics=(pltpu.PARALLEL,))(i_hbm, o_hbm)

    return gather(table, idx_rows)


def _final_kernel(x1_ref, tg_ref, g2_ref, nw_ref, y0_ref, y1_ref, y2_ref, y3_ref, o_ref):
    gates = tg_ref[0]
    moe = gates[:, 0:1] * _unpack_bf16_pairs(y0_ref[...])
    for k, y_ref in enumerate((y1_ref, y2_ref, y3_ref), start=1):
        moe = moe + gates[:, k:k + 1] * _unpack_bf16_pairs(y_ref[...])
    o_ref[0] = _rmsnorm_rows(x1_ref[0] + g2_ref[0] * moe, nw_ref[...])


def _moe_final(x1, gates, g2, final_norm_w, y_rows, row0, tt):
    B, L, _ = x1.shape
    steps = L // tt
    blocks_per_k = B * steps
    base = row0 // tt
    assert row0 % tt == 0 and TOP_K == 4
    per_row = g2.shape[1] != 1
    mod_spec = (pl.BlockSpec((1, tt, D_MODEL), lambda b, i: (b, i, 0)) if per_row
                else pl.BlockSpec((1, 1, D_MODEL), lambda b, i: (b, 0, 0)))
    y_specs = [pl.BlockSpec((tt, D_MODEL // 2), functools.partial(lambda b, i, k: (base + k * blocks_per_k + b * steps + i, 0), k=k))
               for k in range(TOP_K)]
    nw = final_norm_w.reshape(1, D_MODEL)
    return pl.pallas_call(
        _final_kernel,
        grid=(B, steps),
        in_specs=[pl.BlockSpec((1, tt, D_MODEL), lambda b, i: (b, i, 0)),
                  pl.BlockSpec((1, tt, LANES), lambda b, i: (b, i, 0)),
                  mod_spec, pl.BlockSpec((1, D_MODEL), lambda b, i: (0, 0))] + y_specs,
        out_specs=pl.BlockSpec((1, tt, D_MODEL), lambda b, i: (b, i, 0)),
        out_shape=jax.ShapeDtypeStruct((B, L, D_MODEL), F32),
        compiler_params=_cparams("arbitrary", "arbitrary"),
        name="moe_final",
    )(x1, gates, g2, nw, y_rows, y_rows, y_rows, y_rows)


def kernel(x_prompt, x_sample, c_prompt, c_sample, cache_k, cache_v, cache_lf, state_conv, state_ssm, page_table,
           ada_w, ada_b, norm1_w, w_in, b_f, conv_w, conv_b, dt_bias, A_log, D_skip, ssm_norm_w, w_out,
           norm2_w, router_w, router_b, w_gate_up, b_gate_up, w_down, b_down, final_norm_w):
    assert ada_w.shape[0] == 1, "single-layer trunk"
    B, L, D = x_prompt.shape
    Bd = x_sample.shape[0]
    assert x_sample.shape[1] == 1 and L % SSD_CHUNK == 0

    n_c = B + Bd
    rows = -(-n_c // SUBLANES) * SUBLANES
    c_all = jnp.concatenate([c_prompt, c_sample, jnp.zeros((rows - n_c, D), F32)], axis=0)
    mod = _modulation(c_all, ada_w[0], ada_b[0])
    mod_p = [m.reshape(B, 1, D) for m in jnp.split(mod[:B], 6, axis=-1)]
    mod_s = [m.reshape(1, Bd, D) for m in jnp.split(mod[B:n_c], 6, axis=-1)]

    w = w_in[0]
    o_f = 3 * ATT_WIDTH
    o_z = o_f + ATT_HEADS
    o_x = o_z + SSM_WIDTH
    o_dt = o_x + CONV_DIM
    w_small = jnp.concatenate([w[:, o_f:o_z], w[:, o_dt:o_dt + SSM_HEADS],
                               jnp.zeros((D, SMALL_W - ATT_HEADS - SSM_HEADS), F32)], axis=1)
    wts = (w[:, :ATT_WIDTH].astype(BF16), w[:, ATT_WIDTH:2 * ATT_WIDTH].astype(BF16),
           w[:, 2 * ATT_WIDTH:o_f].astype(BF16), w[:, o_z:o_x].astype(BF16), w[:, o_x:o_dt].astype(BF16), _split_weight(w_small))
    wa = w_out[0][:ATT_WIDTH].astype(BF16)
    wsm = w_out[0][ATT_WIDTH:].astype(BF16)
    rw = _split_weight(jnp.pad(router_w[0], ((0, 0), (0, LANES - N_EXPERTS))))
    rb = jnp.pad(router_b[0].reshape(1, N_EXPERTS), ((0, 0), (0, LANES - N_EXPERTS)), constant_values=NEG_BIG)
    dtb, alog, dx = _ssm_params(dt_bias[0], A_log[0], D_skip[0])
    e = jnp.asarray(_head_expander(), BF16)

    tm_p = min(512, L)
    qp, kp, vp, k_p, v_p, z_p, xbc_p, small_p, lf_p = _in_proj(x_prompt, mod_p[0], mod_p[1], norm1_w[0], wts,
                                                              b_f[0], tm_p)
    att_p = _flash_attention(qp, kp, vp, min(1024, L), min(512, L))
    ssm_p, st_p = _ssd_prompt(xbc_p, small_p, z_p, conv_w[0], conv_b[0], dtb, alog, dx, e, ssm_norm_w[0])
    n_tok = B * L + Bd
    zero_counts = jnp.zeros((SUBLANES, LANES), F32)
    h2_all = jnp.zeros((n_tok, D // 2), jnp.uint32)
    x1_p, h2_all, rt_p, tg_p, counts_p = _out_proj(att_p, ssm_p, x_prompt, mod_p[2], mod_p[3], mod_p[4], norm2_w[0],
                                                   wa, wsm, rw, rb, zero_counts, h2_all, 0, min(512, L))

    xs_rows = x_sample.reshape(1, Bd, D)
    qp_s, _, _, k_s, v_s, z_s, xbc_s, small_s, _ = _in_proj(xs_rows, mod_s[0], mod_s[1], norm1_w[0], wts, b_f[0], Bd)
    per_row = lambda a: a.reshape(Bd, 1, a.shape[-1])
    head_cols = lambda a: a.reshape(Bd, ATT_HEADS, HEAD_DIM).transpose(0, 2, 1).astype(F32)
    q_halves = qp_s.astype(F32).reshape(Bd, ATT_HEADS, 2, HEAD_DIM)
    odd_head = (jnp.arange(ATT_HEADS) % 2 == 1)[None, :, None]
    q_s = jnp.where(odd_head, q_halves[:, :, 1, :], q_halves[:, :, 0, :])
    att_s4, lf_s = _decode_attention(
        page_table, head_cols(q_s), head_cols(k_s), head_cols(v_s), per_row(small_s), b_f[0],
        cache_k[0].transpose(0, 2, 3, 1), cache_v[0].transpose(0, 2, 3, 1), cache_lf[0].transpose(0, 2, 1))
    att_s = att_s4.reshape(1, Bd, ATT_WIDTH).astype(BF16)
    ssm_s, st_s = _ssm_step(per_row(xbc_s), state_conv[0], per_row(small_s), per_row(z_s), state_ssm[0],
                            conv_w[0], conv_b[0], dtb, alog, dx, e, ssm_norm_w[0])
    x1_s, h2_all, rt_s, tg_s, counts = _out_proj(att_s, ssm_s.reshape(1, Bd, SSM_WIDTH), xs_rows, mod_s[2], mod_s[3],
                                                 mod_s[4], norm2_w[0], wa, wsm, rw, rb, counts_p, h2_all, B * L, Bd)

    n_blocks = (n_tok * TOP_K + N_EXPERTS * (MOE_ROWS - 1)) // MOE_ROWS
    pad_start, b_exp, b_first, b_rows = _moe_block_tables(counts[0, :N_EXPERTS].astype(I32), n_blocks)

    def token_slots(rt):
        e_idx = rt[..., :TOP_K].reshape(-1, TOP_K)
        first = jnp.sum(jnp.where(e_idx[..., None] == jnp.arange(N_EXPERTS, dtype=I32), pad_start, 0), axis=-1)
        return first + rt[..., TOP_K:2 * TOP_K].reshape(-1, TOP_K)

    slots_p, slots_s = token_slots(rt_p), token_slots(rt_s)
    x_sorted = _sc_scatter_rows(h2_all, jnp.concatenate([slots_p, slots_s], axis=0), n_blocks * MOE_ROWS)
    y_sorted = _moe_blocks(b_exp, b_first, b_rows, x_sorted, w_gate_up[0], b_gate_up[0], w_down[0], b_down[0])

    n_assign = n_tok * TOP_K
    chunk = SC_WINDOW * SC_WORKERS
    n_idx = -(-n_assign // chunk) * chunk
    slots = jnp.concatenate([slots_p.T.reshape(-1), slots_s.T.reshape(-1), jnp.zeros((n_idx - n_assign,), I32)])
    y_rows = _sc_gather_rows(y_sorted, slots)
    y_prompt = _moe_final(x1_p, tg_p, mod_p[5], final_norm_w, y_rows, 0, min(512, L))
    y_sample = _moe_final(x1_s, tg_s, mod_s[5], final_norm_w, y_rows, B * L * TOP_K, Bd).reshape(Bd, 1, D)

    conv_s = jnp.concatenate([state_conv[0][:, 1:], xbc_s.reshape(Bd, 1, CONV_DIM)], axis=1)
    ssm_state_p = st_p.reshape(B, D_STATE, SSM_HEADS, SSM_HEAD_DIM).transpose(0, 2, 3, 1)
    return (y_prompt, y_sample,
            k_p.reshape(1, B, L, ATT_HEADS, HEAD_DIM), v_p.reshape(1, B, L, ATT_HEADS, HEAD_DIM),
            lf_p.reshape(1, B, L, ATT_HEADS), xbc_p[:, L - (CONV_WIDTH - 1):].reshape(1, B, CONV_WIDTH - 1, CONV_DIM),
            ssm_state_p.reshape(1, B, SSM_HEADS, SSM_HEAD_DIM, D_STATE),
            k_s.reshape(1, Bd, 1, ATT_HEADS, HEAD_DIM), v_s.reshape(1, Bd, 1, ATT_HEADS, HEAD_DIM),
            lf_s.reshape(1, Bd, 1, ATT_HEADS), conv_s.reshape(1, Bd, CONV_WIDTH - 1, CONV_DIM),
            st_s.reshape(1, Bd, SSM_HEADS, SSM_HEAD_DIM, D_STATE))
```

```python
import functools
import math

import numpy as np
import jax
import jax.numpy as jnp
from jax import lax
from jax.experimental import pallas as pl
from jax.experimental.pallas import tpu as pltpu
from jax.experimental.pallas import tpu_sc as plsc

F32 = jnp.float32
BF16 = jnp.bfloat16
I32 = jnp.int32
HIGHEST = lax.Precision.HIGHEST

D_MODEL = 1024
ATT_HEADS = 16
HEAD_DIM = 64
ATT_WIDTH = ATT_HEADS * HEAD_DIM
SSM_HEADS = 16
SSM_HEAD_DIM = 64
SSM_WIDTH = SSM_HEADS * SSM_HEAD_DIM
SSM_GROUPS = 2
D_STATE = 128
CONV_WIDTH = 4
CONV_DIM = SSM_WIDTH + 2 * SSM_GROUPS * D_STATE
SSD_CHUNK = 128
N_EXPERTS = 32
TOP_K = 4
D_FF = D_MODEL
SWIGLU_LIMIT = 7.0
SWIGLU_ALPHA = 1.702
NORM_EPS = 1e-5

LANES = 128
SUBLANES = 8
SMALL_W = LANES
DT_COL = ATT_HEADS
NEG_BIG = -1e30
LOG2E = math.log2(math.e)
VMEM_LIMIT = 48 * 1024 * 1024
INPROJ_VMEM_LIMIT = 58 * 1024 * 1024
MOE_ROWS = 512


def _cparams(*sem):
    return pltpu.CompilerParams(dimension_semantics=sem, vmem_limit_bytes=VMEM_LIMIT)


def _silu(x):
    return x * jax.nn.sigmoid(x)


def _softplus(x):
    return jnp.maximum(x, 0.0) + jnp.log(1.0 + jnp.exp(-jnp.abs(x)))


def _log_sigmoid(x):
    return -_softplus(-x)


def _rmsnorm_rows(x, w):
    var = jnp.mean(x * x, axis=-1, keepdims=True)
    return x * lax.rsqrt(var + NORM_EPS) * w


def _split3_bf16(x):
    hi = x.astype(BF16)
    r = x - hi.astype(F32)
    mid = r.astype(BF16)
    lo = (r - mid.astype(F32)).astype(BF16)
    return hi, mid, lo


def _pack_bf16_pairs(x):
    half = x.shape[1] // 2
    lo = pltpu.bitcast(x[:, :half].astype(BF16).astype(F32), jnp.uint32)
    hi = pltpu.bitcast(x[:, half:].astype(BF16).astype(F32), jnp.uint32)
    return hi | lax.shift_right_logical(lo, jnp.uint32(16))


def _unpack_bf16_pairs(w):
    lo = pltpu.bitcast(lax.shift_left(w, jnp.uint32(16)), F32)
    hi = pltpu.bitcast(w & jnp.uint32(0xFFFF0000), F32)
    return jnp.concatenate([lo, hi], axis=1)


def _split_weight(w):
    hi = w.astype(BF16)
    lo = (w - hi.astype(F32)).astype(BF16)
    return jnp.concatenate([hi, lo], axis=1)


def _dot_split(x, x_hi, w_ref):
    x_lo = (x - x_hi.astype(F32)).astype(BF16)
    both = jnp.dot(x_hi, w_ref[...], preferred_element_type=F32)
    return (both[:, :LANES] + both[:, LANES:]
            + jnp.dot(x_lo, w_ref[:, :LANES], preferred_element_type=F32))


def _mod_kernel(c_ref, w_ref, b_ref, o_ref):
    s = _silu(c_ref[...]).astype(BF16)
    o_ref[...] = jnp.dot(s, w_ref[...].astype(BF16), preferred_element_type=F32) + b_ref[...]


def _modulation(c_all, ada_w, ada_b):
    rows = c_all.shape[0]
    n_out = ada_w.shape[1]
    tn = D_MODEL
    return pl.pallas_call(
        _mod_kernel,
        grid=(n_out // tn,),
        in_specs=[pl.BlockSpec((rows, D_MODEL), lambda j: (0, 0)),
                  pl.BlockSpec((D_MODEL, tn), lambda j: (0, j)),
                  pl.BlockSpec((1, tn), lambda j: (0, j))],
        out_specs=pl.BlockSpec((rows, tn), lambda j: (0, j)),
        out_shape=jax.ShapeDtypeStruct((rows, n_out), F32),
        compiler_params=_cparams("arbitrary"),
        name="adaln_mod",
    )(c_all, ada_w, ada_b.reshape(1, n_out))


def _free_half(h):
    return h * LANES + (HEAD_DIM if h % 2 == 0 else 0)


def _bias_layout():
    wide = ATT_HEADS * LANES
    sel = np.zeros((SMALL_W, wide), np.float32)
    rows = np.zeros((5, wide), np.float32)
    for h in range(ATT_HEADS):
        base = _free_half(h)
        rows[4, base] = 1.0
        for part in range(3):
            sel[part * ATT_HEADS + h, base + part] = 1.0
            sel[part * ATT_HEADS + h, base + 3 + part] = -1.0
            rows[0, base + part] = 1.0
            rows[3, base + part] = 1.0
            rows[1, base + 3 + part] = 1.0
            rows[2, base + 3 + part] = 1.0
    return sel, rows


def _inproj_kernel(x_ref, sh_ref, sc_ref, nw_ref, wq_ref, wk_ref, wv_ref, wz_ref, wx_ref, ws_ref, bf_ref,
                   sel_ref, rows_ref, qp_ref, kp_ref, vp_ref, k_ref, v_ref, z_ref, xbc_ref, sm_ref, lf_ref,
                   carry_ref):
    tm = x_ref.shape[1]

    @pl.when(pl.program_id(1) == 0)
    def _():
        carry_ref[...] = jnp.zeros_like(carry_ref)

    h = _rmsnorm_rows(x_ref[0], nw_ref[...]) * (1.0 + sc_ref[0]) + sh_ref[0]
    hb = h.astype(BF16)
    qb = (jnp.dot(hb, wq_ref[...], preferred_element_type=F32) * (HEAD_DIM ** -0.5 * LOG2E)).astype(BF16)
    k = jnp.dot(hb, wk_ref[...], preferred_element_type=F32)
    k_ref[0] = k
    kb = k.astype(BF16)
    v = jnp.dot(hb, wv_ref[...], preferred_element_type=F32)
    v_ref[0] = v
    vb = v.astype(BF16)
    z_ref[0] = jnp.dot(hb, wz_ref[...], preferred_element_type=F32).astype(BF16)
    xbc_ref[0] = jnp.dot(hb, wx_ref[...], preferred_element_type=F32)
    sm = _dot_split(h, hb, ws_ref)
    sm_ref[0] = sm

    lf = _log_sigmoid(sm + bf_ref[...])
    lf_ref[0] = lf[:, :ATT_HEADS]
    tri = (lax.broadcasted_iota(I32, (tm, tm), 1) <= lax.broadcasted_iota(I32, (tm, tm), 0)).astype(BF16)
    sums = jnp.dot(tri, jnp.concatenate(_split3_bf16(lf), axis=1), preferred_element_type=F32)
    fcum = sums[:, :LANES] + sums[:, LANES:2 * LANES] + sums[:, 2 * LANES:] + carry_ref[0:1, :]
    carry_ref[0:1, :] = fcum[tm - 1:tm, :]
    hi, mid, lo = (part.astype(F32) for part in _split3_bf16(fcum * LOG2E))
    lane = lax.broadcasted_iota(I32, (tm, LANES), 1)
    packed = jnp.where(lane < ATT_HEADS, hi,
                       jnp.where(lane < 2 * ATT_HEADS, pltpu.roll(mid, ATT_HEADS, 1),
                                 jnp.where(lane < 3 * ATT_HEADS, pltpu.roll(lo, 2 * ATT_HEADS, 1), 0.0)))
    spread = jnp.dot(packed.astype(BF16), sel_ref[...], preferred_element_type=F32)
    low = lane < HEAD_DIM
    for pair in range(ATT_HEADS // 2):
        ps = slice(pair * LANES, (pair + 1) * LANES)
        for hh in range(2):
            hd = 2 * pair + hh
            keep = low if hh == 0 else jnp.logical_not(low)
            sl = slice(hd * LANES, (hd + 1) * LANES)
            part = spread[:, sl]
            aug_q = (part * rows_ref[0:1, sl] + rows_ref[2:3, sl]).astype(BF16)
            aug_k = (part * rows_ref[1:2, sl] + rows_ref[3:4, sl]).astype(BF16)
            ones_lane = jnp.broadcast_to(rows_ref[4:5, sl], (tm, LANES)).astype(BF16)
            qp_ref[0, :, sl] = jnp.where(keep, qb[:, ps], aug_q)
            kp_ref[0, :, sl] = jnp.where(keep, kb[:, ps], aug_k)
            vp_ref[0, :, sl] = jnp.where(keep, vb[:, ps], ones_lane)


def _in_proj(x, sh, sc, norm_w, wts, b_f, tm):
    B, L, _ = x.shape
    per_row = sh.shape[1] != 1
    mod_spec = (pl.BlockSpec((1, tm, D_MODEL), lambda b, i: (b, i, 0)) if per_row
                else pl.BlockSpec((1, 1, D_MODEL), lambda b, i: (b, 0, 0)))
    wq, wk, wv, wz, wx, ws = wts
    sel, rows = _bias_layout()
    sel, rows = jnp.asarray(sel, BF16), jnp.asarray(rows)
    bf2 = jnp.pad(b_f.reshape(1, ATT_HEADS), ((0, 0), (0, SMALL_W - ATT_HEADS)))
    wide = ATT_HEADS * LANES

    def wspec(w):
        return pl.BlockSpec(w.shape, lambda b, i: (0, 0), pipeline_mode=pl.Buffered(1))

    def ospec(width):
        return pl.BlockSpec((1, tm, width), lambda b, i: (b, i, 0))

    def oshape(width, dt):
        return jax.ShapeDtypeStruct((B, L, width), dt)

    return pl.pallas_call(
        _inproj_kernel,
        grid=(B, L // tm),
        in_specs=[pl.BlockSpec((1, tm, D_MODEL), lambda b, i: (b, i, 0)), mod_spec, mod_spec,
                  pl.BlockSpec((1, D_MODEL), lambda b, i: (0, 0)),
                  wspec(wq), wspec(wk), wspec(wv), wspec(wz), wspec(wx), wspec(ws),
                  wspec(bf2), wspec(sel), wspec(rows)],
        out_specs=[ospec(wide), ospec(wide), ospec(wide), ospec(ATT_WIDTH), ospec(ATT_WIDTH),
                   ospec(SSM_WIDTH), ospec(CONV_DIM), ospec(SMALL_W), ospec(ATT_HEADS)],
        out_shape=[oshape(wide, BF16), oshape(wide, BF16), oshape(wide, BF16),
                   oshape(ATT_WIDTH, F32), oshape(ATT_WIDTH, F32),
                   oshape(SSM_WIDTH, BF16), oshape(CONV_DIM, F32), oshape(SMALL_W, F32), oshape(ATT_HEADS, F32)],
        scratch_shapes=[pltpu.VMEM((SUBLANES, LANES), F32)],
        compiler_params=pltpu.CompilerParams(dimension_semantics=("arbitrary", "arbitrary"),
                                             vmem_limit_bytes=INPROJ_VMEM_LIMIT),
        name="in_proj",
    )(x, sh, sc, norm_w.reshape(1, D_MODEL), wq, wk, wv, wz, wx, ws, bf2, sel, rows)


FLASH_HEADS = 16


def _flash_kernel(qi_ref, ki_ref, qp_ref, kp_ref, vp_ref, o_ref, m_ref, acc_ref, *, tq, tk):
    t = pl.program_id(2)
    qi = qi_ref[t]
    ki = ki_ref[t]
    last = ((qi + 1) * tq - 1) // tk

    @pl.when(ki == 0)
    def _():
        m_ref[...] = jnp.full_like(m_ref, NEG_BIG)
        acc_ref[...] = jnp.zeros_like(acc_ref)

    def step(masked, row0=0):
        rows = slice(row0, tq)
        nr = tq - row0
        if masked:
            qpos = qi * tq + row0 + lax.broadcasted_iota(I32, (nr, tk), 0)
            kpos = ki * tk + lax.broadcasted_iota(I32, (nr, tk), 1)
            visible = kpos <= qpos
        for hh in range(FLASH_HEADS):
            q = qp_ref[0, rows, hh * LANES:(hh + 1) * LANES]
            k = kp_ref[0, :, hh * LANES:(hh + 1) * LANES]
            s = lax.dot_general(q, k, (((1,), (1,)), ((), ())), preferred_element_type=F32)
            if masked:
                s = jnp.where(visible, s, NEG_BIG)
            m_prev = m_ref[hh, rows, :]
            m_new = jnp.maximum(m_prev, jnp.max(s, axis=1, keepdims=True))
            p = jnp.exp2(s - jnp.concatenate([m_new] * (tk // LANES), axis=1))
            acc_ref[hh, rows, :] = (jnp.exp2(m_prev - m_new) * acc_ref[hh, rows, :]
                                    + jnp.dot(p.astype(BF16), vp_ref[0, :, hh * LANES:(hh + 1) * LANES],
                                              preferred_element_type=F32))
            m_ref[hh, rows, :] = m_new

    crosses = (ki + 1) * tk - 1 > qi * tq
    skip_rows = tk if tq == 2 * tk else 0

    @pl.when(crosses & (ki == last))
    def _():
        step(True, skip_rows)

    @pl.when(crosses & (ki != last))
    def _():
        step(True)

    @pl.when(jnp.logical_not(crosses))
    def _():
        step(False)

    @pl.when(ki == last)
    def _():
        lane = lax.broadcasted_iota(I32, (tq, LANES), 1)
        for pr in range(FLASH_HEADS // 2):
            a0 = acc_ref[2 * pr]
            a1 = acc_ref[2 * pr + 1]
            o0 = a0 / a0[:, HEAD_DIM:HEAD_DIM + 1]
            o1 = a1 / a1[:, 0:1]
            o_ref[0, :, pr * LANES:(pr + 1) * LANES] = jnp.where(lane < HEAD_DIM, o0, o1).astype(o_ref.dtype)


def _flash_attention(qp, kp, vp, tq, tk):
    B, L, _ = qp.shape
    pairs = ATT_HEADS // FLASH_HEADS
    hw = FLASH_HEADS * LANES
    qs, ks = [], []
    for qi in range(L // tq):
        for ki in range(((qi + 1) * tq - 1) // tk + 1):
            qs.append(qi)
            ks.append(ki)
    qi_tab = jnp.asarray(np.array(qs, np.int32))
    ki_tab = jnp.asarray(np.array(ks, np.int32))
    grid_spec = pltpu.PrefetchScalarGridSpec(
        num_scalar_prefetch=2,
        grid=(B, pairs, len(qs)),
        in_specs=[pl.BlockSpec((1, tq, hw), lambda b, p, t, qt, kt: (b, qt[t], p)),
                  pl.BlockSpec((1, tk, hw), lambda b, p, t, qt, kt: (b, kt[t], p)),
                  pl.BlockSpec((1, tk, hw), lambda b, p, t, qt, kt: (b, kt[t], p))],
        out_specs=pl.BlockSpec((1, tq, hw // 2), lambda b, p, t, qt, kt: (b, qt[t], p)),
        scratch_shapes=[pltpu.VMEM((FLASH_HEADS, tq, LANES), F32), pltpu.VMEM((FLASH_HEADS, tq, LANES), F32)],
    )
    return pl.pallas_call(
        functools.partial(_flash_kernel, tq=tq, tk=tk),
        grid_spec=grid_spec,
        out_shape=jax.ShapeDtypeStruct((B, L, ATT_WIDTH), BF16),
        compiler_params=_cparams("arbitrary", "arbitrary", "arbitrary"),
        name="fox_flash",
    )(qi_tab, ki_tab, qp, kp, vp)


def _head_expander():
    e = np.zeros((SMALL_W, SSM_WIDTH), np.float32)
    for h in range(SSM_HEADS):
        e[DT_COL + h, h * SSM_HEAD_DIM:(h + 1) * SSM_HEAD_DIM] = 1.0
    return e


def _expand_heads(vals, e_bf16):
    hi = vals.astype(BF16)
    lo = (vals - hi.astype(F32)).astype(BF16)
    return (jnp.dot(hi, e_bf16, preferred_element_type=F32)
            + jnp.dot(lo, e_bf16, preferred_element_type=F32))


def _conv_silu_rows(rows, cw_ref, cb_ref):
    acc = cb_ref[...] + cw_ref[CONV_WIDTH - 1:CONV_WIDTH, :] * rows[0]
    for j in range(1, CONV_WIDTH):
        acc = acc + cw_ref[CONV_WIDTH - 1 - j:CONV_WIDTH - j, :] * rows[j]
    return _silu(acc)


def _ssd_kernel(xbc_ref, sm_ref, z_ref, cw_ref, cb_ref, dtb_ref, alog_ref, e_ref, dx_ref, nw_ref,
                y_ref, st_ref, buf_ref, ht_ref):
    c = pl.program_id(1)
    nc = pl.num_programs(1)
    Q = SSD_CHUNK
    halo = SUBLANES

    @pl.when(c == 0)
    def _():
        buf_ref[0:halo, :] = jnp.zeros((halo, CONV_DIM), F32)
        ht_ref[...] = jnp.zeros_like(ht_ref)

    @pl.when(c > 0)
    def _():
        buf_ref[0:halo, :] = buf_ref[Q:Q + halo, :]

    buf_ref[halo:halo + Q, :] = xbc_ref[0]
    xc = _conv_silu_rows([buf_ref[halo - j:halo - j + Q, :] for j in range(CONV_WIDTH)], cw_ref, cb_ref)
    xs = xc[:, :SSM_WIDTH]
    e = e_ref[...]

    dt = _softplus(sm_ref[0] + dtb_ref[...])
    a = dt * (-jnp.exp(alog_ref[...]))
    row = lax.broadcasted_iota(I32, (Q, Q), 0)
    col = lax.broadcasted_iota(I32, (Q, Q), 1)
    causal = col <= row
    acum = jnp.dot(causal.astype(F32), a, precision=HIGHEST, preferred_element_type=F32)
    acum_t = acum.T
    dt_x = _expand_heads(dt, e)
    acum_x = _expand_heads(acum, e)
    last_x = acum_x[Q - 1:Q, :]
    xdt = xs * dt_x
    xdt_b = xdt.astype(BF16)
    x_end = (xdt * jnp.exp(last_x - acum_x)).astype(BF16)
    grow = jnp.exp(acum_x)
    cdecay = jnp.exp(last_x)

    lane = lax.broadcasted_iota(I32, (Q, LANES), 1)
    low = lane < SSM_HEAD_DIM
    hpg = SSM_HEADS // SSM_GROUPS
    gw = hpg * SSM_HEAD_DIM
    y_parts = []
    for g in range(SSM_GROUPS):
        bg = xc[:, SSM_WIDTH + g * D_STATE:SSM_WIDTH + (g + 1) * D_STATE].astype(BF16)
        cg = xc[:, SSM_WIDTH + (SSM_GROUPS + g) * D_STATE:SSM_WIDTH + (SSM_GROUPS + g + 1) * D_STATE].astype(BF16)
        scores = lax.dot_general(cg, bg, (((1,), (1,)), ((), ())), preferred_element_type=F32)
        gs = slice(g * gw, (g + 1) * gw)
        h_prev = ht_ref[:, gs]
        y_off = jnp.dot(cg, h_prev.astype(BF16), preferred_element_type=F32) * grow[:, gs]
        ht_ref[:, gs] = h_prev * cdecay[:, gs] + lax.dot_general(
            bg, x_end[:, gs], (((0,), (0,)), ((), ())), preferred_element_type=F32)
        for pr in range(hpg // 2):
            pair_lo = g * gw + pr * LANES
            xpair = xdt_b[:, pair_lo:pair_lo + LANES]
            halves = []
            for hh in range(2):
                h = g * hpg + 2 * pr + hh
                decay = jnp.where(causal, jnp.exp(acum[:, DT_COL + h:DT_COL + h + 1]
                                                  - acum_t[DT_COL + h:DT_COL + h + 1, :]), 0.0)
                halves.append(jnp.dot((scores * decay).astype(BF16), xpair, preferred_element_type=F32))
            y_parts.append(jnp.where(low, halves[0], halves[1]) + y_off[:, pr * LANES:(pr + 1) * LANES])
    y = jnp.concatenate(y_parts, axis=1) + dx_ref[...] * xs
    gated = y * _silu(z_ref[0].astype(F32))
    y_ref[0] = _rmsnorm_rows(gated, nw_ref[...]).astype(y_ref.dtype)

    @pl.when(c == nc - 1)
    def _():
        st_ref[0] = ht_ref[...]


def _ssm_params(dt_bias, A_log, D_skip):
    pad = (DT_COL, SMALL_W - DT_COL - SSM_HEADS)
    dtb = jnp.pad(dt_bias.astype(F32), pad).reshape(1, SMALL_W)
    alog = jnp.pad(A_log.astype(F32), pad).reshape(1, SMALL_W)
    dx = jnp.repeat(D_skip.astype(F32), SSM_HEAD_DIM).reshape(1, SSM_WIDTH)
    return dtb, alog, dx


def _ssd_prompt(xbc, small, z, conv_w, conv_b, dtb, alog, dx, e, ssm_norm_w):
    B, L, _ = xbc.shape
    Q = SSD_CHUNK
    full = lambda a: pl.BlockSpec(a.shape, lambda b, c: (0,) * a.ndim)
    row_spec = lambda w: pl.BlockSpec((1, Q, w), lambda b, c: (b, c, 0))
    cb = conv_b.reshape(1, CONV_DIM)
    nw = ssm_norm_w.reshape(1, SSM_WIDTH)
    return pl.pallas_call(
        _ssd_kernel,
        grid=(B, L // Q),
        in_specs=[row_spec(CONV_DIM), row_spec(SMALL_W), row_spec(SSM_WIDTH),
                  full(conv_w), full(cb), full(dtb), full(alog), full(e), full(dx), full(nw)],
        out_specs=[row_spec(SSM_WIDTH), pl.BlockSpec((1, D_STATE, SSM_WIDTH), lambda b, c: (b, 0, 0))],
        out_shape=[jax.ShapeDtypeStruct((B, L, SSM_WIDTH), BF16),
                   jax.ShapeDtypeStruct((B, D_STATE, SSM_WIDTH), F32)],
        scratch_shapes=[pltpu.VMEM((Q + SUBLANES, CONV_DIM), F32), pltpu.VMEM((D_STATE, SSM_WIDTH), F32)],
        compiler_params=_cparams("arbitrary", "arbitrary"),
        name="ssd_prompt",
    )(xbc, small, z, conv_w, cb, dtb, alog, e, dx, nw)


DECODE_PAGES = 16


def _decode_attn_kernel(pt_ref, qt_ref, knt_ref, vnt_ref, sm_ref, bf_ref, *refs, pps):
    k_refs = refs[0:pps]
    v_refs = refs[pps:2 * pps]
    lf_refs = refs[2 * pps:3 * pps]
    o_ref, lfo_ref, qrep_ref, m_ref, l_ref, acc_ref, carry_ref, bias_ref = refs[3 * pps:]
    blk = pl.program_id(1)
    nblk = pl.num_programs(1)
    H = ATT_HEADS
    page = k_refs[0].shape[3]
    lane_row = lax.broadcasted_iota(I32, (1, page), 1)

    @pl.when(blk == 0)
    def _():
        lf_new = _log_sigmoid(sm_ref[0] + bf_ref[...])
        lfo_ref[0] = lf_new[:, :H]
        diag = (lax.broadcasted_iota(I32, (H, SMALL_W), 0) == lax.broadcasted_iota(I32, (H, SMALL_W), 1))
        lf_col = jnp.sum(jnp.where(diag, jnp.broadcast_to(lf_new, (H, SMALL_W)), 0.0), axis=1, keepdims=True)
        carry_ref[...] = jnp.broadcast_to(lf_col, (H, page))
        qt = qt_ref[0]
        knt = knt_ref[0].astype(BF16).astype(F32)
        vnt = vnt_ref[0].astype(BF16).astype(F32)
        s_row = jnp.sum(qt * knt, axis=0, keepdims=True)
        lane = lax.broadcasted_iota(I32, (HEAD_DIM, page), 1)
        for h in range(H):
            qrep_ref[h] = jnp.broadcast_to(qt[:, h:h + 1], (HEAD_DIM, page))
            m_ref[h:h + 1, :] = jnp.where(lane_row == 0, jnp.broadcast_to(s_row[:, h:h + 1], (1, page)), NEG_BIG)
            acc_ref[h] = jnp.where(lane == 0, jnp.broadcast_to(vnt[:, h:h + 1], (HEAD_DIM, page)), 0.0)
        l_ref[...] = jnp.broadcast_to(jnp.where(lane_row == 0, 1.0, 0.0), (H, page))

    later = (lax.broadcasted_iota(I32, (page, page), 0) > lax.broadcasted_iota(I32, (page, page), 1)).astype(F32)
    carry = carry_ref[...]
    lf_all = jnp.concatenate([lf_refs[j][0] for j in range(pps)], axis=0)
    suffix = jnp.dot(lf_all, later, precision=HIGHEST, preferred_element_type=F32)
    for j in range(pps):
        bias_ref[j] = (suffix[j * H:(j + 1) * H, :] + carry) * LOG2E
        carry = carry + jnp.sum(lf_refs[j][0], axis=1, keepdims=True)
    carry_ref[...] = carry

    def head_body(h, _):
        q3 = qrep_ref[h]
        row = pl.ds(h, 1)
        m = m_ref[row, :]
        l = l_ref[row, :]
        acc = acc_ref[h]
        for j in range(pps):
            s = jnp.sum(q3 * k_refs[j][0, h], axis=0, keepdims=True) + bias_ref[j, row, :]
            m_new = jnp.maximum(m, s)
            alpha = jnp.exp2(m - m_new)
            p = jnp.exp2(s - m_new)
            l = alpha * l + p
            acc = alpha * acc + p * v_refs[j][0, h]
            m = m_new
        m_ref[row, :] = m
        l_ref[row, :] = l
        acc_ref[h] = acc
        return 0

    lax.fori_loop(0, H, head_body, 0)

    @pl.when(blk == nblk - 1)
    def _():
        m_all = m_ref[...]
        w = jnp.exp2(m_all - jnp.max(m_all, axis=1, keepdims=True))
        den = jnp.sum(l_ref[...] * w, axis=1, keepdims=True)
        heads = []
        for h in range(H):
            w_rows = jnp.broadcast_to(w[h:h + 1, :], (SUBLANES, page))
            num = lax.dot_general(w_rows, acc_ref[h], (((1,), (1,)), ((), ())), precision=HIGHEST,
                                  preferred_element_type=F32)
            heads.append(num[0:1, :] / den[h:h + 1, :])
        o_ref[0] = jnp.concatenate(heads, axis=1)


def _decode_attention(page_table, q_t, kn_t, vn_t, small, b_f, cache_k_t, cache_v_t, cache_lf_t):
    Bd = q_t.shape[0]
    n_pages = page_table.shape[1]
    page = cache_k_t.shape[3]
    pps = math.gcd(DECODE_PAGES, n_pages)
    pt_flat = page_table.reshape(-1)
    bf2 = jnp.pad(b_f.reshape(1, ATT_HEADS), ((0, 0), (0, SMALL_W - ATT_HEADS)))

    def page_spec(shape, j):
        def imap(b, blk, pt):
            return (pt[b * n_pages + (n_pages - 1 - (blk * pps + j))],) + (0,) * (len(shape) - 1)
        return pl.BlockSpec(shape, imap)

    col_spec = pl.BlockSpec((1, HEAD_DIM, ATT_HEADS), lambda b, blk, pt: (b, 0, 0))
    grid_spec = pltpu.PrefetchScalarGridSpec(
        num_scalar_prefetch=1,
        grid=(Bd, n_pages // pps),
        in_specs=([col_spec, col_spec, col_spec,
                   pl.BlockSpec((1, 1, SMALL_W), lambda b, blk, pt: (b, 0, 0)),
                   pl.BlockSpec((1, SMALL_W), lambda b, blk, pt: (0, 0))]
                  + [page_spec((1, ATT_HEADS, HEAD_DIM, page), j) for j in range(pps)]
                  + [page_spec((1, ATT_HEADS, HEAD_DIM, page), j) for j in range(pps)]
                  + [page_spec((1, ATT_HEADS, page), j) for j in range(pps)]),
        out_specs=[pl.BlockSpec((1, 1, ATT_WIDTH), lambda b, blk, pt: (b, 0, 0)),
                   pl.BlockSpec((1, 1, ATT_HEADS), lambda b, blk, pt: (b, 0, 0))],
        scratch_shapes=[pltpu.VMEM((ATT_HEADS, HEAD_DIM, page), F32), pltpu.VMEM((ATT_HEADS, page), F32),
                        pltpu.VMEM((ATT_HEADS, page), F32), pltpu.VMEM((ATT_HEADS, HEAD_DIM, page), F32),
                        pltpu.VMEM((ATT_HEADS, page), F32), pltpu.VMEM((pps, ATT_HEADS, page), F32)],
    )
    return pl.pallas_call(
        functools.partial(_decode_attn_kernel, pps=pps),
        grid_spec=grid_spec,
        out_shape=[jax.ShapeDtypeStruct((Bd, 1, ATT_WIDTH), F32),
                   jax.ShapeDtypeStruct((Bd, 1, ATT_HEADS), F32)],
        compiler_params=_cparams("arbitrary", "arbitrary"),
        name="fox_decode",
    )(pt_flat, q_t, kn_t, vn_t, small, bf2, *([cache_k_t] * pps), *([cache_v_t] * pps), *([cache_lf_t] * pps))


SSM_STEP_BATCH = 4


def _ssm_step_kernel(xbc_ref, sc_ref, sm_ref, z_ref, h0_ref, cw_ref, cb_ref, dtb_ref, alog_ref, e_ref,
                     dx_ref, nw_ref, y_ref, st_ref):
    for i in range(xbc_ref.shape[0]):
        _ssm_step_one(i, xbc_ref, sc_ref, sm_ref, z_ref, h0_ref, cw_ref, cb_ref, dtb_ref, alog_ref, e_ref,
                      dx_ref, nw_ref, y_ref, st_ref)


def _ssm_step_one(i, xbc_ref, sc_ref, sm_ref, z_ref, h0_ref, cw_ref, cb_ref, dtb_ref, alog_ref, e_ref,
                  dx_ref, nw_ref, y_ref, st_ref):
    H = SSM_HEADS
    rows = [xbc_ref[i]] + [sc_ref[i, CONV_WIDTH - 1 - j:CONV_WIDTH - j, :] for j in range(1, CONV_WIDTH)]
    xc = _conv_silu_rows(rows, cw_ref, cb_ref)
    xs = xc[:, :SSM_WIDTH]
    e = e_ref[...]
    dt = _softplus(sm_ref[i] + dtb_ref[...])
    da = jnp.exp(dt * (-jnp.exp(alog_ref[...])))
    both = _expand_heads(jnp.concatenate([jnp.broadcast_to(dt, (SUBLANES, SMALL_W)),
                                          jnp.broadcast_to(da, (SUBLANES, SMALL_W))], axis=0), e)
    dt_x = both[0:1, :]
    da_x = both[SUBLANES:SUBLANES + 1, :]
    xdt = xs * dt_x

    sub = lax.broadcasted_iota(I32, (H, SSM_WIDTH), 0)
    own = sub == lax.broadcasted_iota(I32, (H, SSM_WIDTH), 1) // SSM_HEAD_DIM

    def masked_parts(v):
        m = jnp.where(own, jnp.broadcast_to(v, (H, SSM_WIDTH)), 0.0)
        hi = m.astype(BF16)
        return hi, (m - hi.astype(F32)).astype(BF16)

    da_hi, da_lo = masked_parts(da_x)
    x_hi, x_lo = masked_parts(xdt)
    lhs = jnp.concatenate([da_hi, da_lo, x_hi, x_lo], axis=0)
    hpg = H // SSM_GROUPS
    grp = lax.broadcasted_iota(I32, (H, D_STATE), 0) // hpg
    b_rows = jnp.zeros((H, D_STATE), F32)
    c_rows = jnp.zeros((H, D_STATE), F32)
    for g in range(SSM_GROUPS):
        bg = xc[:, SSM_WIDTH + g * D_STATE:SSM_WIDTH + (g + 1) * D_STATE]
        cg = xc[:, SSM_WIDTH + (SSM_GROUPS + g) * D_STATE:SSM_WIDTH + (SSM_GROUPS + g + 1) * D_STATE]
        b_rows = jnp.where(grp == g, jnp.broadcast_to(bg, (H, D_STATE)), b_rows)
        c_rows = jnp.where(grp == g, jnp.broadcast_to(cg, (H, D_STATE)), c_rows)
    ones = jnp.ones((2 * H, D_STATE), BF16)
    zeros = jnp.zeros((2 * H, D_STATE), BF16)
    b_bf = b_rows.astype(BF16)
    rhs = jnp.concatenate([jnp.concatenate([ones, zeros], axis=1),
                           jnp.concatenate([zeros, jnp.concatenate([b_bf, b_bf], axis=0)], axis=1)], axis=0)
    mix = lax.dot_general(lhs, rhs, (((0,), (0,)), ((), ())), preferred_element_type=F32)
    h0 = h0_ref[i].reshape(SSM_WIDTH, D_STATE)
    h_new = mix[:, :D_STATE] * h0 + mix[:, D_STATE:]
    st_ref[i] = h_new.reshape(H, SSM_HEAD_DIM, D_STATE)
    y_t = lax.dot_general(c_rows.astype(BF16), h_new.astype(BF16), (((1,), (1,)), ((), ())),
                          preferred_element_type=F32)
    y = jnp.sum(jnp.where(own, y_t, 0.0), axis=0, keepdims=True) + dx_ref[...] * xs
    gated = y * _silu(z_ref[i].astype(F32))
    y_ref[i] = _rmsnorm_rows(gated, nw_ref[...]).astype(y_ref.dtype)


def _ssm_step(xbc, state_conv, small, z, state_ssm, conv_w, conv_b, dtb, alog, dx, e, ssm_norm_w):
    Bd = xbc.shape[0]
    bb = math.gcd(SSM_STEP_BATCH, Bd)
    full = lambda a: pl.BlockSpec(a.shape, lambda b: (0,) * a.ndim)
    row = lambda w: pl.BlockSpec((bb, 1, w), lambda b: (b, 0, 0))
    st_spec = pl.BlockSpec((bb, SSM_HEADS, SSM_HEAD_DIM, D_STATE), lambda b: (b, 0, 0, 0))
    cb = conv_b.reshape(1, CONV_DIM)
    nw = ssm_norm_w.reshape(1, SSM_WIDTH)
    return pl.pallas_call(
        _ssm_step_kernel,
        grid=(Bd // bb,),
        in_specs=[row(CONV_DIM), pl.BlockSpec((bb, CONV_WIDTH - 1, CONV_DIM), lambda b: (b, 0, 0)),
                  row(SMALL_W), row(SSM_WIDTH), st_spec,
                  full(conv_w), full(cb), full(dtb), full(alog), full(e), full(dx), full(nw)],
        out_specs=[row(SSM_WIDTH), st_spec],
        out_shape=[jax.ShapeDtypeStruct((Bd, 1, SSM_WIDTH), BF16),
                   jax.ShapeDtypeStruct(state_ssm.shape, F32)],
        compiler_params=_cparams("arbitrary"),
        name="ssm_step",
    )(xbc, state_conv, small, z, state_ssm, conv_w, cb, dtb, alog, e, dx, nw)


def _outproj_kernel(att_ref, ssm_ref, x_ref, g1_ref, sh2_ref, sc2_ref, n2_ref, wa_ref, wsm_ref, rw_ref, rb_ref,
                    cin_ref, h2_prev_ref, x1_ref, h2_ref, rt_ref, tg_ref, cout_ref, cnt_ref):
    del h2_prev_ref

    @pl.when((pl.program_id(0) == 0) & (pl.program_id(1) == 0))
    def _():
        cnt_ref[...] = cin_ref[...]

    y = (jnp.dot(att_ref[0], wa_ref[...], preferred_element_type=F32)
         + jnp.dot(ssm_ref[0], wsm_ref[...], preferred_element_type=F32))
    x1 = x_ref[0] + g1_ref[0] * y
    x1_ref[0] = x1
    h2 = _rmsnorm_rows(x1, n2_ref[...]) * (1.0 + sc2_ref[0]) + sh2_ref[0]
    h2_ref[...] = _pack_bf16_pairs(h2)
    logits = _dot_split(h2, h2.astype(BF16), rw_ref) + rb_ref[...]
    tm = logits.shape[0]
    lane = lax.broadcasted_iota(I32, (tm, LANES), 1).astype(F32)
    cur = logits
    idxs = []
    val_tile = jnp.full((tm, LANES), NEG_BIG, F32)
    chosen = jnp.zeros((tm, LANES), F32)
    for k in range(TOP_K):
        m = jnp.max(cur, axis=1, keepdims=True)
        idx = jnp.min(jnp.where(cur == m, lane, float(LANES)), axis=1, keepdims=True)
        idxs.append(idx)
        val_tile = jnp.where(lane == float(k), m, val_tile)
        hit = lane == idx
        chosen = jnp.where(hit, 1.0, chosen)
        cur = jnp.where(hit, 2.0 * NEG_BIG, cur)
    top = jnp.max(val_tile, axis=1, keepdims=True)
    ex = jnp.exp2((val_tile - top) * LOG2E)
    tg_ref[0] = ex / jnp.sum(ex, axis=1, keepdims=True)

    before = (lax.broadcasted_iota(I32, (tm, tm), 1) < lax.broadcasted_iota(I32, (tm, tm), 0)).astype(BF16)
    rank = jnp.dot(before, chosen.astype(BF16), preferred_element_type=F32) + cnt_ref[0:1, :]
    cnt_ref[0:1, :] = cnt_ref[0:1, :] + jnp.sum(chosen, axis=0, keepdims=True)
    cout_ref[...] = cnt_ref[...]
    route = jnp.zeros((tm, LANES), F32)
    for k in range(TOP_K):
        rank_k = jnp.sum(jnp.where(lane == idxs[k], rank, 0.0), axis=1, keepdims=True)
        route = jnp.where(lane == float(k), idxs[k], route)
        route = jnp.where(lane == float(TOP_K + k), rank_k, route)
    rt_ref[0] = route.astype(I32)


def _out_proj(att, ssm, x, g1, sh2, sc2, norm2_w, wa, wsm, rw, rb, counts_in, h2_all, row0, tm):
    B, L, _ = x.shape
    steps = L // tm
    assert row0 % tm == 0
    per_row = g1.shape[1] != 1
    mod_spec = (pl.BlockSpec((1, tm, D_MODEL), lambda b, i: (b, i, 0)) if per_row
                else pl.BlockSpec((1, 1, D_MODEL), lambda b, i: (b, 0, 0)))
    full = lambda a: pl.BlockSpec(a.shape, lambda b, i: (0,) * a.ndim)
    row = lambda w: pl.BlockSpec((1, tm, w), lambda b, i: (b, i, 0))
    n2 = norm2_w.reshape(1, D_MODEL)
    cnt_spec = pl.BlockSpec((SUBLANES, LANES), lambda b, i: (0, 0))
    return pl.pallas_call(
        _outproj_kernel,
        grid=(B, L // tm),
        in_specs=[row(ATT_WIDTH), row(SSM_WIDTH), row(D_MODEL), mod_spec, mod_spec, mod_spec,
                  full(n2), full(wa), full(wsm), full(rw), full(rb), cnt_spec, pl.BlockSpec(memory_space=pl.ANY)],
        out_specs=[row(D_MODEL), pl.BlockSpec((tm, D_MODEL // 2), lambda b, i: (row0 // tm + b * steps + i, 0)),
                   row(LANES), row(LANES), cnt_spec],
        out_shape=[jax.ShapeDtypeStruct((B, L, D_MODEL), F32), jax.ShapeDtypeStruct(h2_all.shape, h2_all.dtype),
                   jax.ShapeDtypeStruct((B, L, LANES), I32), jax.ShapeDtypeStruct((B, L, LANES), F32),
                   jax.ShapeDtypeStruct((SUBLANES, LANES), F32)],
        input_output_aliases={12: 1},
        scratch_shapes=[pltpu.VMEM((SUBLANES, LANES), F32)],
        compiler_params=_cparams("arbitrary", "arbitrary"),
        name="out_proj_route",
    )(att, ssm, x, g1, sh2, sc2, n2, wa, wsm, rw, rb, counts_in, h2_all)


def _moe_block_tables(counts, n_blocks):
    tb = MOE_ROWS
    nb = (counts + tb - 1) // tb
    cum = jnp.cumsum(nb)
    blk_start = cum - nb
    b = jnp.arange(n_blocks, dtype=I32)
    block_expert = jnp.minimum(jnp.sum((cum[None, :] <= b[:, None]).astype(I32), axis=1), N_EXPERTS - 1)
    rows_left = counts[block_expert] - (b - blk_start[block_expert]) * tb
    block_rows = jnp.where(b < cum[-1], jnp.clip(rows_left, 0, tb), 0).astype(I32)
    prev = jnp.concatenate([jnp.full((1,), -1, I32), block_expert[:-1]])
    block_first = ((block_expert != prev) & (block_rows > 0)).astype(I32)
    return (blk_start * tb).astype(I32), block_expert.astype(I32), block_first, block_rows


SC_WORKERS = 32
SC_SCATTER_WINDOW = 48
SC_WINDOW = 64


def _sc_scatter_rows(src, slots, n_rows):
    T, d = src.shape
    win = SC_SCATTER_WINDOW
    steps = T // win * TOP_K
    assert T % win == 0 and steps % SC_WORKERS == 0
    idx = slots.reshape(T // win, win, TOP_K).transpose(0, 2, 1).reshape(steps, win)
    idx_rows = jnp.pad(idx, ((0, 0), (0, LANES - win)))
    mesh = plsc.VectorSubcoreMesh(core_axis_name="c", subcore_axis_name="s")

    @functools.partial(pl.kernel, out_type=jax.ShapeDtypeStruct((n_rows, d), src.dtype), mesh=mesh, name="moe_scatter")
    def scatter(x_hbm, i_hbm, o_hbm):
        def body(x_vmem, i_vmem):
            pltpu.sync_copy(x_vmem, o_hbm.at[i_vmem.at[0, pl.ds(0, win)]])

        pltpu.emit_pipeline(body, grid=(steps,),
                            in_specs=[pl.BlockSpec((win, d), lambda i: (i // TOP_K, 0)),
                                      pl.BlockSpec((1, LANES), lambda i: (i, 0))],
                            out_specs=[],
                            core_axis_name=("c", "s"), dimension_semantics=(pltpu.PARALLEL,))(x_hbm, i_hbm)

    return scatter(src, idx_rows)


def _moe_kernel(be_ref, first_ref, rows_ref, x_ref, wgu_ref, bgu_ref, wd_ref, bd_ref, o_ref, wgu_s, wd_s):
    i = pl.program_id(0)

    @pl.when(first_ref[i] == 1)
    def _():
        wgu_s[...] = wgu_ref[0].astype(BF16)
        wd_s[...] = wd_ref[0].astype(BF16)

    @pl.when(rows_ref[i] > 0)
    def _():
        live = lax.broadcasted_iota(I32, x_ref.shape, 0) < rows_ref[i]
        x = _unpack_bf16_pairs(jnp.where(live, x_ref[...], jnp.uint32(0))).astype(BF16)
        gu = jnp.dot(x, wgu_s[...], preferred_element_type=F32) + bgu_ref[0]
        g = jnp.minimum(gu[:, :D_FF], SWIGLU_LIMIT)
        u = jnp.clip(gu[:, D_FF:], -SWIGLU_LIMIT, SWIGLU_LIMIT)
        act = (u + 1.0) * (g * jax.nn.sigmoid(SWIGLU_ALPHA * g))
        o_ref[...] = _pack_bf16_pairs(jnp.dot(act.astype(BF16), wd_s[...], preferred_element_type=F32) + bd_ref[0])

    @pl.when(rows_ref[i] == 0)
    def _():
        o_ref[...] = jnp.zeros_like(o_ref)


def _moe_blocks(block_expert, block_first, block_rows, x_sorted, w_gate_up, b_gate_up, w_down, b_down):
    n_rows = x_sorted.shape[0]
    tb = MOE_ROWS
    grid_spec = pltpu.PrefetchScalarGridSpec(
        num_scalar_prefetch=3,
        grid=(n_rows // tb,),
        in_specs=[pl.BlockSpec((tb, D_MODEL // 2), lambda i, be, bf, br: (i, 0)),
                  pl.BlockSpec((1, D_MODEL, 2 * D_FF), lambda i, be, bf, br: (be[i], 0, 0)),
                  pl.BlockSpec((1, 1, 2 * D_FF), lambda i, be, bf, br: (be[i], 0, 0)),
                  pl.BlockSpec((1, D_FF, D_MODEL), lambda i, be, bf, br: (be[i], 0, 0)),
                  pl.BlockSpec((1, 1, D_MODEL), lambda i, be, bf, br: (be[i], 0, 0))],
        out_specs=pl.BlockSpec((tb, D_MODEL // 2), lambda i, be, bf, br: (i, 0)),
        scratch_shapes=[pltpu.VMEM((D_MODEL, 2 * D_FF), BF16), pltpu.VMEM((D_FF, D_MODEL), BF16)],
    )
    return pl.pallas_call(
        _moe_kernel,
        grid_spec=grid_spec,
        out_shape=jax.ShapeDtypeStruct((n_rows, D_MODEL // 2), jnp.uint32),
        compiler_params=_cparams("arbitrary"),
        name="moe_experts",
    )(block_expert, block_first, block_rows, x_sorted, w_gate_up,
      b_gate_up.reshape(N_EXPERTS, 1, 2 * D_FF), w_down, b_down.reshape(N_EXPERTS, 1, D_MODEL))


def _sc_gather_rows(table, idx):
    n = idx.shape[0]
    d = table.shape[1]
    win = SC_WINDOW
    idx_rows = jnp.pad(idx.reshape(n // win, win), ((0, 0), (0, LANES - win)))
    mesh = plsc.VectorSubcoreMesh(core_axis_name="c", subcore_axis_name="s")

    @functools.partial(pl.kernel, out_type=jax.ShapeDtypeStruct((n, d), table.dtype), mesh=mesh, name="moe_gather")
    def gather(t_hbm, i_hbm, o_hbm):
        def body(i_vmem, o_vmem):
            pltpu.sync_copy(t_hbm.at[i_vmem.at[0, pl.ds(0, win)]], o_vmem)

        pltpu.emit_pipeline(body, grid=(n // win,),
                            in_specs=[pl.BlockSpec((1, LANES), lambda i: (i, 0))],
                            out_specs=[pl.BlockSpec((win, d), lambda i: (i, 0))],
                            core_axis_name=("c", "s"), dimension_semantics=(pltpu.PARALLEL,))(i_hbm, o_hbm)

    return gather(table, idx_rows)


def _final_kernel(x1_ref, tg_ref, g2_ref, nw_ref, y0_ref, y1_ref, y2_ref, y3_ref, o_ref):
    gates = tg_ref[0]
    moe = gates[:, 0:1] * _unpack_bf16_pairs(y0_ref[...])
    for k, y_ref in enumerate((y1_ref, y2_ref, y3_ref), start=1):
        moe = moe + gates[:, k:k + 1] * _unpack_bf16_pairs(y_ref[...])
    o_ref[0] = _rmsnorm_rows(x1_ref[0] + g2_ref[0] * moe, nw_ref[...])


def _moe_final(x1, gates, g2, final_norm_w, y_rows, row0, tt):
    B, L, _ = x1.shape
    steps = L // tt
    blocks_per_k = B * steps
    base = row0 // tt
    assert row0 % tt == 0 and TOP_K == 4
    per_row = g2.shape[1] != 1
    mod_spec = (pl.BlockSpec((1, tt, D_MODEL), lambda b, i: (b, i, 0)) if per_row
                else pl.BlockSpec((1, 1, D_MODEL), lambda b, i: (b, 0, 0)))
    y_specs = [pl.BlockSpec((tt, D_MODEL // 2), functools.partial(lambda b, i, k: (base + k * blocks_per_k + b * steps + i, 0), k=k))
               for k in range(TOP_K)]
    nw = final_norm_w.reshape(1, D_MODEL)
    return pl.pallas_call(
        _final_kernel,
        grid=(B, steps),
        in_specs=[pl.BlockSpec((1, tt, D_MODEL), lambda b, i: (b, i, 0)),
                  pl.BlockSpec((1, tt, LANES), lambda b, i: (b, i, 0)),
                  mod_spec, pl.BlockSpec((1, D_MODEL), lambda b, i: (0, 0))] + y_specs,
        out_specs=pl.BlockSpec((1, tt, D_MODEL), lambda b, i: (b, i, 0)),
        out_shape=jax.ShapeDtypeStruct((B, L, D_MODEL), F32),
        compiler_params=_cparams("arbitrary", "arbitrary"),
        name="moe_final",
    )(x1, gates, g2, nw, y_rows, y_rows, y_rows, y_rows)


def kernel(x_prompt, x_sample, c_prompt, c_sample, cache_k, cache_v, cache_lf, state_conv, state_ssm, page_table,
           ada_w, ada_b, norm1_w, w_in, b_f, conv_w, conv_b, dt_bias, A_log, D_skip, ssm_norm_w, w_out,
           norm2_w, router_w, router_b, w_gate_up, b_gate_up, w_down, b_down, final_norm_w):
    assert ada_w.shape[0] == 1, "single-layer trunk"
    B, L, D = x_prompt.shape
    Bd = x_sample.shape[0]
    assert x_sample.shape[1] == 1 and L % SSD_CHUNK == 0

    n_c = B + Bd
    rows = -(-n_c // SUBLANES) * SUBLANES
    c_all = jnp.concatenate([c_prompt, c_sample, jnp.zeros((rows - n_c, D), F32)], axis=0)
    mod = _modulation(c_all, ada_w[0], ada_b[0])
    mod_p = [m.reshape(B, 1, D) for m in jnp.split(mod[:B], 6, axis=-1)]
    mod_s = [m.reshape(1, Bd, D) for m in jnp.split(mod[B:n_c], 6, axis=-1)]

    w = w_in[0]
    o_f = 3 * ATT_WIDTH
    o_z = o_f + ATT_HEADS
    o_x = o_z + SSM_WIDTH
    o_dt = o_x + CONV_DIM
    w_small = jnp.concatenate([w[:, o_f:o_z], w[:, o_dt:o_dt + SSM_HEADS],
                               jnp.zeros((D, SMALL_W - ATT_HEADS - SSM_HEADS), F32)], axis=1)
    wts = (w[:, :ATT_WIDTH].astype(BF16), w[:, ATT_WIDTH:2 * ATT_WIDTH].astype(BF16),
           w[:, 2 * ATT_WIDTH:o_f].astype(BF16), w[:, o_z:o_x].astype(BF16), w[:, o_x:o_dt].astype(BF16), _split_weight(w_small))
    wa = w_out[0][:ATT_WIDTH].astype(BF16)
    wsm = w_out[0][ATT_WIDTH:].astype(BF16)
    rw = _split_weight(jnp.pad(router_w[0], ((0, 0), (0, LANES - N_EXPERTS))))
    rb = jnp.pad(router_b[0].reshape(1, N_EXPERTS), ((0, 0), (0, LANES - N_EXPERTS)), constant_values=NEG_BIG)
    dtb, alog, dx = _ssm_params(dt_bias[0], A_log[0], D_skip[0])
    e = jnp.asarray(_head_expander(), BF16)

    tm_p = min(512, L)
    qp, kp, vp, k_p, v_p, z_p, xbc_p, small_p, lf_p = _in_proj(x_prompt, mod_p[0], mod_p[1], norm1_w[0], wts,
                                                              b_f[0], tm_p)
    att_p = _flash_attention(qp, kp, vp, min(1024, L), min(512, L))
    ssm_p, st_p = _ssd_prompt(xbc_p, small_p, z_p, conv_w[0], conv_b[0], dtb, alog, dx, e, ssm_norm_w[0])
    n_tok = B * L + Bd
    zero_counts = jnp.zeros((SUBLANES, LANES), F32)
    h2_all = jnp.zeros((n_tok, D // 2), jnp.uint32)
    x1_p, h2_all, rt_p, tg_p, counts_p = _out_proj(att_p, ssm_p, x_prompt, mod_p[2], mod_p[3], mod_p[4], norm2_w[0],
                                                   wa, wsm, rw, rb, zero_counts, h2_all, 0, min(512, L))

    xs_rows = x_sample.reshape(1, Bd, D)
    qp_s, _, _, k_s, v_s, z_s, xbc_s, small_s, _ = _in_proj(xs_rows, mod_s[0], mod_s[1], norm1_w[0], wts, b_f[0], Bd)
    per_row = lambda a: a.reshape(Bd, 1, a.shape[-1])
    head_cols = lambda a: a.reshape(Bd, ATT_HEADS, HEAD_DIM).transpose(0, 2, 1).astype(F32)
    q_halves = qp_s.astype(F32).reshape(Bd, ATT_HEADS, 2, HEAD_DIM)
    odd_head = (jnp.arange(ATT_HEADS) % 2 == 1)[None, :, None]
    q_s = jnp.where(odd_head, q_halves[:, :, 1, :], q_halves[:, :, 0, :])
    att_s4, lf_s = _decode_attention(
        page_table, head_cols(q_s), head_cols(k_s), head_cols(v_s), per_row(small_s), b_f[0],
        cache_k[0].transpose(0, 2, 3, 1), cache_v[0].transpose(0, 2, 3, 1), cache_lf[0].transpose(0, 2, 1))
    att_s = att_s4.reshape(1, Bd, ATT_WIDTH).astype(BF16)
    ssm_s, st_s = _ssm_step(per_row(xbc_s), state_conv[0], per_row(small_s), per_row(z_s), state_ssm[0],
                            conv_w[0], conv_b[0], dtb, alog, dx, e, ssm_norm_w[0])
    x1_s, h2_all, rt_s, tg_s, counts = _out_proj(att_s, ssm_s.reshape(1, Bd, SSM_WIDTH), xs_rows, mod_s[2], mod_s[3],
                                                 mod_s[4], norm2_w[0], wa, wsm, rw, rb, counts_p, h2_all, B * L, Bd)

    n_blocks = (n_tok * TOP_K + N_EXPERTS * (MOE_ROWS - 1)) // MOE_ROWS
    pad_start, b_exp, b_first, b_rows = _moe_block_tables(counts[0, :N_EXPERTS].astype(I32), n_blocks)

    def token_slots(rt):
        e_idx = rt[..., :TOP_K].reshape(-1, TOP_K)
        first = jnp.sum(jnp.where(e_idx[..., None] == jnp.arange(N_EXPERTS, dtype=I32), pad_start, 0), axis=-1)
        return first + rt[..., TOP_K:2 * TOP_K].reshape(-1, TOP_K)

    slots_p, slots_s = token_slots(rt_p), token_slots(rt_s)
    x_sorted = _sc_scatter_rows(h2_all, jnp.concatenate([slots_p, slots_s], axis=0), n_blocks * MOE_ROWS)
    y_sorted = _moe_blocks(b_exp, b_first, b_rows, x_sorted, w_gate_up[0], b_gate_up[0], w_down[0], b_down[0])

    n_assign = n_tok * TOP_K
    chunk = SC_WINDOW * SC_WORKERS
    n_idx = -(-n_assign // chunk) * chunk
    slots = jnp.concatenate([slots_p.T.reshape(-1), slots_s.T.reshape(-1), jnp.zeros((n_idx - n_assign,), I32)])
    y_rows = _sc_gather_rows(y_sorted, slots)
    y_prompt = _moe_final(x1_p, tg_p, mod_p[5], final_norm_w, y_rows, 0, min(512, L))
    y_sample = _moe_final(x1_s, tg_s, mod_s[5], final_norm_w, y_rows, B * L * TOP_K, Bd).reshape(Bd, 1, D)

    conv_s = jnp.concatenate([state_conv[0][:, 1:], xbc_s.reshape(Bd, 1, CONV_DIM)], axis=1)
    ssm_state_p = st_p.reshape(B, D_STATE, SSM_HEADS, SSM_HEAD_DIM).transpose(0, 2, 3, 1)
    return (y_prompt, y_sample,
            k_p.reshape(1, B, L, ATT_HEADS, HEAD_DIM), v_p.reshape(1, B, L, ATT_HEADS, HEAD_DIM),
            lf_p.reshape(1, B, L, ATT_HEADS), xbc_p[:, L - (CONV_WIDTH - 1):].reshape(1, B, CONV_WIDTH - 1, CONV_DIM),
            ssm_state_p.reshape(1, B, SSM_HEADS, SSM_HEAD_DIM, D_STATE),
            k_s.reshape(1, Bd, 1, ATT_HEADS, HEAD_DIM), v_s.reshape(1, Bd, 1, ATT_HEADS, HEAD_DIM),
            lf_s.reshape(1, Bd, 1, ATT_HEADS), conv_s.reshape(1, Bd, CONV_WIDTH - 1, CONV_DIM),
            st_s.reshape(1, Bd, SSM_HEADS, SSM_HEAD_DIM, D_STATE))
```
